```python
import jax, jax.numpy as jnp
from jax import lax
import numpy as np

D_MODEL = 1024
BATCH = 8
SEQ = 8192
DEPTH = 1

CHUNK = 64
Q_BLOCK = 128
D_FF = 2816
HG_HEADS = 8
HG_HEAD_K = 128
HG_HEAD_V = 128
HG_WIDTH = HG_HEADS * HG_HEAD_K
HG_VWIDTH = HG_HEADS * HG_HEAD_V
MLA_HEADS = 8
MLA_NOPE = 128
MLA_ROPE = 64
MLA_V = 128
MLA_QK = MLA_NOPE + MLA_ROPE
Q_LORA = 384
KV_LORA = 256
ROPE_THETA = 10000.0
EPS = 1e-6
IN_SPLITS = (HG_WIDTH, HG_WIDTH, HG_VWIDTH, HG_VWIDTH, Q_LORA, KV_LORA, MLA_ROPE)
IN_COLS = 2 * HG_WIDTH + 2 * HG_VWIDTH + Q_LORA + KV_LORA + MLA_ROPE

kernel_name = 'hybrid_hgrn2_mla_macaron'


def _rms_norm(x, gain):
    xf = x.astype(jnp.float32)
    y = xf * lax.rsqrt(jnp.mean(xf * xf, axis=-1, keepdims=True) + EPS)
    return (y * gain.astype(jnp.float32)).astype(x.dtype)


def _swiglu(x, w_in, w_out):
    gate, up = jnp.split(x @ w_in, 2, axis=-1)
    return (jax.nn.silu(gate) * up) @ w_out


def _rotate(x, cos, sin):
    half = x.shape[-1] // 2
    x1, x2 = x[..., :half], x[..., half:]
    return jnp.concatenate([x1 * cos - x2 * sin, x2 * cos + x1 * sin], axis=-1)


def _to_chunks(t):
    b, s, h, d = t.shape
    return t.reshape(b, s // CHUNK, CHUNK, h, d).transpose(1, 0, 3, 2, 4)


def _from_chunks(t):
    n, b, h, c, d = t.shape
    return t.transpose(1, 0, 3, 2, 4).reshape(b, n * c, h, d)


def _hgrn2_chunk_step(state, inputs):
    q, k, v, log_f = inputs
    cum = jnp.cumsum(log_f, axis=2)
    o_inter = jnp.einsum('bhtk,bhkv->bhtv', q * jnp.exp(cum), state)
    causal = jnp.tril(jnp.ones((CHUNK, CHUNK), dtype=bool))[:, :, None]
    rel = cum[:, :, :, None, :] - cum[:, :, None, :, :]
    decay = jnp.exp(jnp.where(causal, rel, -jnp.inf))
    scores = jnp.einsum('bhtk,bhtsk,bhsk->bhts', q, decay, k)
    o_intra = jnp.einsum('bhts,bhsv->bhtv', scores, v)
    last = cum[:, :, -1, :]
    new_state = jnp.exp(last)[..., None] * state + jnp.einsum(
        'bhsk,bhsv->bhkv', k * jnp.exp(last[:, :, None, :] - cum), v)
    return new_state, o_intra + o_inter


def _hgrn2(q_raw, f_raw, i_raw, g_raw, lower_bound, out_gain):
    b, s, _ = q_raw.shape
    f32 = jnp.float32
    q = jax.nn.silu(q_raw.astype(f32)).reshape(b, s, HG_HEADS, HG_HEAD_K)
    z = f_raw.astype(f32).reshape(b, s, HG_HEADS, HG_HEAD_K)
    lb = lower_bound.astype(f32).reshape(HG_HEADS, HG_HEAD_K)
    log_f = jnp.logaddexp(jnp.log(lb), jnp.log1p(-lb) + jax.nn.log_sigmoid(z))
    k = -jnp.expm1(log_f)
    v = i_raw.astype(f32).reshape(b, s, HG_HEADS, HG_HEAD_V)
    state0 = jnp.zeros((b, HG_HEADS, HG_HEAD_K, HG_HEAD_V), f32)
    _, o = lax.scan(_hgrn2_chunk_step, state0,
                    (_to_chunks(q), _to_chunks(k), _to_chunks(v), _to_chunks(log_f)))
    o = _rms_norm(_from_chunks(o), out_gain)
    o = o * jax.nn.silu(g_raw.astype(f32)).reshape(b, s, HG_HEADS, HG_HEAD_V)
    return o.reshape(b, s, HG_VWIDTH).astype(q_raw.dtype)


def _mla(c_q, c_kv, k_pe, positions, q_lora_gain, w_q_up, kv_lora_gain, w_kv_up,
         q_head_gain, k_head_gain):
    b, s, _ = c_q.shape
    q = (_rms_norm(c_q, q_lora_gain) @ w_q_up).reshape(b, s, MLA_HEADS, MLA_QK)
    kv = (_rms_norm(c_kv, kv_lora_gain) @ w_kv_up).reshape(b, s, MLA_HEADS, MLA_NOPE + MLA_V)
    k_nope, v = kv[..., :MLA_NOPE], kv[..., MLA_NOPE:]
    k = jnp.concatenate(
        [k_nope, jnp.broadcast_to(k_pe[:, :, None, :], (b, s, MLA_HEADS, MLA_ROPE))], axis=-1)
    q = _rms_norm(q, q_head_gain)
    k = _rms_norm(k, k_head_gain)
    inv_freq = ROPE_THETA ** (-jnp.arange(0, MLA_ROPE, 2, dtype=jnp.float32) / MLA_ROPE)
    ang = positions.astype(jnp.float32)[:, :, None, None] * inv_freq
    cos = jnp.cos(ang).astype(q.dtype)
    sin = jnp.sin(ang).astype(q.dtype)
    q = jnp.concatenate([q[..., :MLA_NOPE], _rotate(q[..., MLA_NOPE:], cos, sin)], axis=-1)
    k = jnp.concatenate([k[..., :MLA_NOPE], _rotate(k[..., MLA_NOPE:], cos, sin)], axis=-1)
    n_blocks = s // Q_BLOCK
    q_blocks = q.reshape(b, n_blocks, Q_BLOCK, MLA_HEADS, MLA_QK).transpose(1, 0, 2, 3, 4)
    key_chunk = jnp.arange(s) // CHUNK
    scale = MLA_QK ** -0.5

    def attend(args):
        q_blk, blk = args
        q_chunk = (blk * Q_BLOCK + jnp.arange(Q_BLOCK)) // CHUNK
        allowed = key_chunk[None, :] <= q_chunk[:, None]
        logits = jnp.einsum('bqhd,bkhd->bhqk', q_blk, k).astype(jnp.float32) * scale
        logits = jnp.where(allowed, logits, -jnp.inf)
        probs = jax.nn.softmax(logits, axis=-1).astype(v.dtype)
        return jnp.einsum('bhqk,bkhd->bqhd', probs, v)

    o = lax.map(attend, (q_blocks, jnp.arange(n_blocks)))
    return o.transpose(1, 0, 2, 3, 4).reshape(b, s, MLA_HEADS * MLA_V)


def setup_inputs(seed: int = 0) -> dict:
    key = jax.random.key(seed)
    ks = jax.random.split(key, 24)
    f32 = jnp.float32
    L = DEPTH

    def w(k, shape, fan_in):
        return jax.random.normal(k, shape, f32) * (fan_in ** -0.5)

    def gain(k, shape):
        return 1.0 + 0.05 * jax.random.normal(k, shape, f32)

    x = jax.random.normal(ks[0], (BATCH, SEQ, D_MODEL), f32)
    offsets = jax.random.randint(ks[1], (BATCH, 1), 0, 64, dtype=jnp.int32) * CHUNK
    positions = (offsets + jnp.arange(SEQ, dtype=jnp.int32)[None, :]).astype(jnp.int32)
    return {
        'x': x,
        'positions': positions,
        'ffn1_norm': gain(ks[2], (L, D_MODEL)),
        'ffn1_w_in': w(ks[3], (L, D_MODEL, 2 * D_FF), D_MODEL),
        'ffn1_w_out': w(ks[4], (L, D_FF, D_MODEL), D_FF),
        'mix_norm': gain(ks[5], (L, D_MODEL)),
        'w_in': w(ks[6], (L, D_MODEL, IN_COLS), D_MODEL),
        'hg_lb_table': 0.5 * jax.random.normal(ks[7], (L + 1, HG_WIDTH), f32),
        'hg_out_norm': gain(ks[8], (L, HG_HEAD_V)),
        'w_hg_branch': w(ks[9], (L, HG_VWIDTH, D_MODEL), HG_VWIDTH),
        'mla_q_lora_norm': gain(ks[10], (L, Q_LORA)),
        'w_q_up': w(ks[11], (L, Q_LORA, MLA_HEADS * MLA_QK), Q_LORA),
        'mla_kv_lora_norm': gain(ks[12], (L, KV_LORA)),
        'w_kv_up': w(ks[13], (L, KV_LORA, MLA_HEADS * (MLA_NOPE + MLA_V)), KV_LORA),
        'q_head_norm': gain(ks[14], (L, MLA_QK)),
        'k_head_norm': gain(ks[15], (L, MLA_QK)),
        'w_mla_branch': w(ks[16], (L, MLA_HEADS * MLA_V, D_MODEL), MLA_HEADS * MLA_V),
        'w_merge': w(ks[17], (L, D_MODEL, 2 * D_MODEL), D_MODEL),
        'b_merge': 0.02 * jax.random.normal(ks[18], (L, 2 * D_MODEL), f32),
        'w_out': w(ks[19], (L, D_MODEL, D_MODEL), D_MODEL),
        'ffn2_norm': gain(ks[20], (L, D_MODEL)),
        'ffn2_w_in': w(ks[21], (L, D_MODEL, 2 * D_FF), D_MODEL),
        'ffn2_w_out': w(ks[22], (L, D_FF, D_MODEL), D_FF),
        'final_norm': gain(ks[23], (L, D_MODEL)),
    }


def reference(x, positions, ffn1_norm, ffn1_w_in, ffn1_w_out, mix_norm, w_in, hg_lb_table,
              hg_out_norm, w_hg_branch, mla_q_lora_norm, w_q_up, mla_kv_lora_norm, w_kv_up,
              q_head_norm, k_head_norm, w_mla_branch, w_merge, b_merge, w_out,
              ffn2_norm, ffn2_w_in, ffn2_w_out, final_norm):
    lower_bounds = jnp.cumsum(jax.nn.softmax(hg_lb_table.astype(jnp.float32), axis=0), axis=0)
    split_at = np.cumsum(IN_SPLITS)[:-1].tolist()
    h = x
    for l in range(DEPTH):
        h = h + 0.5 * _swiglu(_rms_norm(h, ffn1_norm[l]), ffn1_w_in[l], ffn1_w_out[l])
        u = _rms_norm(h, mix_norm[l])
        hg_q, hg_f, hg_i, hg_g, c_q, c_kv, k_pe = jnp.split(u @ w_in[l], split_at, axis=-1)
        y_hg = _hgrn2(hg_q, hg_f, hg_i, hg_g, lower_bounds[l], hg_out_norm[l]) @ w_hg_branch[l]
        y_mla = _mla(c_q, c_kv, k_pe, positions, mla_q_lora_norm[l], w_q_up[l],
                     mla_kv_lora_norm[l], w_kv_up[l], q_head_norm[l], k_head_norm[l]) @ w_mla_branch[l]
        g_hg, g_mla = jnp.split(jax.nn.sigmoid(u @ w_merge[l] + b_merge[l]), 2, axis=-1)
        h = h + (g_hg * y_hg + g_mla * y_mla) @ w_out[l]
        h = h + 0.5 * _swiglu(_rms_norm(h, ffn2_norm[l]), ffn2_w_in[l], ffn2_w_out[l])
        h = _rms_norm(h, final_norm[l])
    return h
```

```python
import functools

import numpy as np
import jax
import jax.numpy as jnp
from jax import lax
from jax.experimental import pallas as pl
from jax.experimental.pallas import tpu as pltpu

F32 = jnp.float32
BF16 = jnp.bfloat16

EPS = 1e-6
CHUNK = 64
HG_HEAD = 128
MLA_NOPE = 128
MLA_ROPE = 64
MLA_V = 128
MLA_QK = MLA_NOPE + MLA_ROPE
MLA_HEAD_PAD = 256
ROPE_THETA = 10000.0

LANES = 128
TOKEN_TILE = 512
FF_CHUNK = 256
HG_BLOCK = 512
ATT_TILE = 512
VMEM_LIMIT = 56 * 1024 * 1024


def _rms(x, gain):
    return x * lax.rsqrt(jnp.mean(x * x, axis=-1, keepdims=True) + EPS) * gain


def _silu(x):
    return x * jax.nn.sigmoid(x)


def _dot(a, b):
    return jnp.dot(a, b, preferred_element_type=F32)


def _dot_nt(a, b):
    return lax.dot_general(a, b, (((1,), (1,)), ((), ())), preferred_element_type=F32)


def _dot_tn(a, b):
    return lax.dot_general(a, b, (((0,), (0,)), ((), ())), preferred_element_type=F32)


def _resident(shape):
    zeros = (0,) * len(shape)
    return pl.BlockSpec(shape, lambda *_: zeros, pipeline_mode=pl.Buffered(1))


def _params(semantics):
    return pltpu.CompilerParams(dimension_semantics=semantics, vmem_limit_bytes=VMEM_LIMIT)


def _ffn_body(x_ref, gin_ref, win_ref, wout_ref, gout_ref, *rest, emit_h):
    if emit_h:
        h_ref, n_ref, a_ref = rest
    else:
        n_ref, a_ref = rest
    n_chunks, _, two_fc = win_ref.shape
    fc = two_fc // 2
    x = x_ref[...]
    xn = _rms(x, gin_ref[...]).astype(BF16)
    for c in range(n_chunks):
        hc = _dot(xn, win_ref[c])
        a_ref[:, c * fc:(c + 1) * fc] = (_silu(hc[:, :fc]) * hc[:, fc:]).astype(BF16)
    h = x + 0.5 * _dot(a_ref[...], wout_ref[...])
    if emit_h:
        h_ref[...] = h
    n_ref[...] = _rms(h, gout_ref[...]).astype(n_ref.dtype)


def _ffn(x, gin, w_in, w_out, gout, *, emit_h, norm_dtype):
    t, d = x.shape
    d_ff = w_out.shape[0]
    nc = d_ff // FF_CHUNK
    gate = w_in[:, :d_ff].reshape(d, nc, 1, FF_CHUNK)
    up = w_in[:, d_ff:].reshape(d, nc, 1, FF_CHUNK)
    win = jnp.concatenate([gate, up], axis=2).transpose(1, 0, 2, 3).reshape(nc, d, 2 * FF_CHUNK).astype(BF16)
    tm = TOKEN_TILE
    row = pl.BlockSpec((tm, d), lambda i: (i, 0))
    out_shape = [jax.ShapeDtypeStruct((t, d), norm_dtype)]
    out_specs = [row]
    if emit_h:
        out_shape = [jax.ShapeDtypeStruct((t, d), F32)] + out_shape
        out_specs = [row] + out_specs
    return pl.pallas_call(
        functools.partial(_ffn_body, emit_h=emit_h),
        grid=(t // tm,),
        in_specs=[row, _resident((1, d)), _resident(win.shape), _resident((d_ff, d)), _resident((1, d))],
        out_specs=out_specs,
        out_shape=out_shape,
        scratch_shapes=[pltpu.VMEM((tm, d_ff), BF16)],
        compiler_params=_params(("parallel",)),
        name="ffn1" if emit_h else "ffn2",
    )(x, gin.reshape(1, d), win, w_out.astype(BF16), gout.reshape(1, d))


def _proj_body(u_ref, w_ref, wm_ref, bm_ref, tab_ref, hq_ref, lf_ref, kk_ref, v_ref, g_ref, c_ref, gate_ref,
               *, layer):
    w = hq_ref.shape[1]
    d = u_ref.shape[1]
    u = u_ref[...]
    hq_ref[...] = _silu(_dot(u, w_ref[:, 0:w]))
    tab = tab_ref[...]
    e = jnp.exp(tab - jnp.max(tab, axis=0, keepdims=True))
    lb = jnp.sum(e[:layer + 1], axis=0, keepdims=True) / jnp.sum(e, axis=0, keepdims=True)
    z = _dot(u, w_ref[:, w:2 * w])
    ez = jnp.exp(-jnp.abs(z))
    r = 1.0 / (1.0 + ez)
    a = ez * r
    pos = z >= 0
    lf_ref[...] = jnp.log(lb + (1.0 - lb) * jnp.where(pos, r, a))
    kk_ref[...] = ((1.0 - lb) * jnp.where(pos, a, r)).astype(BF16)
    v_ref[...] = _dot(u, w_ref[:, 2 * w:3 * w]).astype(BF16)
    g_ref[...] = _silu(_dot(u, w_ref[:, 3 * w:4 * w])).astype(BF16)
    c_ref[...] = _dot(u, w_ref[:, 4 * w:])
    for j in range(2):
        gm = _dot(u, wm_ref[:, j * d:(j + 1) * d]) + bm_ref[:, j * d:(j + 1) * d]
        gate_ref[:, j * d:(j + 1) * d] = jax.nn.sigmoid(gm).astype(BF16)


def _proj(u, w_in_ext, w_merge, b_merge, lb_table, *, layer, hg_width):
    t, d = u.shape
    cols = w_in_ext.shape[1]
    c_cols = cols - 4 * hg_width
    tm = TOKEN_TILE

    def row(n):
        return pl.BlockSpec((tm, n), lambda i: (i, 0))

    return pl.pallas_call(
        functools.partial(_proj_body, layer=layer),
        grid=(t // tm,),
        in_specs=[row(d), _resident((d, cols)), _resident((d, 2 * d)), _resident((1, 2 * d)),
                  _resident(lb_table.shape)],
        out_specs=[row(hg_width), row(hg_width), row(hg_width), row(hg_width), row(hg_width), row(c_cols),
                   row(2 * d)],
        out_shape=[jax.ShapeDtypeStruct((t, hg_width), F32),
                   jax.ShapeDtypeStruct((t, hg_width), F32),
                   jax.ShapeDtypeStruct((t, hg_width), BF16),
                   jax.ShapeDtypeStruct((t, hg_width), BF16),
                   jax.ShapeDtypeStruct((t, hg_width), BF16),
                   jax.ShapeDtypeStruct((t, c_cols), F32),
                   jax.ShapeDtypeStruct((t, 2 * d), BF16)],
        compiler_params=_params(("parallel",)),
        name="proj",
    )(u, w_in_ext, w_merge.astype(BF16), b_merge.reshape(1, 2 * d), lb_table)


def _mla_prep_body(c_ref, pos_ref, gq_ref, wq_ref, gkv_ref, wkv_ref, qn_gain_ref, qr_gain_ref, kn_gain_ref,
                   kr_gain_ref, invf_ref, q_ref, k_ref, v_ref, *, heads, q_lora, kv_lora):
    tm = c_ref.shape[0]
    c = c_ref[...]
    qraw = _dot(_rms(c[:, :q_lora], gq_ref[...]).astype(BF16), wq_ref[...])
    kvraw = _dot(_rms(c[:, q_lora:q_lora + kv_lora], gkv_ref[...]).astype(BF16), wkv_ref[...])
    kpe = c[:, q_lora + kv_lora:]
    ang = pos_ref[...].astype(F32) * invf_ref[...]
    first = lax.broadcasted_iota(jnp.int32, (tm, LANES), 1) < MLA_ROPE
    cs = jnp.where(first, jnp.cos(ang), jnp.sin(ang))

    def rope(block, gain):
        t = block * (gain * cs)
        return t + pltpu.roll(t, MLA_ROPE, 1)

    def sumsq(x):
        return jnp.sum(x * x, axis=-1, keepdims=True)

    k_rope = rope(kpe, kr_gain_ref[...])
    k_pe_ss = sumsq(jnp.where(first, kpe, 0.0))
    scale = MLA_QK ** -0.5
    nope_w = heads * MLA_NOPE
    for h in range(heads):
        lo, hi = h * MLA_NOPE, (h + 1) * MLA_NOPE
        qn = qraw[:, lo:hi]
        qr = qraw[:, nope_w + lo:nope_w + hi]
        sq = lax.rsqrt((sumsq(qn) + sumsq(jnp.where(first, qr, 0.0))) / MLA_QK + EPS) * scale
        o = h * MLA_HEAD_PAD
        q_ref[:, o:o + MLA_NOPE] = (qn * sq * qn_gain_ref[...]).astype(BF16)
        q_ref[:, o + MLA_NOPE:o + MLA_HEAD_PAD] = jnp.where(first, rope(qr, qr_gain_ref[...]) * sq, 0.0).astype(BF16)
        kn = kvraw[:, lo:hi]
        sk = lax.rsqrt((sumsq(kn) + k_pe_ss) / MLA_QK + EPS)
        k_ref[:, o:o + MLA_NOPE] = (kn * sk * kn_gain_ref[...]).astype(BF16)
        k_ref[:, o + MLA_NOPE:o + MLA_HEAD_PAD] = jnp.where(first, k_rope * sk, 0.0).astype(BF16)
    v_ref[...] = kvraw[:, nope_w:].astype(BF16)


def _rot_cols(w):
    half = w.shape[-1] // 2
    return jnp.concatenate([-w[..., half:], w[..., :half]], axis=-1)


def _rot_gain(g):
    half = g.shape[-1] // 2
    return jnp.concatenate([g[..., half:], g[..., :half]], axis=-1)


def _mla_prep(c, pos, gq, w_q_up, gkv, w_kv_up, q_head_gain, k_head_gain, *, heads):
    t, c_cols = c.shape
    q_lora, kv_lora = w_q_up.shape[0], w_kv_up.shape[0]
    wq = w_q_up.reshape(q_lora, heads, MLA_QK)
    wq_rope = wq[:, :, MLA_NOPE:]
    wq_ext = jnp.concatenate(
        [wq[:, :, :MLA_NOPE].reshape(q_lora, heads * MLA_NOPE),
         jnp.concatenate([wq_rope, _rot_cols(wq_rope)], axis=-1).reshape(q_lora, heads * 2 * MLA_ROPE)],
        axis=1).astype(BF16)
    wkv = w_kv_up.reshape(kv_lora, heads, MLA_NOPE + MLA_V)
    wkv_ext = jnp.concatenate([wkv[:, :, :MLA_NOPE].reshape(kv_lora, heads * MLA_NOPE),
                               wkv[:, :, MLA_NOPE:].reshape(kv_lora, heads * MLA_V)], axis=1).astype(BF16)

    def rope_gain(g):
        return jnp.concatenate([g[MLA_NOPE:], _rot_gain(g[MLA_NOPE:])]).reshape(1, 2 * MLA_ROPE)

    inv_freq = ROPE_THETA ** (-jnp.arange(0, MLA_ROPE, 2, dtype=F32) / MLA_ROPE)
    invf = jnp.tile(inv_freq, 4).reshape(1, LANES)
    tm = TOKEN_TILE

    def row(n):
        return pl.BlockSpec((tm, n), lambda i: (i, 0))

    qk_w = heads * MLA_HEAD_PAD
    return pl.pallas_call(
        functools.partial(_mla_prep_body, heads=heads, q_lora=q_lora, kv_lora=kv_lora),
        grid=(t // tm,),
        in_specs=[row(c_cols), row(1), _resident((1, q_lora)), _resident(wq_ext.shape), _resident((1, kv_lora)),
                  _resident(wkv_ext.shape), _resident((1, MLA_NOPE)), _resident((1, LANES)),
                  _resident((1, MLA_NOPE)), _resident((1, LANES)), _resident((1, LANES))],
        out_specs=[row(qk_w), row(qk_w), row(heads * MLA_V)],
        out_shape=[jax.ShapeDtypeStruct((t, qk_w), BF16), jax.ShapeDtypeStruct((t, qk_w), BF16),
                   jax.ShapeDtypeStruct((t, heads * MLA_V), BF16)],
        compiler_params=_params(("parallel",)),
        name="mla_prep",
    )(c, pos, gq.reshape(1, q_lora), wq_ext, gkv.reshape(1, kv_lora), wkv_ext,
      q_head_gain[:MLA_NOPE].reshape(1, MLA_NOPE), rope_gain(q_head_gain),
      k_head_gain[:MLA_NOPE].reshape(1, MLA_NOPE), rope_gain(k_head_gain), invf)


def _split3(x):
    hi = x.astype(BF16)
    r = x - hi.astype(F32)
    mid = r.astype(BF16)
    lo = (r - mid.astype(F32)).astype(BF16)
    return hi, mid, lo


def _hgrn_body(q_ref, lf_ref, kk_ref, v_ref, g_ref, gain_ref, tri_ref, y_ref, state_ref):
    @pl.when(pl.program_id(2) == 0)
    def _():
        state_ref[...] = jnp.zeros_like(state_ref)

    n = CHUNK
    rows = lax.broadcasted_iota(jnp.int32, (n, HG_HEAD), 0)
    trow = lax.broadcasted_iota(jnp.int32, (n, n), 0)
    tcol = lax.broadcasted_iota(jnp.int32, (n, n), 1)
    tri = tri_ref[...]
    gain = gain_ref[...]
    state = state_ref[...]
    halves = [n >> i for i in range(1, n.bit_length())]
    pairs = {half: (trow // (2 * half) == tcol // (2 * half)) & (trow % (2 * half) >= half)
             & (tcol % (2 * half) < half) for half in halves}
    eye = trow == tcol
    for c in range(q_ref.shape[1] // n):
        sl = slice(c * n, (c + 1) * n)
        q = q_ref[0, sl, :]
        lf = lf_ref[0, sl, :]
        k = kk_ref[0, sl, :].astype(F32)
        v = v_ref[0, sl, :]
        hi, mid, lo = _split3(lf)
        cum = _dot(tri, hi) + _dot(tri, mid) + _dot(tri, lo)
        scores = jnp.where(eye, jnp.sum(q * k, axis=-1, keepdims=True), 0.0)
        for half in halves:
            size = 2 * half
            if half >= 4:
                ref = jnp.concatenate(
                    [jnp.broadcast_to(cum[b * size + half - 1:b * size + half, :], (size, HG_HEAD))
                     for b in range(n // size)], axis=0)
                dec = -jnp.abs(cum - ref)
            elif half == 2:
                r4 = rows % 4
                prev = pltpu.roll(lf, 1, 0)
                nxt = pltpu.roll(lf, n - 1, 0)
                dec = jnp.where(r4 == 3, lf + prev, jnp.where(r4 == 2, lf, jnp.where(r4 == 0, nxt, 0.0)))
            else:
                dec = jnp.where(rows % 2 == 1, lf, 0.0)
            e = jnp.exp(dec)
            sc = _dot_nt((q * e).astype(BF16), (k * e).astype(BF16))
            scores = scores + jnp.where(pairs[half], sc, 0.0)
        last = cum[n - 1:n, :]
        o = _dot(scores.astype(BF16), v) + _dot_nt((q * jnp.exp(cum)).astype(BF16), state.astype(BF16))
        kdec = (k * jnp.exp(last - cum)).astype(BF16)
        state = state * jnp.exp(last) + _dot_tn(v, kdec)
        o = _rms(o, gain) * g_ref[0, sl, :].astype(F32)
        y_ref[0, sl, :] = o.astype(y_ref.dtype)
    state_ref[...] = state


def _hgrn(hq, lf, kk, v, g, out_gain, *, batch, seq):
    width = hq.shape[1]
    heads = width // HG_HEAD
    tb = HG_BLOCK
    tri = jnp.asarray(np.tril(np.ones((CHUNK, CHUNK), np.float32)), BF16)

    def r3(a):
        return a.reshape(batch, seq, width)

    blk = pl.BlockSpec((1, tb, HG_HEAD), lambda b, h, j: (b, j, h))
    y = pl.pallas_call(
        _hgrn_body,
        grid=(batch, heads, seq // tb),
        in_specs=[blk, blk, blk, blk, blk, _resident((1, HG_HEAD)), _resident((CHUNK, CHUNK))],
        out_specs=blk,
        out_shape=jax.ShapeDtypeStruct((batch, seq, width), BF16),
        scratch_shapes=[pltpu.VMEM((HG_HEAD, HG_HEAD), F32)],
        compiler_params=_params(("parallel", "parallel", "arbitrary")),
        name="hgrn",
    )(r3(hq), r3(lf), r3(kk), r3(v), r3(g), out_gain.reshape(1, HG_HEAD), tri)
    return y.reshape(batch * seq, width)


def _attn_body(q_ref, k_ref, v_ref, o_ref):
    seq = q_ref.shape[1]
    tq = ATT_TILE
    row_chunk = lax.broadcasted_iota(jnp.int32, (tq, tq), 0) // CHUNK
    col_chunk = lax.broadcasted_iota(jnp.int32, (tq, tq), 1) // CHUNK
    allowed = col_chunk <= row_chunk

    def q_step(qi, _):
        q = q_ref[0, pl.ds(pl.multiple_of(qi * tq, tq), tq), :]

        def update(carry, kj, masked):
            m, l, acc = carry
            start = pl.multiple_of(kj * tq, tq)
            s = _dot_nt(q, k_ref[0, pl.ds(start, tq), :])
            if masked:
                s = jnp.where(allowed, s, -jnp.inf)
            m_new = jnp.maximum(m, jnp.max(s, axis=-1, keepdims=True))
            alpha = jnp.exp(m - m_new)
            p = jnp.exp(s - m_new)
            l = alpha * l + jnp.sum(p, axis=-1, keepdims=True)
            acc = alpha * acc + _dot(p.astype(BF16), v_ref[0, pl.ds(start, tq), :])
            return m_new, l, acc

        init = (jnp.full((tq, 1), -jnp.inf, F32), jnp.zeros((tq, 1), F32), jnp.zeros((tq, MLA_V), F32))
        carry = lax.fori_loop(0, qi, lambda kj, cr: update(cr, kj, False), init)
        m, l, acc = update(carry, qi, True)
        o_ref[0, pl.ds(pl.multiple_of(qi * tq, tq), tq), :] = (acc / l).astype(o_ref.dtype)
        return 0

    lax.fori_loop(0, seq // tq, q_step, 0)


def _attention(q, k, v, *, batch, seq, heads):
    qk = pl.BlockSpec((1, seq, MLA_HEAD_PAD), lambda b, h: (b, 0, h))
    vo = pl.BlockSpec((1, seq, MLA_V), lambda b, h: (b, 0, h))
    o = pl.pallas_call(
        _attn_body,
        grid=(batch, heads),
        in_specs=[qk, qk, vo],
        out_specs=vo,
        out_shape=jax.ShapeDtypeStruct((batch, seq, heads * MLA_V), BF16),
        compiler_params=_params(("parallel", "parallel")),
        name="attn",
    )(q.reshape(batch, seq, -1), k.reshape(batch, seq, -1), v.reshape(batch, seq, -1))
    return o.reshape(batch * seq, heads * MLA_V)


def _merge_body(yh_ref, ym_ref, gate_ref, h_ref, wh_ref, wm_ref, wo_ref, o_ref):
    d = h_ref.shape[1]
    mix = (gate_ref[:, :d].astype(F32) * _dot(yh_ref[...], wh_ref[...])
           + gate_ref[:, d:].astype(F32) * _dot(ym_ref[...], wm_ref[...]))
    o_ref[...] = h_ref[...] + _dot(mix.astype(BF16), wo_ref[...])


def _merge(yh, ym, gates, h, w_h, w_m, w_o):
    t, d = h.shape
    tm = TOKEN_TILE

    def row(n):
        return pl.BlockSpec((tm, n), lambda i: (i, 0))

    return pl.pallas_call(
        _merge_body,
        grid=(t // tm,),
        in_specs=[row(yh.shape[1]), row(ym.shape[1]), row(2 * d), row(d), _resident(w_h.shape),
                  _resident(w_m.shape), _resident(w_o.shape)],
        out_specs=row(d),
        out_shape=jax.ShapeDtypeStruct((t, d), F32),
        compiler_params=_params(("parallel",)),
        name="merge",
    )(yh, ym, gates, h, w_h.astype(BF16), w_m.astype(BF16), w_o.astype(BF16))


def kernel(x, positions, ffn1_norm, ffn1_w_in, ffn1_w_out, mix_norm, w_in, hg_lb_table, hg_out_norm, w_hg_branch, mla_q_lora_norm, w_q_up, mla_kv_lora_norm, w_kv_up, q_head_norm, k_head_norm, w_mla_branch, w_merge, b_merge, w_out, ffn2_norm, ffn2_w_in, ffn2_w_out, final_norm):
    batch, seq, d = x.shape
    depth = ffn1_norm.shape[0]
    hg_width = hg_lb_table.shape[1]
    heads = w_q_up.shape[2] // MLA_QK
    assert hg_out_norm.shape[1] == HG_HEAD and w_hg_branch.shape[1] == hg_width
    assert q_head_norm.shape[1] == MLA_QK and w_kv_up.shape[2] == heads * (MLA_NOPE + MLA_V)
    assert seq % max(HG_BLOCK, ATT_TILE) == 0 and (batch * seq) % TOKEN_TILE == 0
    t = batch * seq
    pos = positions.reshape(t, 1)
    h = x.reshape(t, d)
    for l in range(depth):
        kpe_w = w_in[l][:, -MLA_ROPE:]
        w_in_ext = jnp.concatenate([w_in[l], _rot_cols(kpe_w)], axis=1).astype(BF16)
        assert w_in_ext.shape[1] == 4 * hg_width + w_q_up.shape[1] + w_kv_up.shape[1] + 2 * MLA_ROPE
        h1, u = _ffn(h, ffn1_norm[l], ffn1_w_in[l], ffn1_w_out[l], mix_norm[l], emit_h=True, norm_dtype=BF16)
        hq, lf, kk, hv, hg, c, gates = _proj(u, w_in_ext, w_merge[l], b_merge[l], hg_lb_table, layer=l,
                                             hg_width=hg_width)
        q, k, v = _mla_prep(c, pos, mla_q_lora_norm[l], w_q_up[l], mla_kv_lora_norm[l], w_kv_up[l],
                            q_head_norm[l], k_head_norm[l], heads=heads)
        y_hg = _hgrn(hq, lf, kk, hv, hg, hg_out_norm[l], batch=batch, seq=seq)
        y_mla = _attention(q, k, v, batch=batch, seq=seq, heads=heads)
        h2 = _merge(y_hg, y_mla, gates, h1, w_hg_branch[l], w_mla_branch[l], w_out[l])
        (h,) = _ffn(h2, ffn2_norm[l], ffn2_w_in[l], ffn2_w_out[l], final_norm[l], emit_h=False, norm_dtype=F32)
    return h.reshape(batch, seq, d)
```

```python
import functools

import numpy as np
import jax
import jax.numpy as jnp
from jax import lax
from jax.experimental import pallas as pl
from jax.experimental.pallas import tpu as pltpu

F32 = jnp.float32
BF16 = jnp.bfloat16

EPS = 1e-6
CHUNK = 64
HG_HEAD = 128
MLA_NOPE = 128
MLA_ROPE = 64
MLA_V = 128
MLA_QK = MLA_NOPE + MLA_ROPE
MLA_HEAD_PAD = 256
ROPE_THETA = 10000.0

LANES = 128
TOKEN_TILE = 512
FF_CHUNK = 256
HG_BLOCK = 512
ATT_TILE = 512
ATT_SPLIT = 2
ATT_UNROLL = 4
VMEM_LIMIT = 56 * 1024 * 1024


def _rms(x, gain):
    return x * lax.rsqrt(jnp.mean(x * x, axis=-1, keepdims=True) + EPS) * gain


def _silu(x):
    return x * jax.nn.sigmoid(x)


def _dot(a, b):
    return jnp.dot(a, b, preferred_element_type=F32)


def _dot_nt(a, b):
    return lax.dot_general(a, b, (((1,), (1,)), ((), ())), preferred_element_type=F32)


def _dot_tn(a, b):
    return lax.dot_general(a, b, (((0,), (0,)), ((), ())), preferred_element_type=F32)


def _resident(shape):
    zeros = (0,) * len(shape)
    return pl.BlockSpec(shape, lambda *_: zeros, pipeline_mode=pl.Buffered(1))


def _params(semantics):
    return pltpu.CompilerParams(dimension_semantics=semantics, vmem_limit_bytes=VMEM_LIMIT)


def _ffn_body(x_ref, gin_ref, win_ref, wout_ref, gout_ref, *rest, emit_h):
    if emit_h:
        h_ref, n_ref, a_ref = rest
    else:
        n_ref, a_ref = rest
    n_chunks, _, two_fc = win_ref.shape
    fc = two_fc // 2
    x = x_ref[...]
    xn = _rms(x, gin_ref[...]).astype(BF16)
    for c in range(n_chunks):
        hc = _dot(xn, win_ref[c])
        a_ref[:, c * fc:(c + 1) * fc] = (_silu(hc[:, :fc]) * hc[:, fc:]).astype(BF16)
    h = x + 0.5 * _dot(a_ref[...], wout_ref[...])
    if emit_h:
        h_ref[...] = h
    n_ref[...] = _rms(h, gout_ref[...]).astype(n_ref.dtype)


def _ffn(x, gin, w_in, w_out, gout, *, emit_h, norm_dtype):
    t, d = x.shape
    d_ff = w_out.shape[0]
    nc = d_ff // FF_CHUNK
    gate = w_in[:, :d_ff].reshape(d, nc, 1, FF_CHUNK)
    up = w_in[:, d_ff:].reshape(d, nc, 1, FF_CHUNK)
    win = jnp.concatenate([gate, up], axis=2).transpose(1, 0, 2, 3).reshape(nc, d, 2 * FF_CHUNK).astype(BF16)
    tm = TOKEN_TILE
    row = pl.BlockSpec((tm, d), lambda i: (i, 0))
    out_shape = [jax.ShapeDtypeStruct((t, d), norm_dtype)]
    out_specs = [row]
    if emit_h:
        out_shape = [jax.ShapeDtypeStruct((t, d), F32)] + out_shape
        out_specs = [row] + out_specs
    return pl.pallas_call(
        functools.partial(_ffn_body, emit_h=emit_h),
        grid=(t // tm,),
        in_specs=[row, _resident((1, d)), _resident(win.shape), _resident((d_ff, d)), _resident((1, d))],
        out_specs=out_specs,
        out_shape=out_shape,
        scratch_shapes=[pltpu.VMEM((tm, d_ff), BF16)],
        compiler_params=_params(("parallel",)),
        name="ffn1" if emit_h else "ffn2",
    )(x, gin.reshape(1, d), win, w_out.astype(BF16), gout.reshape(1, d))


def _proj_body(u_ref, w_ref, wm_ref, bm_ref, tab_ref, hq_ref, lf_ref, kk_ref, v_ref, g_ref, c_ref, gate_ref,
               *, layer):
    w = hq_ref.shape[1]
    d = u_ref.shape[1]
    u = u_ref[...]
    hq_ref[...] = _silu(_dot(u, w_ref[:, 0:w]))
    tab = tab_ref[...]
    e = jnp.exp(tab - jnp.max(tab, axis=0, keepdims=True))
    lb = jnp.sum(e[:layer + 1], axis=0, keepdims=True) / jnp.sum(e, axis=0, keepdims=True)
    z = _dot(u, w_ref[:, w:2 * w])
    ez = jnp.exp(-jnp.abs(z))
    r = 1.0 / (1.0 + ez)
    a = ez * r
    pos = z >= 0
    lf_ref[...] = jnp.log(lb + (1.0 - lb) * jnp.where(pos, r, a))
    kk_ref[...] = ((1.0 - lb) * jnp.where(pos, a, r)).astype(BF16)
    v_ref[...] = _dot(u, w_ref[:, 2 * w:3 * w]).astype(BF16)
    g_ref[...] = _silu(_dot(u, w_ref[:, 3 * w:4 * w])).astype(BF16)
    c_ref[...] = _dot(u, w_ref[:, 4 * w:])
    for j in range(2):
        gm = _dot(u, wm_ref[:, j * d:(j + 1) * d]) + bm_ref[:, j * d:(j + 1) * d]
        gate_ref[:, j * d:(j + 1) * d] = jax.nn.sigmoid(gm).astype(BF16)


def _proj(u, w_in_ext, w_merge, b_merge, lb_table, *, layer, hg_width):
    t, d = u.shape
    cols = w_in_ext.shape[1]
    c_cols = cols - 4 * hg_width
    tm = TOKEN_TILE

    def row(n):
        return pl.BlockSpec((tm, n), lambda i: (i, 0))

    return pl.pallas_call(
        functools.partial(_proj_body, layer=layer),
        grid=(t // tm,),
        in_specs=[row(d), _resident((d, cols)), _resident((d, 2 * d)), _resident((1, 2 * d)),
                  _resident(lb_table.shape)],
        out_specs=[row(hg_width), row(hg_width), row(hg_width), row(hg_width), row(hg_width), row(c_cols),
                   row(2 * d)],
        out_shape=[jax.ShapeDtypeStruct((t, hg_width), F32),
                   jax.ShapeDtypeStruct((t, hg_width), F32),
                   jax.ShapeDtypeStruct((t, hg_width), BF16),
                   jax.ShapeDtypeStruct((t, hg_width), BF16),
                   jax.ShapeDtypeStruct((t, hg_width), BF16),
                   jax.ShapeDtypeStruct((t, c_cols), F32),
                   jax.ShapeDtypeStruct((t, 2 * d), BF16)],
        compiler_params=_params(("parallel",)),
        name="proj",
    )(u, w_in_ext, w_merge.astype(BF16), b_merge.reshape(1, 2 * d), lb_table)


def _mla_prep_body(c_ref, pos_ref, gq_ref, wq_ref, gkv_ref, wkv_ref, qn_gain_ref, qr_gain_ref, kn_gain_ref,
                   kr_gain_ref, invf_ref, q_ref, k_ref, v_ref, *, heads, q_lora, kv_lora):
    tm = c_ref.shape[0]
    c = c_ref[...]
    qraw = _dot(_rms(c[:, :q_lora], gq_ref[...]).astype(BF16), wq_ref[...])
    kvraw = _dot(_rms(c[:, q_lora:q_lora + kv_lora], gkv_ref[...]).astype(BF16), wkv_ref[...])
    kpe = c[:, q_lora + kv_lora:]
    ang = pos_ref[...].astype(F32) * invf_ref[...]
    first = lax.broadcasted_iota(jnp.int32, (tm, LANES), 1) < MLA_ROPE
    cs = jnp.where(first, jnp.cos(ang), jnp.sin(ang))

    def rope(block, gain):
        t = block * (gain * cs)
        return t + pltpu.roll(t, MLA_ROPE, 1)

    def sumsq(x):
        return jnp.sum(x * x, axis=-1, keepdims=True)

    k_rope = rope(kpe, kr_gain_ref[...])
    k_pe_ss = sumsq(jnp.where(first, kpe, 0.0))
    scale = MLA_QK ** -0.5 * float(np.log2(np.e))
    nope_w = heads * MLA_NOPE
    for h in range(heads):
        lo, hi = h * MLA_NOPE, (h + 1) * MLA_NOPE
        qn = qraw[:, lo:hi]
        qr = qraw[:, nope_w + lo:nope_w + hi]
        sq = lax.rsqrt((sumsq(qn) + sumsq(jnp.where(first, qr, 0.0))) / MLA_QK + EPS) * scale
        o = h * MLA_HEAD_PAD
        q_ref[:, o:o + MLA_NOPE] = (qn * sq * qn_gain_ref[...]).astype(BF16)
        q_ref[:, o + MLA_NOPE:o + MLA_HEAD_PAD] = jnp.where(first, rope(qr, qr_gain_ref[...]) * sq, 0.0).astype(BF16)
        kn = kvraw[:, lo:hi]
        sk = lax.rsqrt((sumsq(kn) + k_pe_ss) / MLA_QK + EPS)
        k_ref[:, o:o + MLA_NOPE] = (kn * sk * kn_gain_ref[...]).astype(BF16)
        k_ref[:, o + MLA_NOPE:o + MLA_HEAD_PAD] = jnp.where(first, k_rope * sk, 0.0).astype(BF16)
    v_ref[...] = kvraw[:, nope_w:].astype(BF16)


def _rot_cols(w):
    half = w.shape[-1] // 2
    return jnp.concatenate([-w[..., half:], w[..., :half]], axis=-1)


def _rot_gain(g):
    half = g.shape[-1] // 2
    return jnp.concatenate([g[..., half:], g[..., :half]], axis=-1)


def _mla_prep(c, pos, gq, w_q_up, gkv, w_kv_up, q_head_gain, k_head_gain, *, heads):
    t, c_cols = c.shape
    q_lora, kv_lora = w_q_up.shape[0], w_kv_up.shape[0]
    wq = w_q_up.reshape(q_lora, heads, MLA_QK)
    wq_rope = wq[:, :, MLA_NOPE:]
    wq_ext = jnp.concatenate(
        [wq[:, :, :MLA_NOPE].reshape(q_lora, heads * MLA_NOPE),
         jnp.concatenate([wq_rope, _rot_cols(wq_rope)], axis=-1).reshape(q_lora, heads * 2 * MLA_ROPE)],
        axis=1).astype(BF16)
    wkv = w_kv_up.reshape(kv_lora, heads, MLA_NOPE + MLA_V)
    wkv_ext = jnp.concatenate([wkv[:, :, :MLA_NOPE].reshape(kv_lora, heads * MLA_NOPE),
                               wkv[:, :, MLA_NOPE:].reshape(kv_lora, heads * MLA_V)], axis=1).astype(BF16)

    def rope_gain(g):
        return jnp.concatenate([g[MLA_NOPE:], _rot_gain(g[MLA_NOPE:])]).reshape(1, 2 * MLA_ROPE)

    inv_freq = ROPE_THETA ** (-jnp.arange(0, MLA_ROPE, 2, dtype=F32) / MLA_ROPE)
    invf = jnp.tile(inv_freq, 4).reshape(1, LANES)
    tm = TOKEN_TILE

    def row(n):
        return pl.BlockSpec((tm, n), lambda i: (i, 0))

    qk_w = heads * MLA_HEAD_PAD
    return pl.pallas_call(
        functools.partial(_mla_prep_body, heads=heads, q_lora=q_lora, kv_lora=kv_lora),
        grid=(t // tm,),
        in_specs=[row(c_cols), row(1), _resident((1, q_lora)), _resident(wq_ext.shape), _resident((1, kv_lora)),
                  _resident(wkv_ext.shape), _resident((1, MLA_NOPE)), _resident((1, LANES)),
                  _resident((1, MLA_NOPE)), _resident((1, LANES)), _resident((1, LANES))],
        out_specs=[row(qk_w), row(qk_w), row(heads * MLA_V)],
        out_shape=[jax.ShapeDtypeStruct((t, qk_w), BF16), jax.ShapeDtypeStruct((t, qk_w), BF16),
                   jax.ShapeDtypeStruct((t, heads * MLA_V), BF16)],
        compiler_params=_params(("parallel",)),
        name="mla_prep",
    )(c, pos, gq.reshape(1, q_lora), wq_ext, gkv.reshape(1, kv_lora), wkv_ext,
      q_head_gain[:MLA_NOPE].reshape(1, MLA_NOPE), rope_gain(q_head_gain),
      k_head_gain[:MLA_NOPE].reshape(1, MLA_NOPE), rope_gain(k_head_gain), invf)


def _split3(x):
    hi = x.astype(BF16)
    r = x - hi.astype(F32)
    mid = r.astype(BF16)
    lo = (r - mid.astype(F32)).astype(BF16)
    return hi, mid, lo


def _hgrn_body(q_ref, lf_ref, kk_ref, v_ref, g_ref, gain_ref, tri_ref, y_ref, state_ref):
    @pl.when(pl.program_id(2) == 0)
    def _():
        state_ref[...] = jnp.zeros_like(state_ref)

    n = CHUNK
    rows = lax.broadcasted_iota(jnp.int32, (n, HG_HEAD), 0)
    trow = lax.broadcasted_iota(jnp.int32, (n, n), 0)
    tcol = lax.broadcasted_iota(jnp.int32, (n, n), 1)
    tri = tri_ref[...]
    gain = gain_ref[...]
    state = state_ref[...]
    halves = [n >> i for i in range(1, n.bit_length())]
    pairs = {half: (trow // (2 * half) == tcol // (2 * half)) & (trow % (2 * half) >= half)
             & (tcol % (2 * half) < half) for half in halves}
    eye = trow == tcol
    for c in range(q_ref.shape[1] // n):
        sl = slice(c * n, (c + 1) * n)
        q = q_ref[0, sl, :]
        lf = lf_ref[0, sl, :]
        k = kk_ref[0, sl, :].astype(F32)
        v = v_ref[0, sl, :]
        hi, mid, lo = _split3(lf)
        cum = _dot(tri, hi) + _dot(tri, mid) + _dot(tri, lo)
        scores = jnp.where(eye, jnp.sum(q * k, axis=-1, keepdims=True), 0.0)
        for half in halves:
            size = 2 * half
            if half >= 4:
                ref = jnp.concatenate(
                    [jnp.broadcast_to(cum[b * size + half - 1:b * size + half, :], (size, HG_HEAD))
                     for b in range(n // size)], axis=0)
                dec = -jnp.abs(cum - ref)
            elif half == 2:
                r4 = rows % 4
                prev = pltpu.roll(lf, 1, 0)
                nxt = pltpu.roll(lf, n - 1, 0)
                dec = jnp.where(r4 == 3, lf + prev, jnp.where(r4 == 2, lf, jnp.where(r4 == 0, nxt, 0.0)))
            else:
                dec = jnp.where(rows % 2 == 1, lf, 0.0)
            e = jnp.exp(dec)
            sc = _dot_nt((q * e).astype(BF16), (k * e).astype(BF16))
            scores = scores + jnp.where(pairs[half], sc, 0.0)
        last = cum[n - 1:n, :]
        o = _dot(scores.astype(BF16), v) + _dot_nt((q * jnp.exp(cum)).astype(BF16), state.astype(BF16))
        kdec = (k * jnp.exp(last - cum)).astype(BF16)
        state = state * jnp.exp(last) + _dot_tn(v, kdec)
        o = _rms(o, gain) * g_ref[0, sl, :].astype(F32)
        y_ref[0, sl, :] = o.astype(y_ref.dtype)
    state_ref[...] = state


def _hgrn(hq, lf, kk, v, g, out_gain, *, batch, seq):
    width = hq.shape[1]
    heads = width // HG_HEAD
    tb = HG_BLOCK
    tri = jnp.asarray(np.tril(np.ones((CHUNK, CHUNK), np.float32)), BF16)

    def r3(a):
        return a.reshape(batch, seq, width)

    blk = pl.BlockSpec((1, tb, HG_HEAD), lambda b, h, j: (b, j, h))
    y = pl.pallas_call(
        _hgrn_body,
        grid=(batch, heads, seq // tb),
        in_specs=[blk, blk, blk, blk, blk, _resident((1, HG_HEAD)), _resident((CHUNK, CHUNK))],
        out_specs=blk,
        out_shape=jax.ShapeDtypeStruct((batch, seq, width), BF16),
        scratch_shapes=[pltpu.VMEM((HG_HEAD, HG_HEAD), F32)],
        compiler_params=_params(("parallel", "parallel", "arbitrary")),
        name="hgrn",
    )(r3(hq), r3(lf), r3(kk), r3(v), r3(g), out_gain.reshape(1, HG_HEAD), tri)
    return y.reshape(batch * seq, width)


def _attn_body(q_ref, k_ref, v_ref, o_ref, vt_ref, s_ref, mx_ref, m_ref, l_ref, acc_ref):
    seq = q_ref.shape[1]
    tk = ATT_TILE
    tqs = ATT_TILE // ATT_SPLIT
    n_tiles = seq // tk

    def transpose_v(j, _):
        vt_ref[j] = v_ref[0, pl.ds(pl.multiple_of(j * tk, tk), tk), :].astype(F32).T.astype(BF16)
        return 0

    lax.fori_loop(0, n_tiles, transpose_v, 0)

    key_chunk = lax.broadcasted_iota(jnp.int32, (tk, tqs), 0) // CHUNK
    qry_chunk = lax.broadcasted_iota(jnp.int32, (tk, tqs), 1) // CHUNK
    allowed = [key_chunk <= qry_chunk + (sub * tqs) // CHUNK for sub in range(ATT_SPLIT)]

    def fold_rows(x, op):
        rows = x.shape[0]
        while rows > 8:
            rows //= 2
            x = op(x[:rows], x[rows:])
        return x

    def q_step(qi, _):
        base = pl.multiple_of(qi * tk, tk)

        def scores(kj, buf):
            k_tile = k_ref[0, pl.ds(pl.multiple_of(kj * tk, tk), tk), :]
            s = _dot_nt(k_tile, q_ref[0, pl.ds(base, tk), :])
            s_ref[buf] = s
            mx_ref[buf] = jnp.max(fold_rows(s, jnp.maximum), axis=0, keepdims=True)

        def absorb(kj, buf, masked):
            vt_tile = vt_ref[kj]
            for sub in range(ATT_SPLIT):
                cols = slice(sub * tqs, (sub + 1) * tqs)
                ss = s_ref[buf, :, cols]
                if masked:
                    ss = jnp.where(allowed[sub], ss, -jnp.inf)
                    tile_max = jnp.max(fold_rows(ss, jnp.maximum), axis=0, keepdims=True)
                else:
                    tile_max = mx_ref[buf, :, cols]
                m = m_ref[sub]
                m_new = jnp.maximum(m, tile_max)
                alpha = jnp.exp2(m - m_new)
                p = jnp.exp2(ss - m_new)
                m_ref[sub] = m_new
                l_ref[sub] = alpha * l_ref[sub] + jnp.sum(fold_rows(p, jnp.add), axis=0, keepdims=True)
                acc_ref[sub] = alpha * acc_ref[sub] + _dot(vt_tile, p.astype(BF16))

        def run(first, count, last_masked):
            for j in range(count):
                if j + 1 < count or not last_masked:
                    scores(first + j + 1, (j + 1) % 2)
                absorb(first + j, j % 2, last_masked and j + 1 == count)

        def group(g, _):
            run(g * ATT_UNROLL, ATT_UNROLL, False)
            return 0

        m_ref[...] = jnp.full(m_ref.shape, -jnp.inf, F32)
        l_ref[...] = jnp.zeros(l_ref.shape, F32)
        acc_ref[...] = jnp.zeros(acc_ref.shape, F32)
        scores(0, 0)
        lax.fori_loop(0, qi // ATT_UNROLL, group, 0)
        tail_first = (qi // ATT_UNROLL) * ATT_UNROLL
        for rem in range(ATT_UNROLL):
            @pl.when(qi - tail_first == rem)
            def _(rem=rem):
                run(tail_first, rem + 1, True)

        for sub in range(ATT_SPLIT):
            o_ref[0, pl.ds(pl.multiple_of(base + sub * tqs, tqs), tqs), :] = (
                (acc_ref[sub] / l_ref[sub]).T.astype(o_ref.dtype))
        return 0

    lax.fori_loop(0, n_tiles, q_step, 0)


def _attention(q, k, v, *, batch, seq, heads):
    qk = pl.BlockSpec((1, seq, MLA_HEAD_PAD), lambda b, h: (b, 0, h))
    vo = pl.BlockSpec((1, seq, MLA_V), lambda b, h: (b, 0, h))
    o = pl.pallas_call(
        _attn_body,
        grid=(batch, heads),
        in_specs=[qk, qk, vo],
        out_specs=vo,
        out_shape=jax.ShapeDtypeStruct((batch, seq, heads * MLA_V), BF16),
        scratch_shapes=[pltpu.VMEM((seq // ATT_TILE, MLA_V, ATT_TILE), BF16),
                        pltpu.VMEM((2, ATT_TILE, ATT_TILE), F32),
                        pltpu.VMEM((2, 1, ATT_TILE), F32),
                        pltpu.VMEM((ATT_SPLIT, 1, ATT_TILE // ATT_SPLIT), F32),
                        pltpu.VMEM((ATT_SPLIT, 1, ATT_TILE // ATT_SPLIT), F32),
                        pltpu.VMEM((ATT_SPLIT, MLA_V, ATT_TILE // ATT_SPLIT), F32)],
        compiler_params=_params(("parallel", "parallel")),
        name="attn",
    )(q.reshape(batch, seq, -1), k.reshape(batch, seq, -1), v.reshape(batch, seq, -1))
    return o.reshape(batch * seq, heads * MLA_V)


def _merge_body(yh_ref, ym_ref, gate_ref, h_ref, wh_ref, wm_ref, wo_ref, o_ref):
    d = h_ref.shape[1]
    mix = (gate_ref[:, :d].astype(F32) * _dot(yh_ref[...], wh_ref[...])
           + gate_ref[:, d:].astype(F32) * _dot(ym_ref[...], wm_ref[...]))
    o_ref[...] = h_ref[...] + _dot(mix.astype(BF16), wo_ref[...])


def _merge(yh, ym, gates, h, w_h, w_m, w_o):
    t, d = h.shape
    tm = TOKEN_TILE

    def row(n):
        return pl.BlockSpec((tm, n), lambda i: (i, 0))

    return pl.pallas_call(
        _merge_body,
        grid=(t // tm,),
        in_specs=[row(yh.shape[1]), row(ym.shape[1]), row(2 * d), row(d), _resident(w_h.shape),
                  _resident(w_m.shape), _resident(w_o.shape)],
        out_specs=row(d),
        out_shape=jax.ShapeDtypeStruct((t, d), F32),
        compiler_params=_params(("parallel",)),
        name="merge",
    )(yh, ym, gates, h, w_h.astype(BF16), w_m.astype(BF16), w_o.astype(BF16))


def kernel(x, positions, ffn1_norm, ffn1_w_in, ffn1_w_out, mix_norm, w_in, hg_lb_table, hg_out_norm, w_hg_branch, mla_q_lora_norm, w_q_up, mla_kv_lora_norm, w_kv_up, q_head_norm, k_head_norm, w_mla_branch, w_merge, b_merge, w_out, ffn2_norm, ffn2_w_in, ffn2_w_out, final_norm):
    batch, seq, d = x.shape
    depth = ffn1_norm.shape[0]
    hg_width = hg_lb_table.shape[1]
    heads = w_q_up.shape[2] // MLA_QK
    assert hg_out_norm.shape[1] == HG_HEAD and w_hg_branch.shape[1] == hg_width
    assert q_head_norm.shape[1] == MLA_QK and w_kv_up.shape[2] == heads * (MLA_NOPE + MLA_V)
    assert seq % max(HG_BLOCK, ATT_TILE) == 0 and (batch * seq) % TOKEN_TILE == 0
    t = batch * seq
    pos = positions.reshape(t, 1)
    h = x.reshape(t, d)
    for l in range(depth):
        kpe_w = w_in[l][:, -MLA_ROPE:]
        w_in_ext = jnp.concatenate([w_in[l], _rot_cols(kpe_w)], axis=1).astype(BF16)
        assert w_in_ext.shape[1] == 4 * hg_width + w_q_up.shape[1] + w_kv_up.shape[1] + 2 * MLA_ROPE
        h1, u = _ffn(h, ffn1_norm[l], ffn1_w_in[l], ffn1_w_out[l], mix_norm[l], emit_h=True, norm_dtype=BF16)
        hq, lf, kk, hv, hg, c, gates = _proj(u, w_in_ext, w_merge[l], b_merge[l], hg_lb_table, layer=l,
                                             hg_width=hg_width)
        q, k, v = _mla_prep(c, pos, mla_q_lora_norm[l], w_q_up[l], mla_kv_lora_norm[l], w_kv_up[l],
                            q_head_norm[l], k_head_norm[l], heads=heads)
        y_hg = _hgrn(hq, lf, kk, hv, hg, hg_out_norm[l], batch=batch, seq=seq)
        y_mla = _attention(q, k, v, batch=batch, seq=seq, heads=heads)
        h2 = _merge(y_hg, y_mla, gates, h1, w_hg_branch[l], w_mla_branch[l], w_out[l])
        (h,) = _ffn(h2, ffn2_norm[l], ffn2_w_in[l], ffn2_w_out[l], final_norm[l], emit_h=False, norm_dtype=F32)
    return h.reshape(batch, seq, d)
```

```python
import functools

import numpy as np
import jax
import jax.numpy as jnp
from jax import lax
from jax.experimental import pallas as pl
from jax.experimental.pallas import tpu as pltpu

F32 = jnp.float32
BF16 = jnp.bfloat16

EPS = 1e-6
CHUNK = 64
HG_HEAD = 128
MLA_NOPE = 128
MLA_ROPE = 64
MLA_V = 128
MLA_QK = MLA_NOPE + MLA_ROPE
MLA_HEAD_PAD = 256
ROPE_THETA = 10000.0

LANES = 128
TOKEN_TILE = 512
FF_CHUNK = 256
HG_BLOCK = 512
ATT_TILE = 512
ATT_SPLIT = 2
ATT_UNROLL = 4
VMEM_LIMIT = 56 * 1024 * 1024


def _rms(x, gain):
    return x * lax.rsqrt(jnp.mean(x * x, axis=-1, keepdims=True) + EPS) * gain


def _silu(x):
    return x * jax.nn.sigmoid(x)


def _dot(a, b):
    return jnp.dot(a, b, preferred_element_type=F32)


def _dot_nt(a, b):
    return lax.dot_general(a, b, (((1,), (1,)), ((), ())), preferred_element_type=F32)


def _dot_tn(a, b):
    return lax.dot_general(a, b, (((0,), (0,)), ((), ())), preferred_element_type=F32)


def _resident(shape):
    zeros = (0,) * len(shape)
    return pl.BlockSpec(shape, lambda *_: zeros, pipeline_mode=pl.Buffered(1))


def _params(semantics):
    return pltpu.CompilerParams(dimension_semantics=semantics, vmem_limit_bytes=VMEM_LIMIT)


def _ffn_body(x_ref, gin_ref, win_ref, wout_ref, gout_ref, *rest, emit_h):
    if emit_h:
        h_ref, n_ref, a_ref = rest
    else:
        n_ref, a_ref = rest
    n_chunks, _, two_fc = win_ref.shape
    fc = two_fc // 2
    x = x_ref[...]
    xn = _rms(x, gin_ref[...]).astype(BF16)
    for c in range(n_chunks):
        hc = _dot(xn, win_ref[c])
        a_ref[:, c * fc:(c + 1) * fc] = (_silu(hc[:, :fc]) * hc[:, fc:]).astype(BF16)
    h = x + 0.5 * _dot(a_ref[...], wout_ref[...])
    if emit_h:
        h_ref[...] = h
    n_ref[...] = _rms(h, gout_ref[...]).astype(n_ref.dtype)


def _ffn(x, gin, w_in, w_out, gout, *, emit_h, norm_dtype):
    t, d = x.shape
    d_ff = w_out.shape[0]
    nc = d_ff // FF_CHUNK
    gate = w_in[:, :d_ff].reshape(d, nc, 1, FF_CHUNK)
    up = w_in[:, d_ff:].reshape(d, nc, 1, FF_CHUNK)
    win = jnp.concatenate([gate, up], axis=2).transpose(1, 0, 2, 3).reshape(nc, d, 2 * FF_CHUNK).astype(BF16)
    tm = TOKEN_TILE
    row = pl.BlockSpec((tm, d), lambda i: (i, 0))
    out_shape = [jax.ShapeDtypeStruct((t, d), norm_dtype)]
    out_specs = [row]
    if emit_h:
        out_shape = [jax.ShapeDtypeStruct((t, d), F32)] + out_shape
        out_specs = [row] + out_specs
    return pl.pallas_call(
        functools.partial(_ffn_body, emit_h=emit_h),
        grid=(t // tm,),
        in_specs=[row, _resident((1, d)), _resident(win.shape), _resident((d_ff, d)), _resident((1, d))],
        out_specs=out_specs,
        out_shape=out_shape,
        scratch_shapes=[pltpu.VMEM((tm, d_ff), BF16)],
        compiler_params=_params(("parallel",)),
        name="ffn1" if emit_h else "ffn2",
    )(x, gin.reshape(1, d), win, w_out.astype(BF16), gout.reshape(1, d))


def _proj_body(u_ref, w_ref, wm_ref, bm_ref, tab_ref, hq_ref, lf_ref, kk_ref, v_ref, g_ref, c_ref, gate_ref,
               *, layer):
    w = hq_ref.shape[1]
    d = u_ref.shape[1]
    u = u_ref[...]
    hq_ref[...] = _silu(_dot(u, w_ref[:, 0:w]))
    tab = tab_ref[...]
    e = jnp.exp(tab - jnp.max(tab, axis=0, keepdims=True))
    lb = jnp.sum(e[:layer + 1], axis=0, keepdims=True) / jnp.sum(e, axis=0, keepdims=True)
    z = _dot(u, w_ref[:, w:2 * w])
    ez = jnp.exp(-jnp.abs(z))
    r = 1.0 / (1.0 + ez)
    a = ez * r
    pos = z >= 0
    lf_ref[...] = jnp.log(lb + (1.0 - lb) * jnp.where(pos, r, a)) * float(np.log2(np.e))
    kk_ref[...] = ((1.0 - lb) * jnp.where(pos, a, r)).astype(BF16)
    v_ref[...] = _dot(u, w_ref[:, 2 * w:3 * w]).astype(BF16)
    g_ref[...] = _silu(_dot(u, w_ref[:, 3 * w:4 * w])).astype(BF16)
    c_ref[...] = _dot(u, w_ref[:, 4 * w:])
    for j in range(2):
        gm = _dot(u, wm_ref[:, j * d:(j + 1) * d]) + bm_ref[:, j * d:(j + 1) * d]
        gate_ref[:, j * d:(j + 1) * d] = jax.nn.sigmoid(gm).astype(BF16)


def _proj(u, w_in_ext, w_merge, b_merge, lb_table, *, layer, hg_width):
    t, d = u.shape
    cols = w_in_ext.shape[1]
    c_cols = cols - 4 * hg_width
    tm = TOKEN_TILE

    def row(n):
        return pl.BlockSpec((tm, n), lambda i: (i, 0))

    return pl.pallas_call(
        functools.partial(_proj_body, layer=layer),
        grid=(t // tm,),
        in_specs=[row(d), _resident((d, cols)), _resident((d, 2 * d)), _resident((1, 2 * d)),
                  _resident(lb_table.shape)],
        out_specs=[row(hg_width), row(hg_width), row(hg_width), row(hg_width), row(hg_width), row(c_cols),
                   row(2 * d)],
        out_shape=[jax.ShapeDtypeStruct((t, hg_width), F32),
                   jax.ShapeDtypeStruct((t, hg_width), F32),
                   jax.ShapeDtypeStruct((t, hg_width), BF16),
                   jax.ShapeDtypeStruct((t, hg_width), BF16),
                   jax.ShapeDtypeStruct((t, hg_width), BF16),
                   jax.ShapeDtypeStruct((t, c_cols), F32),
                   jax.ShapeDtypeStruct((t, 2 * d), BF16)],
        compiler_params=_params(("parallel",)),
        name="proj",
    )(u, w_in_ext, w_merge.astype(BF16), b_merge.reshape(1, 2 * d), lb_table)


def _mla_prep_body(c_ref, pos_ref, gq_ref, wq_ref, gkv_ref, wkv_ref, qn_gain_ref, qr_gain_ref, kn_gain_ref,
                   kr_gain_ref, invf_ref, q_ref, k_ref, v_ref, *, heads, q_lora, kv_lora):
    tm = c_ref.shape[0]
    c = c_ref[...]
    qraw = _dot(_rms(c[:, :q_lora], gq_ref[...]).astype(BF16), wq_ref[...])
    kvraw = _dot(_rms(c[:, q_lora:q_lora + kv_lora], gkv_ref[...]).astype(BF16), wkv_ref[...])
    kpe = c[:, q_lora + kv_lora:]
    ang = pos_ref[...].astype(F32) * invf_ref[...]
    first = lax.broadcasted_iota(jnp.int32, (tm, LANES), 1) < MLA_ROPE
    cs = jnp.where(first, jnp.cos(ang), jnp.sin(ang))

    def rope(block, gain):
        t = block * (gain * cs)
        return t + pltpu.roll(t, MLA_ROPE, 1)

    def sumsq(x):
        return jnp.sum(x * x, axis=-1, keepdims=True)

    k_rope = rope(kpe, kr_gain_ref[...])
    k_pe_ss = sumsq(jnp.where(first, kpe, 0.0))
    scale = MLA_QK ** -0.5 * float(np.log2(np.e))
    nope_w = heads * MLA_NOPE
    for h in range(heads):
        lo, hi = h * MLA_NOPE, (h + 1) * MLA_NOPE
        qn = qraw[:, lo:hi]
        qr = qraw[:, nope_w + lo:nope_w + hi]
        sq = lax.rsqrt((sumsq(qn) + sumsq(jnp.where(first, qr, 0.0))) / MLA_QK + EPS) * scale
        o = h * MLA_HEAD_PAD
        q_ref[:, o:o + MLA_NOPE] = (qn * sq * qn_gain_ref[...]).astype(BF16)
        q_ref[:, o + MLA_NOPE:o + MLA_HEAD_PAD] = jnp.where(first, rope(qr, qr_gain_ref[...]) * sq, 0.0).astype(BF16)
        kn = kvraw[:, lo:hi]
        sk = lax.rsqrt((sumsq(kn) + k_pe_ss) / MLA_QK + EPS)
        k_ref[:, o:o + MLA_NOPE] = (kn * sk * kn_gain_ref[...]).astype(BF16)
        k_ref[:, o + MLA_NOPE:o + MLA_HEAD_PAD] = jnp.where(first, k_rope * sk, 0.0).astype(BF16)
    v_ref[...] = kvraw[:, nope_w:].astype(BF16)


def _rot_cols(w):
    half = w.shape[-1] // 2
    return jnp.concatenate([-w[..., half:], w[..., :half]], axis=-1)


def _rot_gain(g):
    half = g.shape[-1] // 2
    return jnp.concatenate([g[..., half:], g[..., :half]], axis=-1)


def _mla_prep(c, pos, gq, w_q_up, gkv, w_kv_up, q_head_gain, k_head_gain, *, heads):
    t, c_cols = c.shape
    q_lora, kv_lora = w_q_up.shape[0], w_kv_up.shape[0]
    wq = w_q_up.reshape(q_lora, heads, MLA_QK)
    wq_rope = wq[:, :, MLA_NOPE:]
    wq_ext = jnp.concatenate(
        [wq[:, :, :MLA_NOPE].reshape(q_lora, heads * MLA_NOPE),
         jnp.concatenate([wq_rope, _rot_cols(wq_rope)], axis=-1).reshape(q_lora, heads * 2 * MLA_ROPE)],
        axis=1).astype(BF16)
    wkv = w_kv_up.reshape(kv_lora, heads, MLA_NOPE + MLA_V)
    wkv_ext = jnp.concatenate([wkv[:, :, :MLA_NOPE].reshape(kv_lora, heads * MLA_NOPE),
                               wkv[:, :, MLA_NOPE:].reshape(kv_lora, heads * MLA_V)], axis=1).astype(BF16)

    def rope_gain(g):
        return jnp.concatenate([g[MLA_NOPE:], _rot_gain(g[MLA_NOPE:])]).reshape(1, 2 * MLA_ROPE)

    inv_freq = ROPE_THETA ** (-jnp.arange(0, MLA_ROPE, 2, dtype=F32) / MLA_ROPE)
    invf = jnp.tile(inv_freq, 4).reshape(1, LANES)
    tm = TOKEN_TILE

    def row(n):
        return pl.BlockSpec((tm, n), lambda i: (i, 0))

    qk_w = heads * MLA_HEAD_PAD
    return pl.pallas_call(
        functools.partial(_mla_prep_body, heads=heads, q_lora=q_lora, kv_lora=kv_lora),
        grid=(t // tm,),
        in_specs=[row(c_cols), row(1), _resident((1, q_lora)), _resident(wq_ext.shape), _resident((1, kv_lora)),
                  _resident(wkv_ext.shape), _resident((1, MLA_NOPE)), _resident((1, LANES)),
                  _resident((1, MLA_NOPE)), _resident((1, LANES)), _resident((1, LANES))],
        out_specs=[row(qk_w), row(qk_w), row(heads * MLA_V)],
        out_shape=[jax.ShapeDtypeStruct((t, qk_w), BF16), jax.ShapeDtypeStruct((t, qk_w), BF16),
                   jax.ShapeDtypeStruct((t, heads * MLA_V), BF16)],
        compiler_params=_params(("parallel",)),
        name="mla_prep",
    )(c, pos, gq.reshape(1, q_lora), wq_ext, gkv.reshape(1, kv_lora), wkv_ext,
      q_head_gain[:MLA_NOPE].reshape(1, MLA_NOPE), rope_gain(q_head_gain),
      k_head_gain[:MLA_NOPE].reshape(1, MLA_NOPE), rope_gain(k_head_gain), invf)


def _split3(x):
    hi = x.astype(BF16)
    r = x - hi.astype(F32)
    mid = r.astype(BF16)
    lo = (r - mid.astype(F32)).astype(BF16)
    return hi, mid, lo


def _hgrn_body(q_ref, lf_ref, kk_ref, v_ref, g_ref, gain_ref, tri_ref, y_ref, state_ref):
    @pl.when(pl.program_id(2) == 0)
    def _():
        state_ref[...] = jnp.zeros_like(state_ref)

    n = CHUNK
    tb = q_ref.shape[1]
    chunks = [slice(c * n, (c + 1) * n) for c in range(tb // n)]
    rows = lax.broadcasted_iota(jnp.int32, (tb, HG_HEAD), 0)
    trow = lax.broadcasted_iota(jnp.int32, (n, n), 0)
    tcol = lax.broadcasted_iota(jnp.int32, (n, n), 1)
    tri = tri_ref[...]
    halves = [n >> i for i in range(1, n.bit_length())]
    pairs = {half: (trow // (2 * half) == tcol // (2 * half)) & (trow % (2 * half) >= half)
             & (tcol % (2 * half) < half) for half in halves}

    def ref_rows(x, size, row):
        blocks = x.reshape(tb // size, size, HG_HEAD)[:, row:row + 1, :]
        return jnp.broadcast_to(blocks, (tb // size, size, HG_HEAD)).reshape(tb, HG_HEAD)

    q = q_ref[0]
    lf = lf_ref[0]
    k = kk_ref[0].astype(F32)
    v = v_ref[0]
    split = jnp.concatenate(_split3(lf), axis=1)
    cums = []
    for sl in chunks:
        r = _dot(tri, split[sl])
        cums.append(r[:, :HG_HEAD] + r[:, HG_HEAD:2 * HG_HEAD] + r[:, 2 * HG_HEAD:])
    cum = jnp.concatenate(cums, axis=0)
    diag = jnp.sum(q * k, axis=-1, keepdims=True)
    scores = [jnp.where(trow == tcol, diag[sl], 0.0) for sl in chunks]
    for half in halves:
        size = 2 * half
        if half >= 4:
            dec = -jnp.abs(cum - ref_rows(cum, size, half - 1))
        elif half == 2:
            r4 = rows % 4
            dec = jnp.where(r4 == 3, lf + pltpu.roll(lf, 1, 0),
                            jnp.where(r4 == 2, lf, jnp.where(r4 == 0, pltpu.roll(lf, tb - 1, 0), 0.0)))
        else:
            dec = jnp.where(rows % 2 == 1, lf, 0.0)
        e = jnp.exp2(dec)
        qe = (q * e).astype(BF16)
        ke = (k * e).astype(BF16)
        for c, sl in enumerate(chunks):
            scores[c] = scores[c] + jnp.where(pairs[half], _dot_nt(qe[sl], ke[sl]), 0.0)
    last = ref_rows(cum, n, n - 1)
    qdec = (q * jnp.exp2(cum)).astype(BF16)
    kdec = (k * jnp.exp2(last - cum)).astype(BF16)
    updates = [_dot_tn(v[sl], kdec[sl]) for sl in chunks]
    state = state_ref[...]
    outs = []
    for c, sl in enumerate(chunks):
        outs.append(_dot(scores[c].astype(BF16), v[sl]) + _dot_nt(qdec[sl], state.astype(BF16)))
        state = state * jnp.exp2(cum[sl][n - 1:n, :]) + updates[c]
    state_ref[...] = state
    o = jnp.concatenate(outs, axis=0)
    y_ref[0] = (_rms(o, gain_ref[...]) * g_ref[0].astype(F32)).astype(y_ref.dtype)


def _hgrn(hq, lf, kk, v, g, out_gain, *, batch, seq):
    width = hq.shape[1]
    heads = width // HG_HEAD
    tb = HG_BLOCK
    tri = jnp.asarray(np.tril(np.ones((CHUNK, CHUNK), np.float32)), BF16)

    def r3(a):
        return a.reshape(batch, seq, width)

    blk = pl.BlockSpec((1, tb, HG_HEAD), lambda b, h, j: (b, j, h))
    y = pl.pallas_call(
        _hgrn_body,
        grid=(batch, heads, seq // tb),
        in_specs=[blk, blk, blk, blk, blk, _resident((1, HG_HEAD)), _resident((CHUNK, CHUNK))],
        out_specs=blk,
        out_shape=jax.ShapeDtypeStruct((batch, seq, width), BF16),
        scratch_shapes=[pltpu.VMEM((HG_HEAD, HG_HEAD), F32)],
        compiler_params=_params(("parallel", "parallel", "arbitrary")),
        name="hgrn",
    )(r3(hq), r3(lf), r3(kk), r3(v), r3(g), out_gain.reshape(1, HG_HEAD), tri)
    return y.reshape(batch * seq, width)


def _attn_body(q_ref, k_ref, v_ref, o_ref, vt_ref, s_ref, mx_ref, m_ref, l_ref, acc_ref):
    seq = q_ref.shape[1]
    tk = ATT_TILE
    tqs = ATT_TILE // ATT_SPLIT
    n_tiles = seq // tk

    def transpose_v(j, _):
        vt_ref[j] = v_ref[0, pl.ds(pl.multiple_of(j * tk, tk), tk), :].astype(F32).T.astype(BF16)
        return 0

    lax.fori_loop(0, n_tiles, transpose_v, 0)

    key_chunk = lax.broadcasted_iota(jnp.int32, (tk, tqs), 0) // CHUNK
    qry_chunk = lax.broadcasted_iota(jnp.int32, (tk, tqs), 1) // CHUNK
    allowed = [key_chunk <= qry_chunk + (sub * tqs) // CHUNK for sub in range(ATT_SPLIT)]

    def fold_rows(x, op):
        rows = x.shape[0]
        while rows > 8:
            rows //= 2
            x = op(x[:rows], x[rows:])
        return x

    def q_step(qi, _):
        base = pl.multiple_of(qi * tk, tk)

        def scores(kj, buf):
            k_tile = k_ref[0, pl.ds(pl.multiple_of(kj * tk, tk), tk), :]
            s = _dot_nt(k_tile, q_ref[0, pl.ds(base, tk), :])
            s_ref[buf] = s
            mx_ref[buf] = jnp.max(fold_rows(s, jnp.maximum), axis=0, keepdims=True)

        def absorb(kj, buf, masked):
            vt_tile = vt_ref[kj]
            for sub in range(ATT_SPLIT):
                cols = slice(sub * tqs, (sub + 1) * tqs)
                ss = s_ref[buf, :, cols]
                if masked:
                    ss = jnp.where(allowed[sub], ss, -jnp.inf)
                    tile_max = jnp.max(fold_rows(ss, jnp.maximum), axis=0, keepdims=True)
                else:
                    tile_max = mx_ref[buf, :, cols]
                m = m_ref[sub]
                m_new = jnp.maximum(m, tile_max)
                alpha = jnp.exp2(m - m_new)
                p = jnp.exp2(ss - m_new)
                m_ref[sub] = m_new
                l_ref[sub] = alpha * l_ref[sub] + jnp.sum(fold_rows(p, jnp.add), axis=0, keepdims=True)
                acc_ref[sub] = alpha * acc_ref[sub] + _dot(vt_tile, p.astype(BF16))

        def run(first, count, last_masked):
            for j in range(count):
                if j + 1 < count or not last_masked:
                    scores(first + j + 1, (j + 1) % 2)
                absorb(first + j, j % 2, last_masked and j + 1 == count)

        def group(g, _):
            run(g * ATT_UNROLL, ATT_UNROLL, False)
            return 0

        m_ref[...] = jnp.full(m_ref.shape, -jnp.inf, F32)
        l_ref[...] = jnp.zeros(l_ref.shape, F32)
        acc_ref[...] = jnp.zeros(acc_ref.shape, F32)
        scores(0, 0)
        lax.fori_loop(0, qi // ATT_UNROLL, group, 0)
        tail_first = (qi // ATT_UNROLL) * ATT_UNROLL
        for rem in range(ATT_UNROLL):
            @pl.when(qi - tail_first == rem)
            def _(rem=rem):
                run(tail_first, rem + 1, True)

        for sub in range(ATT_SPLIT):
            o_ref[0, pl.ds(pl.multiple_of(base + sub * tqs, tqs), tqs), :] = (
                (acc_ref[sub] / l_ref[sub]).T.astype(o_ref.dtype))
        return 0

    lax.fori_loop(0, n_tiles, q_step, 0)


def _attention(q, k, v, *, batch, seq, heads):
    qk = pl.BlockSpec((1, seq, MLA_HEAD_PAD), lambda b, h: (b, 0, h))
    vo = pl.BlockSpec((1, seq, MLA_V), lambda b, h: (b, 0, h))
    o = pl.pallas_call(
        _attn_body,
        grid=(batch, heads),
        in_specs=[qk, qk, vo],
        out_specs=vo,
        out_shape=jax.ShapeDtypeStruct((batch, seq, heads * MLA_V), BF16),
        scratch_shapes=[pltpu.VMEM((seq // ATT_TILE, MLA_V, ATT_TILE), BF16),
                        pltpu.VMEM((2, ATT_TILE, ATT_TILE), F32),
                        pltpu.VMEM((2, 1, ATT_TILE), F32),
                        pltpu.VMEM((ATT_SPLIT, 1, ATT_TILE // ATT_SPLIT), F32),
                        pltpu.VMEM((ATT_SPLIT, 1, ATT_TILE // ATT_SPLIT), F32),
                        pltpu.VMEM((ATT_SPLIT, MLA_V, ATT_TILE // ATT_SPLIT), F32)],
        compiler_params=_params(("parallel", "parallel")),
        name="attn",
    )(q.reshape(batch, seq, -1), k.reshape(batch, seq, -1), v.reshape(batch, seq, -1))
    return o.reshape(batch * seq, heads * MLA_V)


def _merge_body(yh_ref, ym_ref, gate_ref, h_ref, wh_ref, wm_ref, wo_ref, o_ref):
    d = h_ref.shape[1]
    mix = (gate_ref[:, :d].astype(F32) * _dot(yh_ref[...], wh_ref[...])
           + gate_ref[:, d:].astype(F32) * _dot(ym_ref[...], wm_ref[...]))
    o_ref[...] = h_ref[...] + _dot(mix.astype(BF16), wo_ref[...])


def _merge(yh, ym, gates, h, w_h, w_m, w_o):
    t, d = h.shape
    tm = TOKEN_TILE

    def row(n):
        return pl.BlockSpec((tm, n), lambda i: (i, 0))

    return pl.pallas_call(
        _merge_body,
        grid=(t // tm,),
        in_specs=[row(yh.shape[1]), row(ym.shape[1]), row(2 * d), row(d), _resident(w_h.shape),
                  _resident(w_m.shape), _resident(w_o.shape)],
        out_specs=row(d),
        out_shape=jax.ShapeDtypeStruct((t, d), F32),
        compiler_params=_params(("parallel",)),
        name="merge",
    )(yh, ym, gates, h, w_h.astype(BF16), w_m.astype(BF16), w_o.astype(BF16))


def kernel(x, positions, ffn1_norm, ffn1_w_in, ffn1_w_out, mix_norm, w_in, hg_lb_table, hg_out_norm, w_hg_branch, mla_q_lora_norm, w_q_up, mla_kv_lora_norm, w_kv_up, q_head_norm, k_head_norm, w_mla_branch, w_merge, b_merge, w_out, ffn2_norm, ffn2_w_in, ffn2_w_out, final_norm):
    batch, seq, d = x.shape
    depth = ffn1_norm.shape[0]
    hg_width = hg_lb_table.shape[1]
    heads = w_q_up.shape[2] // MLA_QK
    assert hg_out_norm.shape[1] == HG_HEAD and w_hg_branch.shape[1] == hg_width
    assert q_head_norm.shape[1] == MLA_QK and w_kv_up.shape[2] == heads * (MLA_NOPE + MLA_V)
    assert seq % max(HG_BLOCK, ATT_TILE) == 0 and (batch * seq) % TOKEN_TILE == 0
    t = batch * seq
    pos = positions.reshape(t, 1)
    h = x.reshape(t, d)
    for l in range(depth):
        kpe_w = w_in[l][:, -MLA_ROPE:]
        w_in_ext = jnp.concatenate([w_in[l], _rot_cols(kpe_w)], axis=1).astype(BF16)
        assert w_in_ext.shape[1] == 4 * hg_width + w_q_up.shape[1] + w_kv_up.shape[1] + 2 * MLA_ROPE
        h1, u = _ffn(h, ffn1_norm[l], ffn1_w_in[l], ffn1_w_out[l], mix_norm[l], emit_h=True, norm_dtype=BF16)
        hq, lf, kk, hv, hg, c, gates = _proj(u, w_in_ext, w_merge[l], b_merge[l], hg_lb_table, layer=l,
                                             hg_width=hg_width)
        q, k, v = _mla_prep(c, pos, mla_q_lora_norm[l], w_q_up[l], mla_kv_lora_norm[l], w_kv_up[l],
                            q_head_norm[l], k_head_norm[l], heads=heads)
        y_hg = _hgrn(hq, lf, kk, hv, hg, hg_out_norm[l], batch=batch, seq=seq)
        y_mla = _attention(q, k, v, batch=batch, seq=seq, heads=heads)
        h2 = _merge(y_hg, y_mla, gates, h1, w_hg_branch[l], w_mla_branch[l], w_out[l])
        (h,) = _ffn(h2, ffn2_norm[l], ffn2_w_in[l], ffn2_w_out[l], final_norm[l], emit_h=False, norm_dtype=F32)
    return h.reshape(batch, seq, d)
```

```python
import functools

import numpy as np
import jax
import jax.numpy as jnp
from jax import lax
from jax.experimental import pallas as pl
from jax.experimental.pallas import tpu as pltpu

F32 = jnp.float32
BF16 = jnp.bfloat16

EPS = 1e-6
CHUNK = 64
HG_HEAD = 128
MLA_NOPE = 128
MLA_ROPE = 64
MLA_V = 128
MLA_QK = MLA_NOPE + MLA_ROPE
MLA_HEAD_PAD = 256
ROPE_THETA = 10000.0
LOG2E = float(np.log2(np.e))

LANES = 128
TOKEN_TILE = 512
PROJ_TILE = 256
FF_CHUNK = 256
HG_BLOCK = 512
ATT_TILE = 512
ATT_SPLIT = 2
ATT_UNROLL = 4
ATT_FOLD = 32
VMEM_LIMIT = 56 * 1024 * 1024


def _rms(x, gain):
    return x * lax.rsqrt(jnp.mean(x * x, axis=-1, keepdims=True) + EPS) * gain


def _silu(x):
    return x * jax.nn.sigmoid(x)


def _dot(a, b):
    return jnp.dot(a, b, preferred_element_type=F32)


def _dot_nt(a, b):
    return lax.dot_general(a, b, (((1,), (1,)), ((), ())), preferred_element_type=F32)


def _dot_tn(a, b):
    return lax.dot_general(a, b, (((0,), (0,)), ((), ())), preferred_element_type=F32)


def _resident(shape):
    zeros = (0,) * len(shape)
    return pl.BlockSpec(shape, lambda *_: zeros, pipeline_mode=pl.Buffered(1))


def _params(semantics):
    return pltpu.CompilerParams(dimension_semantics=semantics, vmem_limit_bytes=VMEM_LIMIT)


def _ffn_body(x_ref, gin_ref, win_ref, wout_ref, gout_ref, *rest, emit_h):
    if emit_h:
        h_ref, n_ref, a_ref = rest
    else:
        n_ref, a_ref = rest
    n_chunks, _, two_fc = win_ref.shape
    fc = two_fc // 2
    x = x_ref[...]
    xn = _rms(x, gin_ref[...]).astype(BF16)
    for c in range(n_chunks):
        hc = _dot(xn, win_ref[c])
        a_ref[:, c * fc:(c + 1) * fc] = (_silu(hc[:, :fc]) * hc[:, fc:]).astype(BF16)
    h = x + 0.5 * _dot(a_ref[...], wout_ref[...])
    if emit_h:
        h_ref[...] = h
    n_ref[...] = _rms(h, gout_ref[...]).astype(n_ref.dtype)


def _ffn(x, gin, w_in, w_out, gout, *, emit_h, norm_dtype):
    t, d = x.shape
    d_ff = w_out.shape[0]
    nc = d_ff // FF_CHUNK
    gate = w_in[:, :d_ff].reshape(d, nc, 1, FF_CHUNK)
    up = w_in[:, d_ff:].reshape(d, nc, 1, FF_CHUNK)
    win = jnp.concatenate([gate, up], axis=2).transpose(1, 0, 2, 3).reshape(nc, d, 2 * FF_CHUNK).astype(BF16)
    tm = TOKEN_TILE
    row = pl.BlockSpec((tm, d), lambda i: (i, 0))
    out_shape = [jax.ShapeDtypeStruct((t, d), norm_dtype)]
    out_specs = [row]
    if emit_h:
        out_shape = [jax.ShapeDtypeStruct((t, d), F32)] + out_shape
        out_specs = [row] + out_specs
    return pl.pallas_call(
        functools.partial(_ffn_body, emit_h=emit_h),
        grid=(t // tm,),
        in_specs=[row, _resident((1, d)), _resident(win.shape), _resident((d_ff, d)), _resident((1, d))],
        out_specs=out_specs,
        out_shape=out_shape,
        scratch_shapes=[pltpu.VMEM((tm, d_ff), BF16)],
        compiler_params=_params(("parallel",)),
        name="ffn1" if emit_h else "ffn2",
    )(x, gin.reshape(1, d), win, w_out.astype(BF16), gout.reshape(1, d))


def _proj_body(u_ref, pos_ref, w_ref, wm_ref, bm_ref, tab_ref, gq_ref, wq_ref, gkv_ref, wkv_ref, qr_gain_ref,
               kn_gain_ref, kr_gain_ref, invf_ref,
               hq_ref, lf_ref, kk_ref, hv_ref, hg_ref, gate_ref, q_ref, k_ref, v_ref, *, layer, heads):
    w = hq_ref.shape[1]
    tm, d = u_ref.shape
    q_lora, kv_lora = wq_ref.shape[0], wkv_ref.shape[0]
    u = u_ref[...]

    c = _dot(u, w_ref[:, 4 * w:])
    qraw = _dot(_rms(c[:, :q_lora], gq_ref[...]).astype(BF16), wq_ref[...])
    kvraw = _dot(_rms(c[:, q_lora:q_lora + kv_lora], gkv_ref[...]).astype(BF16), wkv_ref[...])
    kpe = c[:, q_lora + kv_lora:]

    hq_ref[...] = _silu(_dot(u, w_ref[:, 0:w])).astype(hq_ref.dtype)
    tab = tab_ref[...]
    e = jnp.exp(tab - jnp.max(tab, axis=0, keepdims=True))
    lb = jnp.sum(e[:layer + 1], axis=0, keepdims=True) / jnp.sum(e, axis=0, keepdims=True)
    z = _dot(u, w_ref[:, w:2 * w])
    ez = jnp.exp(-jnp.abs(z))
    r = 1.0 / (1.0 + ez)
    a = ez * r
    pos = z >= 0
    lf_ref[...] = jnp.log(lb + (1.0 - lb) * jnp.where(pos, r, a)) * LOG2E
    kk_ref[...] = ((1.0 - lb) * jnp.where(pos, a, r)).astype(BF16)
    hv_ref[...] = _dot(u, w_ref[:, 2 * w:3 * w]).astype(BF16)
    hg_ref[...] = _silu(_dot(u, w_ref[:, 3 * w:4 * w])).astype(BF16)
    for j in range(2):
        gm = _dot(u, wm_ref[:, j * d:(j + 1) * d]) + bm_ref[:, j * d:(j + 1) * d]
        gate_ref[:, j * d:(j + 1) * d] = jax.nn.sigmoid(gm).astype(BF16)

    ang = pos_ref[...].astype(F32) * invf_ref[...]
    first = lax.broadcasted_iota(jnp.int32, (tm, LANES), 1) < MLA_ROPE
    cs = jnp.where(first, jnp.cos(ang), jnp.sin(ang))

    def rope(block, gain_cs):
        t = block * gain_cs
        return t + pltpu.roll(t, MLA_ROPE, 1)

    def sumsq(x):
        return jnp.sum(x * x, axis=-1, keepdims=True)

    k_rope = rope(kpe, kr_gain_ref[...] * cs)
    k_pe_ss = 0.5 * sumsq(kpe)
    q_gain_cs = qr_gain_ref[...] * cs
    scale = MLA_QK ** -0.5 * LOG2E
    nope_w = heads * MLA_NOPE
    for h in range(heads):
        lo, hi = h * MLA_NOPE, (h + 1) * MLA_NOPE
        qn = qraw[:, lo:hi]
        qr = qraw[:, nope_w + lo:nope_w + hi]
        sq = lax.rsqrt((sumsq(qn) + 0.5 * sumsq(qr)) / MLA_QK + EPS) * scale
        o = h * MLA_HEAD_PAD
        q_ref[:, o:o + MLA_NOPE] = (qn * sq).astype(BF16)
        q_ref[:, o + MLA_NOPE:o + MLA_HEAD_PAD] = (rope(qr, q_gain_cs) * sq).astype(BF16)
        kn = kvraw[:, lo:hi]
        sk = lax.rsqrt((sumsq(kn) + k_pe_ss) / MLA_QK + EPS)
        k_ref[:, o:o + MLA_NOPE] = (kn * sk * kn_gain_ref[...]).astype(BF16)
        k_ref[:, o + MLA_NOPE:o + MLA_HEAD_PAD] = jnp.where(first, k_rope * sk, 0.0).astype(BF16)
    v_ref[...] = kvraw[:, nope_w:].astype(BF16)


def _rot_cols(w):
    half = w.shape[-1] // 2
    return jnp.concatenate([-w[..., half:], w[..., :half]], axis=-1)


def _rot_gain(g):
    half = g.shape[-1] // 2
    return jnp.concatenate([g[..., half:], g[..., :half]], axis=-1)


def _proj(u, pos, w_in, w_merge, b_merge, lb_table, gq, w_q_up, gkv, w_kv_up, q_head_gain, k_head_gain,
          *, layer, hg_width, heads):
    t, d = u.shape
    q_lora, kv_lora = w_q_up.shape[0], w_kv_up.shape[0]
    w_in_ext = jnp.concatenate([w_in, _rot_cols(w_in[:, -MLA_ROPE:])], axis=1).astype(BF16)
    cols = w_in_ext.shape[1]
    assert cols == 4 * hg_width + q_lora + kv_lora + 2 * MLA_ROPE
    wq = w_q_up.reshape(q_lora, heads, MLA_QK)
    wq_rope = wq[:, :, MLA_NOPE:]
    wq_ext = jnp.concatenate(
        [wq[:, :, :MLA_NOPE].reshape(q_lora, heads * MLA_NOPE),
         jnp.concatenate([wq_rope, _rot_cols(wq_rope)], axis=-1).reshape(q_lora, heads * 2 * MLA_ROPE)],
        axis=1).astype(BF16)
    wkv = w_kv_up.reshape(kv_lora, heads, MLA_NOPE + MLA_V)
    wkv_ext = jnp.concatenate([wkv[:, :, :MLA_NOPE].reshape(kv_lora, heads * MLA_NOPE),
                               wkv[:, :, MLA_NOPE:].reshape(kv_lora, heads * MLA_V)], axis=1).astype(BF16)

    def rope_gain(g):
        return jnp.concatenate([g[MLA_NOPE:], _rot_gain(g[MLA_NOPE:])]).reshape(1, 2 * MLA_ROPE)

    inv_freq = ROPE_THETA ** (-jnp.arange(0, MLA_ROPE, 2, dtype=F32) / MLA_ROPE)
    invf = jnp.tile(inv_freq, 4).reshape(1, LANES)
    nope_gain = (q_head_gain[:MLA_NOPE] * k_head_gain[:MLA_NOPE]).reshape(1, MLA_NOPE)
    tm = PROJ_TILE

    def row(n):
        return pl.BlockSpec((tm, n), lambda i: (i, 0))

    qk_w = heads * MLA_HEAD_PAD
    return pl.pallas_call(
        functools.partial(_proj_body, layer=layer, heads=heads),
        grid=(t // tm,),
        in_specs=[row(d), row(1), _resident((d, cols)), _resident((d, 2 * d)), _resident((1, 2 * d)),
                  _resident(lb_table.shape), _resident((1, q_lora)), _resident(wq_ext.shape),
                  _resident((1, kv_lora)), _resident(wkv_ext.shape), _resident((1, LANES)),
                  _resident((1, MLA_NOPE)), _resident((1, LANES)), _resident((1, LANES))],
        out_specs=[row(hg_width), row(hg_width), row(hg_width), row(hg_width), row(hg_width), row(2 * d),
                   row(qk_w), row(qk_w), row(heads * MLA_V)],
        out_shape=[jax.ShapeDtypeStruct((t, hg_width), BF16),
                   jax.ShapeDtypeStruct((t, hg_width), F32),
                   jax.ShapeDtypeStruct((t, hg_width), BF16),
                   jax.ShapeDtypeStruct((t, hg_width), BF16),
                   jax.ShapeDtypeStruct((t, hg_width), BF16),
                   jax.ShapeDtypeStruct((t, 2 * d), BF16),
                   jax.ShapeDtypeStruct((t, qk_w), BF16),
                   jax.ShapeDtypeStruct((t, qk_w), BF16),
                   jax.ShapeDtypeStruct((t, heads * MLA_V), BF16)],
        compiler_params=_params(("parallel",)),
        name="proj",
    )(u, pos, w_in_ext, w_merge.astype(BF16), b_merge.reshape(1, 2 * d), lb_table, gq.reshape(1, q_lora), wq_ext,
      gkv.reshape(1, kv_lora), wkv_ext, rope_gain(q_head_gain), nope_gain, rope_gain(k_head_gain), invf)


def _split3(x):
    hi = x.astype(BF16)
    r = x - hi.astype(F32)
    mid = r.astype(BF16)
    lo = (r - mid.astype(F32)).astype(BF16)
    return hi, mid, lo


def _hgrn_body(q_ref, lf_ref, kk_ref, v_ref, g_ref, gain_ref, tri_ref, y_ref, state_ref):
    @pl.when(pl.program_id(2) == 0)
    def _():
        state_ref[...] = jnp.zeros_like(state_ref)

    n = CHUNK
    tb = q_ref.shape[1]
    chunks = [slice(c * n, (c + 1) * n) for c in range(tb // n)]
    rows = lax.broadcasted_iota(jnp.int32, (tb, HG_HEAD), 0)
    trow = lax.broadcasted_iota(jnp.int32, (n, n), 0)
    tcol = lax.broadcasted_iota(jnp.int32, (n, n), 1)
    tri = tri_ref[...]
    halves = [n >> i for i in range(1, n.bit_length())]
    pairs = {half: (trow // (2 * half) == tcol // (2 * half)) & (trow % (2 * half) >= half)
             & (tcol % (2 * half) < half) for half in halves}

    def ref_rows(x, size, row):
        blocks = x.reshape(tb // size, size, HG_HEAD)[:, row:row + 1, :]
        return jnp.broadcast_to(blocks, (tb // size, size, HG_HEAD)).reshape(tb, HG_HEAD)

    q = q_ref[0].astype(F32)
    lf = lf_ref[0]
    k = kk_ref[0].astype(F32)
    v = v_ref[0]
    split = jnp.concatenate(_split3(lf), axis=1)
    cums = []
    for sl in chunks:
        r = _dot(tri, split[sl])
        cums.append(r[:, :HG_HEAD] + r[:, HG_HEAD:2 * HG_HEAD] + r[:, 2 * HG_HEAD:])
    cum = jnp.concatenate(cums, axis=0)
    diag = jnp.sum(q * k, axis=-1, keepdims=True)
    scores = [jnp.where(trow == tcol, diag[sl], 0.0) for sl in chunks]
    for half in halves:
        size = 2 * half
        if half >= 4:
            dec = -jnp.abs(cum - ref_rows(cum, size, half - 1))
        elif half == 2:
            r4 = rows % 4
            dec = jnp.where(r4 == 3, lf + pltpu.roll(lf, 1, 0),
                            jnp.where(r4 == 2, lf, jnp.where(r4 == 0, pltpu.roll(lf, tb - 1, 0), 0.0)))
        else:
            dec = jnp.where(rows % 2 == 1, lf, 0.0)
        e = jnp.exp2(dec)
        qe = (q * e).astype(BF16)
        ke = (k * e).astype(BF16)
        for c, sl in enumerate(chunks):
            scores[c] = scores[c] + jnp.where(pairs[half], _dot_nt(qe[sl], ke[sl]), 0.0)
    last = ref_rows(cum, n, n - 1)
    qdec = (q * jnp.exp2(cum)).astype(BF16)
    kdec = (k * jnp.exp2(last - cum)).astype(BF16)
    updates = [_dot_tn(v[sl], kdec[sl]) for sl in chunks]
    state = state_ref[...]
    outs = []
    for c, sl in enumerate(chunks):
        outs.append(_dot(scores[c].astype(BF16), v[sl]) + _dot_nt(qdec[sl], state.astype(BF16)))
        state = state * jnp.exp2(cum[sl][n - 1:n, :]) + updates[c]
    state_ref[...] = state
    o = jnp.concatenate(outs, axis=0)
    y_ref[0] = (_rms(o, gain_ref[...]) * g_ref[0].astype(F32)).astype(y_ref.dtype)


def _hgrn(hq, lf, kk, v, g, out_gain, *, batch, seq):
    width = hq.shape[1]
    heads = width // HG_HEAD
    tb = HG_BLOCK
    tri = jnp.asarray(np.tril(np.ones((CHUNK, CHUNK), np.float32)), BF16)

    def r3(a):
        return a.reshape(batch, seq, width)

    blk = pl.BlockSpec((1, tb, HG_HEAD), lambda b, h, j: (b, j, h))
    y = pl.pallas_call(
        _hgrn_body,
        grid=(batch, heads, seq // tb),
        in_specs=[blk, blk, blk, blk, blk, _resident((1, HG_HEAD)), _resident((CHUNK, CHUNK))],
        out_specs=blk,
        out_shape=jax.ShapeDtypeStruct((batch, seq, width), BF16),
        scratch_shapes=[pltpu.VMEM((HG_HEAD, HG_HEAD), F32)],
        compiler_params=_params(("parallel", "parallel", "arbitrary")),
        name="hgrn",
    )(r3(hq), r3(lf), r3(kk), r3(v), r3(g), out_gain.reshape(1, HG_HEAD), tri)
    return y.reshape(batch * seq, width)


def _attn_body(q_ref, k_ref, v_ref, o_ref, vt_ref, s_ref, mx_ref, m_ref, l_ref, acc_ref):
    seq = q_ref.shape[1]
    tk = ATT_TILE
    tqs = ATT_TILE // ATT_SPLIT
    n_tiles = seq // tk

    def transpose_v(j, _):
        vt_ref[j] = v_ref[0, pl.ds(pl.multiple_of(j * tk, tk), tk), :].astype(F32).T.astype(BF16)
        return 0

    lax.fori_loop(0, n_tiles, transpose_v, 0)

    key_chunk = lax.broadcasted_iota(jnp.int32, (tk, tqs), 0) // CHUNK
    qry_chunk = lax.broadcasted_iota(jnp.int32, (tk, tqs), 1) // CHUNK
    allowed = [key_chunk <= qry_chunk + (sub * tqs) // CHUNK for sub in range(ATT_SPLIT)]

    def fold_rows(x, op):
        rows, cols = x.shape
        x = x.reshape(rows // ATT_FOLD, ATT_FOLD, cols)
        out = x[0]
        for i in range(1, rows // ATT_FOLD):
            out = op(out, x[i])
        return out

    def q_step(qi, _):
        base = pl.multiple_of(qi * tk, tk)

        def scores(kj, buf):
            k_tile = k_ref[0, pl.ds(pl.multiple_of(kj * tk, tk), tk), :]
            s = _dot_nt(k_tile, q_ref[0, pl.ds(base, tk), :])
            s_ref[buf] = s
            mx_ref[buf] = jnp.max(fold_rows(s, jnp.maximum), axis=0, keepdims=True)

        def absorb(kj, buf, masked):
            vt_tile = vt_ref[kj]
            for sub in range(ATT_SPLIT):
                cols = slice(sub * tqs, (sub + 1) * tqs)
                ss = s_ref[buf, :, cols]
                if masked:
                    ss = jnp.where(allowed[sub], ss, -jnp.inf)
                    tile_max = jnp.max(fold_rows(ss, jnp.maximum), axis=0, keepdims=True)
                else:
                    tile_max = mx_ref[buf, :, cols]
                m = m_ref[sub]
                m_new = jnp.maximum(m, tile_max)
                alpha = jnp.exp2(m - m_new)
                p = jnp.exp2(ss - m_new)
                m_ref[sub] = m_new
                l_ref[sub] = alpha * l_ref[sub] + jnp.sum(fold_rows(p, jnp.add), axis=0, keepdims=True)
                acc_ref[sub] = alpha * acc_ref[sub] + _dot(vt_tile, p.astype(BF16))

        def run(first, count, last_masked):
            for j in range(count):
                if j + 1 < count or not last_masked:
                    scores(first + j + 1, (j + 1) % 2)
                absorb(first + j, j % 2, last_masked and j + 1 == count)

        def group(g, _):
            run(g * ATT_UNROLL, ATT_UNROLL, False)
            return 0

        m_ref[...] = jnp.full(m_ref.shape, -jnp.inf, F32)
        l_ref[...] = jnp.zeros(l_ref.shape, F32)
        acc_ref[...] = jnp.zeros(acc_ref.shape, F32)
        scores(0, 0)
        lax.fori_loop(0, qi // ATT_UNROLL, group, 0)
        tail_first = (qi // ATT_UNROLL) * ATT_UNROLL
        for rem in range(ATT_UNROLL):
            @pl.when(qi - tail_first == rem)
            def _(rem=rem):
                run(tail_first, rem + 1, True)

        for sub in range(ATT_SPLIT):
            o_ref[0, pl.ds(pl.multiple_of(base + sub * tqs, tqs), tqs), :] = (
                (acc_ref[sub] / l_ref[sub]).T.astype(o_ref.dtype))
        return 0

    lax.fori_loop(0, n_tiles, q_step, 0)


def _attention(q, k, v, *, batch, seq, heads):
    qk = pl.BlockSpec((1, seq, MLA_HEAD_PAD), lambda b, h: (b, 0, h))
    vo = pl.BlockSpec((1, seq, MLA_V), lambda b, h: (b, 0, h))
    o = pl.pallas_call(
        _attn_body,
        grid=(batch, heads),
        in_specs=[qk, qk, vo],
        out_specs=vo,
        out_shape=jax.ShapeDtypeStruct((batch, seq, heads * MLA_V), BF16),
        scratch_shapes=[pltpu.VMEM((seq // ATT_TILE, MLA_V, ATT_TILE), BF16),
                        pltpu.VMEM((2, ATT_TILE, ATT_TILE), F32),
                        pltpu.VMEM((2, 1, ATT_TILE), F32),
                        pltpu.VMEM((ATT_SPLIT, 1, ATT_TILE // ATT_SPLIT), F32),
                        pltpu.VMEM((ATT_SPLIT, 1, ATT_TILE // ATT_SPLIT), F32),
                        pltpu.VMEM((ATT_SPLIT, MLA_V, ATT_TILE // ATT_SPLIT), F32)],
        compiler_params=_params(("parallel", "parallel")),
        name="attn",
    )(q.reshape(batch, seq, -1), k.reshape(batch, seq, -1), v.reshape(batch, seq, -1))
    return o.reshape(batch * seq, heads * MLA_V)


def _merge_body(yh_ref, ym_ref, gate_ref, h_ref, wh_ref, wm_ref, wo_ref, o_ref):
    d = h_ref.shape[1]
    mix = (gate_ref[:, :d].astype(F32) * _dot(yh_ref[...], wh_ref[...])
           + gate_ref[:, d:].astype(F32) * _dot(ym_ref[...], wm_ref[...]))
    o_ref[...] = h_ref[...] + _dot(mix.astype(BF16), wo_ref[...])


def _merge(yh, ym, gates, h, w_h, w_m, w_o):
    t, d = h.shape
    tm = TOKEN_TILE

    def row(n):
        return pl.BlockSpec((tm, n), lambda i: (i, 0))

    return pl.pallas_call(
        _merge_body,
        grid=(t // tm,),
        in_specs=[row(yh.shape[1]), row(ym.shape[1]), row(2 * d), row(d), _resident(w_h.shape),
                  _resident(w_m.shape), _resident(w_o.shape)],
        out_specs=row(d),
        out_shape=jax.ShapeDtypeStruct((t, d), F32),
        compiler_params=_params(("parallel",)),
        name="merge",
    )(yh, ym, gates, h, w_h.astype(BF16), w_m.astype(BF16), w_o.astype(BF16))


def kernel(x, positions, ffn1_norm, ffn1_w_in, ffn1_w_out, mix_norm, w_in, hg_lb_table, hg_out_norm, w_hg_branch, mla_q_lora_norm, w_q_up, mla_kv_lora_norm, w_kv_up, q_head_norm, k_head_norm, w_mla_branch, w_merge, b_merge, w_out, ffn2_norm, ffn2_w_in, ffn2_w_out, final_norm):
    batch, seq, d = x.shape
    depth = ffn1_norm.shape[0]
    hg_width = hg_lb_table.shape[1]
    heads = w_q_up.shape[2] // MLA_QK
    assert hg_out_norm.shape[1] == HG_HEAD and w_hg_branch.shape[1] == hg_width
    assert q_head_norm.shape[1] == MLA_QK and w_kv_up.shape[2] == heads * (MLA_NOPE + MLA_V)
    assert seq % max(HG_BLOCK, ATT_TILE) == 0 and (batch * seq) % TOKEN_TILE == 0
    t = batch * seq
    pos = positions.reshape(t, 1)
    h = x.reshape(t, d)
    for l in range(depth):
        h1, u = _ffn(h, ffn1_norm[l], ffn1_w_in[l], ffn1_w_out[l], mix_norm[l], emit_h=True, norm_dtype=BF16)
        hq, lf, kk, hv, hg, gates, q, k, v = _proj(
            u, pos, w_in[l], w_merge[l], b_merge[l], hg_lb_table, mla_q_lora_norm[l], w_q_up[l],
            mla_kv_lora_norm[l], w_kv_up[l], q_head_norm[l], k_head_norm[l], layer=l, hg_width=hg_width,
            heads=heads)
        y_hg = _hgrn(hq, lf, kk, hv, hg, hg_out_norm[l], batch=batch, seq=seq)
        y_mla = _attention(q, k, v, batch=batch, seq=seq, heads=heads)
        h2 = _merge(y_hg, y_mla, gates, h1, w_hg_branch[l], w_mla_branch[l], w_out[l])
        (h,) = _ffn(h2, ffn2_norm[l], ffn2_w_in[l], ffn2_w_out[l], final_norm[l], emit_h=False, norm_dtype=F32)
    return h.reshape(batch, seq, d)
```

```python
import functools

import numpy as np
import jax
import jax.numpy as jnp
from jax import lax
from jax.experimental import pallas as pl
from jax.experimental.pallas import tpu as pltpu

F32 = jnp.float32
BF16 = jnp.bfloat16

EPS = 1e-6
CHUNK = 64
HG_HEAD = 128
MLA_NOPE = 128
MLA_ROPE = 64
MLA_V = 128
MLA_QK = MLA_NOPE + MLA_ROPE
MLA_HEAD_PAD = 256
ROPE_THETA = 10000.0
LOG2E = float(np.log2(np.e))

LANES = 128
TOKEN_TILE = 512
PROJ_TILE = 256
FF_CHUNK = 256
HG_BLOCK = 512
ATT_TILE = 512
ATT_SPLIT = 2
ATT_UNROLL = 4
ATT_FOLD = 32
VMEM_LIMIT = 56 * 1024 * 1024


def _rms(x, gain):
    return x * lax.rsqrt(jnp.mean(x * x, axis=-1, keepdims=True) + EPS) * gain


def _silu(x):
    return x * jax.nn.sigmoid(x)


def _dot(a, b):
    return jnp.dot(a, b, preferred_element_type=F32)


def _dot_nt(a, b):
    return lax.dot_general(a, b, (((1,), (1,)), ((), ())), preferred_element_type=F32)


def _dot_tn(a, b):
    return lax.dot_general(a, b, (((0,), (0,)), ((), ())), preferred_element_type=F32)


def _resident(shape):
    zeros = (0,) * len(shape)
    return pl.BlockSpec(shape, lambda *_: zeros, pipeline_mode=pl.Buffered(1))


def _params(semantics):
    return pltpu.CompilerParams(dimension_semantics=semantics, vmem_limit_bytes=VMEM_LIMIT)


def _ffn_body(x_ref, gin_ref, win_ref, wout_ref, gout_ref, *rest, emit_h):
    if emit_h:
        h_ref, n_ref, a_ref = rest
    else:
        n_ref, a_ref = rest
    n_chunks, _, two_fc = win_ref.shape
    fc = two_fc // 2
    x = x_ref[...]
    xn = _rms(x, gin_ref[...]).astype(BF16)
    for c in range(n_chunks):
        hc = _dot(xn, win_ref[c])
        a_ref[:, c * fc:(c + 1) * fc] = (_silu(hc[:, :fc]) * hc[:, fc:]).astype(BF16)
    h = x + 0.5 * _dot(a_ref[...], wout_ref[...])
    if emit_h:
        h_ref[...] = h
    n_ref[...] = _rms(h, gout_ref[...]).astype(n_ref.dtype)


def _ffn(x, gin, w_in, w_out, gout, *, emit_h, norm_dtype):
    t, d = x.shape
    d_ff = w_out.shape[0]
    nc = d_ff // FF_CHUNK
    gate = w_in[:, :d_ff].reshape(d, nc, 1, FF_CHUNK)
    up = w_in[:, d_ff:].reshape(d, nc, 1, FF_CHUNK)
    win = jnp.concatenate([gate, up], axis=2).transpose(1, 0, 2, 3).reshape(nc, d, 2 * FF_CHUNK).astype(BF16)
    tm = TOKEN_TILE
    row = pl.BlockSpec((tm, d), lambda i: (i, 0))
    out_shape = [jax.ShapeDtypeStruct((t, d), norm_dtype)]
    out_specs = [row]
    if emit_h:
        out_shape = [jax.ShapeDtypeStruct((t, d), F32)] + out_shape
        out_specs = [row] + out_specs
    return pl.pallas_call(
        functools.partial(_ffn_body, emit_h=emit_h),
        grid=(t // tm,),
        in_specs=[row, _resident((1, d)), _resident(win.shape), _resident((d_ff, d)), _resident((1, d))],
        out_specs=out_specs,
        out_shape=out_shape,
        scratch_shapes=[pltpu.VMEM((tm, d_ff), BF16)],
        compiler_params=_params(("parallel",)),
        name="ffn1" if emit_h else "ffn2",
    )(x, gin.reshape(1, d), win, w_out.astype(BF16), gout.reshape(1, d))


def _proj_body(u_ref, pos_ref, w_ref, wm_ref, bm_ref, tab_ref, gq_ref, wq_ref, gkv_ref, wkv_ref, qr_gain_ref,
               kn_gain_ref, kr_gain_ref, invf_ref,
               hq_ref, lf_ref, kk_ref, hv_ref, hg_ref, gate_ref, q_ref, k_ref, v_ref, *, layer, heads):
    w = hq_ref.shape[1]
    tm, d = u_ref.shape
    q_lora, kv_lora = wq_ref.shape[0], wkv_ref.shape[0]
    u = u_ref[...]

    c = _dot(u, w_ref[:, 4 * w:])
    qraw = _dot(_rms(c[:, :q_lora], gq_ref[...]).astype(BF16), wq_ref[...])
    kvraw = _dot(_rms(c[:, q_lora:q_lora + kv_lora], gkv_ref[...]).astype(BF16), wkv_ref[...])
    kpe = c[:, q_lora + kv_lora:]

    hq_ref[...] = _silu(_dot(u, w_ref[:, 0:w])).astype(hq_ref.dtype)
    tab = tab_ref[...]
    e = jnp.exp(tab - jnp.max(tab, axis=0, keepdims=True))
    lb = jnp.sum(e[:layer + 1], axis=0, keepdims=True) / jnp.sum(e, axis=0, keepdims=True)
    z = _dot(u, w_ref[:, w:2 * w])
    ez = jnp.exp(-jnp.abs(z))
    r = 1.0 / (1.0 + ez)
    a = ez * r
    pos = z >= 0
    lf_ref[...] = jnp.log(lb + (1.0 - lb) * jnp.where(pos, r, a)) * LOG2E
    kk_ref[...] = ((1.0 - lb) * jnp.where(pos, a, r)).astype(BF16)
    hv_ref[...] = _dot(u, w_ref[:, 2 * w:3 * w]).astype(BF16)
    hg_ref[...] = _silu(_dot(u, w_ref[:, 3 * w:4 * w])).astype(BF16)
    for j in range(2):
        gm = _dot(u, wm_ref[:, j * d:(j + 1) * d]) + bm_ref[:, j * d:(j + 1) * d]
        gate_ref[:, j * d:(j + 1) * d] = jax.nn.sigmoid(gm).astype(BF16)

    ang = pos_ref[...].astype(F32) * invf_ref[...]
    first = lax.broadcasted_iota(jnp.int32, (tm, LANES), 1) < MLA_ROPE
    cs = jnp.where(first, jnp.cos(ang), jnp.sin(ang))

    def rope(block, gain_cs):
        t = block * gain_cs
        return t + pltpu.roll(t, MLA_ROPE, 1)

    def sumsq(x):
        return jnp.sum(x * x, axis=-1, keepdims=True)

    k_rope = rope(kpe, kr_gain_ref[...] * cs)
    k_pe_ss = 0.5 * sumsq(kpe)
    q_gain_cs = qr_gain_ref[...] * cs
    scale = MLA_QK ** -0.5 * LOG2E
    nope_w = heads * MLA_NOPE
    for h in range(heads):
        lo, hi = h * MLA_NOPE, (h + 1) * MLA_NOPE
        qn = qraw[:, lo:hi]
        qr = qraw[:, nope_w + lo:nope_w + hi]
        sq = lax.rsqrt((sumsq(qn) + 0.5 * sumsq(qr)) / MLA_QK + EPS) * scale
        o = h * MLA_HEAD_PAD
        q_ref[:, o:o + MLA_NOPE] = (qn * sq).astype(BF16)
        q_ref[:, o + MLA_NOPE:o + MLA_HEAD_PAD] = (rope(qr, q_gain_cs) * sq).astype(BF16)
        kn = kvraw[:, lo:hi]
        sk = lax.rsqrt((sumsq(kn) + k_pe_ss) / MLA_QK + EPS)
        k_ref[:, o:o + MLA_NOPE] = (kn * sk * kn_gain_ref[...]).astype(BF16)
        k_ref[:, o + MLA_NOPE:o + MLA_HEAD_PAD] = jnp.where(first, k_rope * sk, 0.0).astype(BF16)
    v_ref[...] = kvraw[:, nope_w:].astype(BF16)


def _rot_cols(w):
    half = w.shape[-1] // 2
    return jnp.concatenate([-w[..., half:], w[..., :half]], axis=-1)


def _rot_gain(g):
    half = g.shape[-1] // 2
    return jnp.concatenate([g[..., half:], g[..., :half]], axis=-1)


def _proj(u, pos, w_in, w_merge, b_merge, lb_table, gq, w_q_up, gkv, w_kv_up, q_head_gain, k_head_gain,
          *, layer, hg_width, heads):
    t, d = u.shape
    q_lora, kv_lora = w_q_up.shape[0], w_kv_up.shape[0]
    w_in_ext = jnp.concatenate([w_in, _rot_cols(w_in[:, -MLA_ROPE:])], axis=1).astype(BF16)
    cols = w_in_ext.shape[1]
    assert cols == 4 * hg_width + q_lora + kv_lora + 2 * MLA_ROPE
    wq = w_q_up.reshape(q_lora, heads, MLA_QK)
    wq_rope = wq[:, :, MLA_NOPE:]
    wq_ext = jnp.concatenate(
        [wq[:, :, :MLA_NOPE].reshape(q_lora, heads * MLA_NOPE),
         jnp.concatenate([wq_rope, _rot_cols(wq_rope)], axis=-1).reshape(q_lora, heads * 2 * MLA_ROPE)],
        axis=1).astype(BF16)
    wkv = w_kv_up.reshape(kv_lora, heads, MLA_NOPE + MLA_V)
    wkv_ext = jnp.concatenate([wkv[:, :, :MLA_NOPE].reshape(kv_lora, heads * MLA_NOPE),
                               wkv[:, :, MLA_NOPE:].reshape(kv_lora, heads * MLA_V)], axis=1).astype(BF16)

    def rope_gain(g):
        return jnp.concatenate([g[MLA_NOPE:], _rot_gain(g[MLA_NOPE:])]).reshape(1, 2 * MLA_ROPE)

    inv_freq = ROPE_THETA ** (-jnp.arange(0, MLA_ROPE, 2, dtype=F32) / MLA_ROPE)
    invf = jnp.tile(inv_freq, 4).reshape(1, LANES)
    nope_gain = (q_head_gain[:MLA_NOPE] * k_head_gain[:MLA_NOPE]).reshape(1, MLA_NOPE)
    tm = PROJ_TILE

    def row(n):
        return pl.BlockSpec((tm, n), lambda i: (i, 0))

    qk_w = heads * MLA_HEAD_PAD
    return pl.pallas_call(
        functools.partial(_proj_body, layer=layer, heads=heads),
        grid=(t // tm,),
        in_specs=[row(d), row(1), _resident((d, cols)), _resident((d, 2 * d)), _resident((1, 2 * d)),
                  _resident(lb_table.shape), _resident((1, q_lora)), _resident(wq_ext.shape),
                  _resident((1, kv_lora)), _resident(wkv_ext.shape), _resident((1, LANES)),
                  _resident((1, MLA_NOPE)), _resident((1, LANES)), _resident((1, LANES))],
        out_specs=[row(hg_width), row(hg_width), row(hg_width), row(hg_width), row(hg_width), row(2 * d),
                   row(qk_w), row(qk_w), row(heads * MLA_V)],
        out_shape=[jax.ShapeDtypeStruct((t, hg_width), BF16),
                   jax.ShapeDtypeStruct((t, hg_width), F32),
                   jax.ShapeDtypeStruct((t, hg_width), BF16),
                   jax.ShapeDtypeStruct((t, hg_width), BF16),
                   jax.ShapeDtypeStruct((t, hg_width), BF16),
                   jax.ShapeDtypeStruct((t, 2 * d), BF16),
                   jax.ShapeDtypeStruct((t, qk_w), BF16),
                   jax.ShapeDtypeStruct((t, qk_w), BF16),
                   jax.ShapeDtypeStruct((t, heads * MLA_V), BF16)],
        compiler_params=_params(("parallel",)),
        name="proj",
    )(u, pos, w_in_ext, w_merge.astype(BF16), b_merge.reshape(1, 2 * d), lb_table, gq.reshape(1, q_lora), wq_ext,
      gkv.reshape(1, kv_lora), wkv_ext, rope_gain(q_head_gain), nope_gain, rope_gain(k_head_gain), invf)


def _split3(x):
    hi = x.astype(BF16)
    r = x - hi.astype(F32)
    mid = r.astype(BF16)
    lo = (r - mid.astype(F32)).astype(BF16)
    return hi, mid, lo


def _hgrn_body(q_ref, lf_ref, kk_ref, v_ref, g_ref, gain_ref, tri_ref, y_ref, state_ref):
    @pl.when(pl.program_id(2) == 0)
    def _():
        state_ref[...] = jnp.zeros_like(state_ref)

    n = CHUNK
    tb = q_ref.shape[1]
    chunks = [slice(c * n, (c + 1) * n) for c in range(tb // n)]
    trow = lax.broadcasted_iota(jnp.int32, (n, n), 0)
    tcol = lax.broadcasted_iota(jnp.int32, (n, n), 1)
    tri = tri_ref[...]
    halves = [n >> i for i in range(1, n.bit_length())]
    pairs = {half: (trow // (2 * half) == tcol // (2 * half)) & (trow % (2 * half) >= half)
             & (tcol % (2 * half) < half) for half in halves}

    def ref_rows(x, size, row):
        blocks = x.reshape(tb // size, size, HG_HEAD)[:, row:row + 1, :]
        return jnp.broadcast_to(blocks, (tb // size, size, HG_HEAD)).reshape(tb, HG_HEAD)

    q = q_ref[0].astype(F32)
    lf = lf_ref[0]
    k = kk_ref[0].astype(F32)
    v = v_ref[0]
    lf3 = lf.reshape(tb // 8, 8, HG_HEAD)
    r8 = lax.broadcasted_iota(jnp.int32, (1, 8, HG_HEAD), 1)
    split = jnp.concatenate(_split3(lf), axis=1)
    cums = []
    for sl in chunks:
        r = _dot(tri, split[sl])
        cums.append(r[:, :HG_HEAD] + r[:, HG_HEAD:2 * HG_HEAD] + r[:, 2 * HG_HEAD:])
    cum = jnp.concatenate(cums, axis=0)
    diag = jnp.sum(q * k, axis=-1, keepdims=True)
    scores = [jnp.where(trow == tcol, diag[sl], 0.0) for sl in chunks]
    for half in halves:
        size = 2 * half
        if half >= 4:
            diff = pltpu.bitcast(cum - ref_rows(cum, size, half - 1), jnp.uint32)
            dec = pltpu.bitcast(diff | jnp.uint32(0x80000000), F32)
        elif half == 2:
            dec = jnp.where(r8 % 4 == 3, lf3 + pltpu.roll(lf3, 1, 1),
                            jnp.where(r8 % 4 == 2, lf3,
                                      jnp.where(r8 % 4 == 0, pltpu.roll(lf3, 7, 1), 0.0))).reshape(tb, HG_HEAD)
        else:
            dec = jnp.where(r8 % 2 == 1, lf3, 0.0).reshape(tb, HG_HEAD)
        e = jnp.exp2(dec)
        qe = (q * e).astype(BF16)
        ke = (k * e).astype(BF16)
        for c, sl in enumerate(chunks):
            scores[c] = scores[c] + jnp.where(pairs[half], _dot_nt(qe[sl], ke[sl]), 0.0)
    last = ref_rows(cum, n, n - 1)
    qdec = (q * jnp.exp2(cum)).astype(BF16)
    kdec = (k * jnp.exp2(last - cum)).astype(BF16)
    updates = [_dot_tn(v[sl], kdec[sl]) for sl in chunks]
    state = state_ref[...]
    outs = []
    for c, sl in enumerate(chunks):
        outs.append(_dot(scores[c].astype(BF16), v[sl]) + _dot_nt(qdec[sl], state.astype(BF16)))
        state = state * jnp.exp2(cum[sl][n - 1:n, :]) + updates[c]
    state_ref[...] = state
    o = jnp.concatenate(outs, axis=0)
    y_ref[0] = (_rms(o, gain_ref[...]) * g_ref[0].astype(F32)).astype(y_ref.dtype)


def _hgrn(hq, lf, kk, v, g, out_gain, *, batch, seq):
    width = hq.shape[1]
    heads = width // HG_HEAD
    tb = HG_BLOCK
    tri = jnp.asarray(np.tril(np.ones((CHUNK, CHUNK), np.float32)), BF16)

    def r3(a):
        return a.reshape(batch, seq, width)

    blk = pl.BlockSpec((1, tb, HG_HEAD), lambda b, h, j: (b, j, h))
    y = pl.pallas_call(
        _hgrn_body,
        grid=(batch, heads, seq // tb),
        in_specs=[blk, blk, blk, blk, blk, _resident((1, HG_HEAD)), _resident((CHUNK, CHUNK))],
        out_specs=blk,
        out_shape=jax.ShapeDtypeStruct((batch, seq, width), BF16),
        scratch_shapes=[pltpu.VMEM((HG_HEAD, HG_HEAD), F32)],
        compiler_params=_params(("parallel", "parallel", "arbitrary")),
        name="hgrn",
    )(r3(hq), r3(lf), r3(kk), r3(v), r3(g), out_gain.reshape(1, HG_HEAD), tri)
    return y.reshape(batch * seq, width)


def _attn_body(q_ref, k_ref, v_ref, o_ref, vt_ref, s_ref, mx_ref, m_ref, l_ref, acc_ref):
    seq = q_ref.shape[1]
    tk = ATT_TILE
    tqs = ATT_TILE // ATT_SPLIT
    n_tiles = seq // tk

    def transpose_v(j, _):
        vt_ref[j] = v_ref[0, pl.ds(pl.multiple_of(j * tk, tk), tk), :].astype(F32).T.astype(BF16)
        return 0

    lax.fori_loop(0, n_tiles, transpose_v, 0)

    key_chunk = lax.broadcasted_iota(jnp.int32, (tk, tqs), 0) // CHUNK
    qry_chunk = lax.broadcasted_iota(jnp.int32, (tk, tqs), 1) // CHUNK
    allowed = [key_chunk <= qry_chunk + (sub * tqs) // CHUNK for sub in range(ATT_SPLIT)]

    def fold_rows(x, op):
        rows, cols = x.shape
        x = x.reshape(rows // ATT_FOLD, ATT_FOLD, cols)
        out = x[0]
        for i in range(1, rows // ATT_FOLD):
            out = op(out, x[i])
        return out

    def q_step(qi, _):
        base = pl.multiple_of(qi * tk, tk)

        def scores(kj, buf):
            k_tile = k_ref[0, pl.ds(pl.multiple_of(kj * tk, tk), tk), :]
            s = _dot_nt(k_tile, q_ref[0, pl.ds(base, tk), :])
            s_ref[buf, :, :tk] = s
            mx_ref[buf] = jnp.max(fold_rows(s, jnp.maximum), axis=0, keepdims=True)

        def absorb(kj, buf, masked):
            vt_tile = vt_ref[kj]
            for sub in range(ATT_SPLIT):
                cols = slice(sub * tqs, (sub + 1) * tqs)
                ss = s_ref[buf, :, cols]
                if masked:
                    ss = jnp.where(allowed[sub], ss, -jnp.inf)
                    tile_max = jnp.max(fold_rows(ss, jnp.maximum), axis=0, keepdims=True)
                else:
                    tile_max = mx_ref[buf, :, cols]
                m = m_ref[sub]
                m_new = jnp.maximum(m, tile_max)
                alpha = jnp.exp2(m - m_new)
                p = jnp.exp2(ss - m_new)
                m_ref[sub] = m_new
                l_ref[sub] = alpha * l_ref[sub] + jnp.sum(fold_rows(p, jnp.add), axis=0, keepdims=True)
                acc_ref[sub] = alpha * acc_ref[sub] + _dot(vt_tile, p.astype(BF16))

        def run(first, count, last_masked):
            for j in range(count):
                if j + 1 < count or not last_masked:
                    scores(first + j + 1, (j + 1) % 2)
                absorb(first + j, j % 2, last_masked and j + 1 == count)

        def group(g, _):
            run(g * ATT_UNROLL, ATT_UNROLL, False)
            return 0

        m_ref[...] = jnp.full(m_ref.shape, -jnp.inf, F32)
        l_ref[...] = jnp.zeros(l_ref.shape, F32)
        acc_ref[...] = jnp.zeros(acc_ref.shape, F32)
        scores(0, 0)
        lax.fori_loop(0, qi // ATT_UNROLL, group, 0)
        tail_first = (qi // ATT_UNROLL) * ATT_UNROLL
        for rem in range(ATT_UNROLL):
            @pl.when(qi - tail_first == rem)
            def _(rem=rem):
                run(tail_first, rem + 1, True)

        for sub in range(ATT_SPLIT):
            o_ref[0, pl.ds(pl.multiple_of(base + sub * tqs, tqs), tqs), :] = (
                (acc_ref[sub] / l_ref[sub]).T.astype(o_ref.dtype))
        return 0

    lax.fori_loop(0, n_tiles, q_step, 0)


def _attention(q, k, v, *, batch, seq, heads):
    qk = pl.BlockSpec((1, seq, MLA_HEAD_PAD), lambda b, h: (b, 0, h))
    vo = pl.BlockSpec((1, seq, MLA_V), lambda b, h: (b, 0, h))
    o = pl.pallas_call(
        _attn_body,
        grid=(batch, heads),
        in_specs=[qk, qk, vo],
        out_specs=vo,
        out_shape=jax.ShapeDtypeStruct((batch, seq, heads * MLA_V), BF16),
        scratch_shapes=[pltpu.VMEM((seq // ATT_TILE, MLA_V, ATT_TILE), BF16),
                        pltpu.VMEM((2, ATT_TILE, ATT_TILE + LANES), F32),
                        pltpu.VMEM((2, 1, ATT_TILE), F32),
                        pltpu.VMEM((ATT_SPLIT, 1, ATT_TILE // ATT_SPLIT), F32),
                        pltpu.VMEM((ATT_SPLIT, 1, ATT_TILE // ATT_SPLIT), F32),
                        pltpu.VMEM((ATT_SPLIT, MLA_V, ATT_TILE // ATT_SPLIT), F32)],
        compiler_params=_params(("parallel", "parallel")),
        name="attn",
    )(q.reshape(batch, seq, -1), k.reshape(batch, seq, -1), v.reshape(batch, seq, -1))
    return o.reshape(batch * seq, heads * MLA_V)


def _merge_body(yh_ref, ym_ref, gate_ref, h_ref, wh_ref, wm_ref, wo_ref, o_ref):
    d = h_ref.shape[1]
    mix = (gate_ref[:, :d].astype(F32) * _dot(yh_ref[...], wh_ref[...])
           + gate_ref[:, d:].astype(F32) * _dot(ym_ref[...], wm_ref[...]))
    o_ref[...] = h_ref[...] + _dot(mix.astype(BF16), wo_ref[...])


def _merge(yh, ym, gates, h, w_h, w_m, w_o):
    t, d = h.shape
    tm = TOKEN_TILE

    def row(n):
        return pl.BlockSpec((tm, n), lambda i: (i, 0))

    return pl.pallas_call(
        _merge_body,
        grid=(t // tm,),
        in_specs=[row(yh.shape[1]), row(ym.shape[1]), row(2 * d), row(d), _resident(w_h.shape),
                  _resident(w_m.shape), _resident(w_o.shape)],
        out_specs=row(d),
        out_shape=jax.ShapeDtypeStruct((t, d), F32),
        compiler_params=_params(("parallel",)),
        name="merge",
    )(yh, ym, gates, h, w_h.astype(BF16), w_m.astype(BF16), w_o.astype(BF16))


def kernel(x, positions, ffn1_norm, ffn1_w_in, ffn1_w_out, mix_norm, w_in, hg_lb_table, hg_out_norm, w_hg_branch, mla_q_lora_norm, w_q_up, mla_kv_lora_norm, w_kv_up, q_head_norm, k_head_norm, w_mla_branch, w_merge, b_merge, w_out, ffn2_norm, ffn2_w_in, ffn2_w_out, final_norm):
    batch, seq, d = x.shape
    depth = ffn1_norm.shape[0]
    hg_width = hg_lb_table.shape[1]
    heads = w_q_up.shape[2] // MLA_QK
    assert hg_out_norm.shape[1] == HG_HEAD and w_hg_branch.shape[1] == hg_width
    assert q_head_norm.shape[1] == MLA_QK and w_kv_up.shape[2] == heads * (MLA_NOPE + MLA_V)
    assert seq % max(HG_BLOCK, ATT_TILE) == 0 and (batch * seq) % TOKEN_TILE == 0
    t = batch * seq
    pos = positions.reshape(t, 1)
    h = x.reshape(t, d)
    for l in range(depth):
        h1, u = _ffn(h, ffn1_norm[l], ffn1_w_in[l], ffn1_w_out[l], mix_norm[l], emit_h=True, norm_dtype=BF16)
        hq, lf, kk, hv, hg, gates, q, k, v = _proj(
            u, pos, w_in[l], w_merge[l], b_merge[l], hg_lb_table, mla_q_lora_norm[l], w_q_up[l],
            mla_kv_lora_norm[l], w_kv_up[l], q_head_norm[l], k_head_norm[l], layer=l, hg_width=hg_width,
            heads=heads)
        y_hg = _hgrn(hq, lf, kk, hv, hg, hg_out_norm[l], batch=batch, seq=seq)
        y_mla = _attention(q, k, v, batch=batch, seq=seq, heads=heads)
        h2 = _merge(y_hg, y_mla, gates, h1, w_hg_branch[l], w_mla_branch[l], w_out[l])
        (h,) = _ffn(h2, ffn2_norm[l], ffn2_w_in[l], ffn2_w_out[l], final_norm[l], emit_h=False, norm_dtype=F32)
    return h.reshape(batch, seq, d)
```

```python
import functools

import numpy as np
import jax
import jax.numpy as jnp
from jax import lax
from jax.experimental import pallas as pl
from jax.experimental.pallas import tpu as pltpu

F32 = jnp.float32
BF16 = jnp.bfloat16

EPS = 1e-6
CHUNK = 64
HG_HEAD = 128
MLA_NOPE = 128
MLA_ROPE = 64
MLA_V = 128
MLA_QK = MLA_NOPE + MLA_ROPE
MLA_HEAD_PAD = 256
ROPE_THETA = 10000.0
LOG2E = float(np.log2(np.e))

LANES = 128
TOKEN_TILE = 512
PROJ_TILE = 256
FF_CHUNK = 256
HG_BLOCK = 512
ATT_TILE = 512
ATT_SPLIT = 2
ATT_UNROLL = 4
ATT_FOLD = 32
ATT_ONES = 16
VMEM_LIMIT = 56 * 1024 * 1024


def _rms(x, gain):
    return x * lax.rsqrt(jnp.mean(x * x, axis=-1, keepdims=True) + EPS) * gain


def _silu(x):
    return x * jax.nn.sigmoid(x)


def _dot(a, b):
    return jnp.dot(a, b, preferred_element_type=F32)


def _dot_nt(a, b):
    return lax.dot_general(a, b, (((1,), (1,)), ((), ())), preferred_element_type=F32)


def _dot_tn(a, b):
    return lax.dot_general(a, b, (((0,), (0,)), ((), ())), preferred_element_type=F32)


def _resident(shape):
    zeros = (0,) * len(shape)
    return pl.BlockSpec(shape, lambda *_: zeros, pipeline_mode=pl.Buffered(1))


def _params(semantics):
    return pltpu.CompilerParams(dimension_semantics=semantics, vmem_limit_bytes=VMEM_LIMIT)


def _ffn_body(x_ref, gin_ref, win_ref, wout_ref, gout_ref, *rest, emit_h):
    if emit_h:
        h_ref, n_ref, a_ref = rest
    else:
        n_ref, a_ref = rest
    n_chunks, _, two_fc = win_ref.shape
    fc = two_fc // 2
    x = x_ref[...]
    xn = _rms(x, gin_ref[...]).astype(BF16)
    for c in range(n_chunks):
        hc = _dot(xn, win_ref[c])
        a_ref[:, c * fc:(c + 1) * fc] = (_silu(hc[:, :fc]) * hc[:, fc:]).astype(BF16)
    h = x + 0.5 * _dot(a_ref[...], wout_ref[...])
    if emit_h:
        h_ref[...] = h
    n_ref[...] = _rms(h, gout_ref[...]).astype(n_ref.dtype)


def _ffn(x, gin, w_in, w_out, gout, *, emit_h, norm_dtype):
    t, d = x.shape
    d_ff = w_out.shape[0]
    nc = d_ff // FF_CHUNK
    gate = w_in[:, :d_ff].reshape(d, nc, 1, FF_CHUNK)
    up = w_in[:, d_ff:].reshape(d, nc, 1, FF_CHUNK)
    win = jnp.concatenate([gate, up], axis=2).transpose(1, 0, 2, 3).reshape(nc, d, 2 * FF_CHUNK).astype(BF16)
    tm = TOKEN_TILE
    row = pl.BlockSpec((tm, d), lambda i: (i, 0))
    out_shape = [jax.ShapeDtypeStruct((t, d), norm_dtype)]
    out_specs = [row]
    if emit_h:
        out_shape = [jax.ShapeDtypeStruct((t, d), F32)] + out_shape
        out_specs = [row] + out_specs
    return pl.pallas_call(
        functools.partial(_ffn_body, emit_h=emit_h),
        grid=(t // tm,),
        in_specs=[row, _resident((1, d)), _resident(win.shape), _resident((d_ff, d)), _resident((1, d))],
        out_specs=out_specs,
        out_shape=out_shape,
        scratch_shapes=[pltpu.VMEM((tm, d_ff), BF16)],
        compiler_params=_params(("parallel",)),
        name="ffn1" if emit_h else "ffn2",
    )(x, gin.reshape(1, d), win, w_out.astype(BF16), gout.reshape(1, d))


def _proj_body(u_ref, pos_ref, w_ref, wm_ref, bm_ref, tab_ref, gq_ref, wq_ref, gkv_ref, wkv_ref, qr_gain_ref,
               kn_gain_ref, kr_gain_ref, invf_ref,
               hq_ref, lf_ref, kk_ref, hv_ref, hg_ref, gate_ref, q_ref, k_ref, v_ref, *, layer, heads):
    w = hq_ref.shape[1]
    tm, d = u_ref.shape
    q_lora, kv_lora = wq_ref.shape[0], wkv_ref.shape[0]
    u = u_ref[...]

    c = _dot(u, w_ref[:, 4 * w:])
    qraw = _dot(_rms(c[:, :q_lora], gq_ref[...]).astype(BF16), wq_ref[...])
    kvraw = _dot(_rms(c[:, q_lora:q_lora + kv_lora], gkv_ref[...]).astype(BF16), wkv_ref[...])
    kpe = c[:, q_lora + kv_lora:]

    hq_ref[...] = _silu(_dot(u, w_ref[:, 0:w])).astype(hq_ref.dtype)
    tab = tab_ref[...]
    e = jnp.exp(tab - jnp.max(tab, axis=0, keepdims=True))
    lb = jnp.sum(e[:layer + 1], axis=0, keepdims=True) / jnp.sum(e, axis=0, keepdims=True)
    z = _dot(u, w_ref[:, w:2 * w])
    ez = jnp.exp(-jnp.abs(z))
    r = 1.0 / (1.0 + ez)
    a = ez * r
    pos = z >= 0
    lf_ref[...] = jnp.log(lb + (1.0 - lb) * jnp.where(pos, r, a)) * LOG2E
    kk_ref[...] = ((1.0 - lb) * jnp.where(pos, a, r)).astype(BF16)
    hv_ref[...] = _dot(u, w_ref[:, 2 * w:3 * w]).astype(BF16)
    hg_ref[...] = _silu(_dot(u, w_ref[:, 3 * w:4 * w])).astype(BF16)
    for j in range(2):
        gm = _dot(u, wm_ref[:, j * d:(j + 1) * d]) + bm_ref[:, j * d:(j + 1) * d]
        gate_ref[:, j * d:(j + 1) * d] = jax.nn.sigmoid(gm).astype(BF16)

    ang = pos_ref[...].astype(F32) * invf_ref[...]
    first = lax.broadcasted_iota(jnp.int32, (tm, LANES), 1) < MLA_ROPE
    cs = jnp.where(first, jnp.cos(ang), jnp.sin(ang))

    def rope(block, gain_cs):
        t = block * gain_cs
        return t + pltpu.roll(t, MLA_ROPE, 1)

    def sumsq(x):
        return jnp.sum(x * x, axis=-1, keepdims=True)

    k_rope = rope(kpe, kr_gain_ref[...] * cs)
    k_pe_ss = 0.5 * sumsq(kpe)
    q_gain_cs = qr_gain_ref[...] * cs
    scale = MLA_QK ** -0.5 * LOG2E
    nope_w = heads * MLA_NOPE
    for h in range(heads):
        lo, hi = h * MLA_NOPE, (h + 1) * MLA_NOPE
        qn = qraw[:, lo:hi]
        qr = qraw[:, nope_w + lo:nope_w + hi]
        sq = lax.rsqrt((sumsq(qn) + 0.5 * sumsq(qr)) / MLA_QK + EPS) * scale
        o = h * MLA_HEAD_PAD
        q_ref[:, o:o + MLA_NOPE] = (qn * sq).astype(BF16)
        q_ref[:, o + MLA_NOPE:o + MLA_HEAD_PAD] = (rope(qr, q_gain_cs) * sq).astype(BF16)
        kn = kvraw[:, lo:hi]
        sk = lax.rsqrt((sumsq(kn) + k_pe_ss) / MLA_QK + EPS)
        k_ref[:, o:o + MLA_NOPE] = (kn * sk * kn_gain_ref[...]).astype(BF16)
        k_ref[:, o + MLA_NOPE:o + MLA_HEAD_PAD] = jnp.where(first, k_rope * sk, 0.0).astype(BF16)
    v_ref[...] = kvraw[:, nope_w:].astype(BF16)


def _rot_cols(w):
    half = w.shape[-1] // 2
    return jnp.concatenate([-w[..., half:], w[..., :half]], axis=-1)


def _rot_gain(g):
    half = g.shape[-1] // 2
    return jnp.concatenate([g[..., half:], g[..., :half]], axis=-1)


def _proj(u, pos, w_in, w_merge, b_merge, lb_table, gq, w_q_up, gkv, w_kv_up, q_head_gain, k_head_gain,
          *, layer, hg_width, heads):
    t, d = u.shape
    q_lora, kv_lora = w_q_up.shape[0], w_kv_up.shape[0]
    w_in_ext = jnp.concatenate([w_in, _rot_cols(w_in[:, -MLA_ROPE:])], axis=1).astype(BF16)
    cols = w_in_ext.shape[1]
    assert cols == 4 * hg_width + q_lora + kv_lora + 2 * MLA_ROPE
    wq = w_q_up.reshape(q_lora, heads, MLA_QK)
    wq_rope = wq[:, :, MLA_NOPE:]
    wq_ext = jnp.concatenate(
        [wq[:, :, :MLA_NOPE].reshape(q_lora, heads * MLA_NOPE),
         jnp.concatenate([wq_rope, _rot_cols(wq_rope)], axis=-1).reshape(q_lora, heads * 2 * MLA_ROPE)],
        axis=1).astype(BF16)
    wkv = w_kv_up.reshape(kv_lora, heads, MLA_NOPE + MLA_V)
    wkv_ext = jnp.concatenate([wkv[:, :, :MLA_NOPE].reshape(kv_lora, heads * MLA_NOPE),
                               wkv[:, :, MLA_NOPE:].reshape(kv_lora, heads * MLA_V)], axis=1).astype(BF16)

    def rope_gain(g):
        return jnp.concatenate([g[MLA_NOPE:], _rot_gain(g[MLA_NOPE:])]).reshape(1, 2 * MLA_ROPE)

    inv_freq = ROPE_THETA ** (-jnp.arange(0, MLA_ROPE, 2, dtype=F32) / MLA_ROPE)
    invf = jnp.tile(inv_freq, 4).reshape(1, LANES)
    nope_gain = (q_head_gain[:MLA_NOPE] * k_head_gain[:MLA_NOPE]).reshape(1, MLA_NOPE)
    tm = PROJ_TILE

    def row(n):
        return pl.BlockSpec((tm, n), lambda i: (i, 0))

    qk_w = heads * MLA_HEAD_PAD
    return pl.pallas_call(
        functools.partial(_proj_body, layer=layer, heads=heads),
        grid=(t // tm,),
        in_specs=[row(d), row(1), _resident((d, cols)), _resident((d, 2 * d)), _resident((1, 2 * d)),
                  _resident(lb_table.shape), _resident((1, q_lora)), _resident(wq_ext.shape),
                  _resident((1, kv_lora)), _resident(wkv_ext.shape), _resident((1, LANES)),
                  _resident((1, MLA_NOPE)), _resident((1, LANES)), _resident((1, LANES))],
        out_specs=[row(hg_width), row(hg_width), row(hg_width), row(hg_width), row(hg_width), row(2 * d),
                   row(qk_w), row(qk_w), row(heads * MLA_V)],
        out_shape=[jax.ShapeDtypeStruct((t, hg_width), BF16),
                   jax.ShapeDtypeStruct((t, hg_width), F32),
                   jax.ShapeDtypeStruct((t, hg_width), BF16),
                   jax.ShapeDtypeStruct((t, hg_width), BF16),
                   jax.ShapeDtypeStruct((t, hg_width), BF16),
                   jax.ShapeDtypeStruct((t, 2 * d), BF16),
                   jax.ShapeDtypeStruct((t, qk_w), BF16),
                   jax.ShapeDtypeStruct((t, qk_w), BF16),
                   jax.ShapeDtypeStruct((t, heads * MLA_V), BF16)],
        compiler_params=_params(("parallel",)),
        name="proj",
    )(u, pos, w_in_ext, w_merge.astype(BF16), b_merge.reshape(1, 2 * d), lb_table, gq.reshape(1, q_lora), wq_ext,
      gkv.reshape(1, kv_lora), wkv_ext, rope_gain(q_head_gain), nope_gain, rope_gain(k_head_gain), invf)


def _split3(x):
    hi = x.astype(BF16)
    r = x - hi.astype(F32)
    mid = r.astype(BF16)
    lo = (r - mid.astype(F32)).astype(BF16)
    return hi, mid, lo


def _hgrn_body(q_ref, lf_ref, kk_ref, v_ref, g_ref, gain_ref, tri_ref, y_ref, state_ref):
    @pl.when(pl.program_id(2) == 0)
    def _():
        state_ref[...] = jnp.zeros_like(state_ref)

    n = CHUNK
    tb = q_ref.shape[1]
    chunks = [slice(c * n, (c + 1) * n) for c in range(tb // n)]
    trow = lax.broadcasted_iota(jnp.int32, (n, n), 0)
    tcol = lax.broadcasted_iota(jnp.int32, (n, n), 1)
    tri = tri_ref[...]
    halves = [n >> i for i in range(1, n.bit_length())]
    pairs = {half: (trow // (2 * half) == tcol // (2 * half)) & (trow % (2 * half) >= half)
             & (tcol % (2 * half) < half) for half in halves}

    def ref_rows(x, size, row):
        blocks = x.reshape(tb // size, size, HG_HEAD)[:, row:row + 1, :]
        return jnp.broadcast_to(blocks, (tb // size, size, HG_HEAD)).reshape(tb, HG_HEAD)

    q = q_ref[0].astype(F32)
    lf = lf_ref[0]
    k = kk_ref[0].astype(F32)
    v = v_ref[0]
    lf3 = lf.reshape(tb // 8, 8, HG_HEAD)
    r8 = lax.broadcasted_iota(jnp.int32, (1, 8, HG_HEAD), 1)
    split = jnp.concatenate(_split3(lf), axis=1)
    cums = []
    for sl in chunks:
        r = _dot(tri, split[sl])
        cums.append(r[:, :HG_HEAD] + r[:, HG_HEAD:2 * HG_HEAD] + r[:, 2 * HG_HEAD:])
    cum = jnp.concatenate(cums, axis=0)
    diag = jnp.sum(q * k, axis=-1, keepdims=True)
    scores = [jnp.where(trow == tcol, diag[sl], 0.0) for sl in chunks]
    for half in halves:
        size = 2 * half
        if half >= 4:
            diff = pltpu.bitcast(cum - ref_rows(cum, size, half - 1), jnp.uint32)
            dec = pltpu.bitcast(diff | jnp.uint32(0x80000000), F32)
        elif half == 2:
            dec = jnp.where(r8 % 4 == 3, lf3 + pltpu.roll(lf3, 1, 1),
                            jnp.where(r8 % 4 == 2, lf3,
                                      jnp.where(r8 % 4 == 0, pltpu.roll(lf3, 7, 1), 0.0))).reshape(tb, HG_HEAD)
        else:
            dec = jnp.where(r8 % 2 == 1, lf3, 0.0).reshape(tb, HG_HEAD)
        e = jnp.exp2(dec)
        qe = (q * e).astype(BF16)
        ke = (k * e).astype(BF16)
        for c, sl in enumerate(chunks):
            scores[c] = scores[c] + jnp.where(pairs[half], _dot_nt(qe[sl], ke[sl]), 0.0)
    last = ref_rows(cum, n, n - 1)
    qdec = (q * jnp.exp2(cum)).astype(BF16)
    kdec = (k * jnp.exp2(last - cum)).astype(BF16)
    updates = [_dot_tn(v[sl], kdec[sl]) for sl in chunks]
    state = state_ref[...]
    outs = []
    for c, sl in enumerate(chunks):
        outs.append(_dot(scores[c].astype(BF16), v[sl]) + _dot_nt(qdec[sl], state.astype(BF16)))
        state = state * jnp.exp2(cum[sl][n - 1:n, :]) + updates[c]
    state_ref[...] = state
    o = jnp.concatenate(outs, axis=0)
    y_ref[0] = (_rms(o, gain_ref[...]) * g_ref[0].astype(F32)).astype(y_ref.dtype)


def _hgrn(hq, lf, kk, v, g, out_gain, *, batch, seq):
    width = hq.shape[1]
    heads = width // HG_HEAD
    tb = HG_BLOCK
    tri = jnp.asarray(np.tril(np.ones((CHUNK, CHUNK), np.float32)), BF16)

    def r3(a):
        return a.reshape(batch, seq, width)

    blk = pl.BlockSpec((1, tb, HG_HEAD), lambda b, h, j: (b, j, h))
    y = pl.pallas_call(
        _hgrn_body,
        grid=(batch, heads, seq // tb),
        in_specs=[blk, blk, blk, blk, blk, _resident((1, HG_HEAD)), _resident((CHUNK, CHUNK))],
        out_specs=blk,
        out_shape=jax.ShapeDtypeStruct((batch, seq, width), BF16),
        scratch_shapes=[pltpu.VMEM((HG_HEAD, HG_HEAD), F32)],
        compiler_params=_params(("parallel", "parallel", "arbitrary")),
        name="hgrn",
    )(r3(hq), r3(lf), r3(kk), r3(v), r3(g), out_gain.reshape(1, HG_HEAD), tri)
    return y.reshape(batch * seq, width)


def _attn_body(q_ref, k_ref, v_ref, o_ref, vt_ref, s_ref, mx_ref, m_ref, acc_ref):
    seq = q_ref.shape[1]
    tk = ATT_TILE
    tqs = ATT_TILE // ATT_SPLIT
    n_tiles = seq // tk

    def transpose_v(j, _):
        vt_ref[j, :MLA_V] = v_ref[0, pl.ds(pl.multiple_of(j * tk, tk), tk), :].astype(F32).T.astype(BF16)
        vt_ref[j, MLA_V:] = jnp.ones((ATT_ONES, tk), BF16)
        return 0

    lax.fori_loop(0, n_tiles, transpose_v, 0)

    key_chunk = lax.broadcasted_iota(jnp.int32, (tk, tqs), 0) // CHUNK
    qry_chunk = lax.broadcasted_iota(jnp.int32, (tk, tqs), 1) // CHUNK
    allowed = [key_chunk <= qry_chunk + (sub * tqs) // CHUNK for sub in range(ATT_SPLIT)]

    def fold_rows(x, op):
        rows, cols = x.shape
        x = x.reshape(rows // ATT_FOLD, ATT_FOLD, cols)
        out = x[0]
        for i in range(1, rows // ATT_FOLD):
            out = op(out, x[i])
        return out

    def q_step(qi, _):
        base = pl.multiple_of(qi * tk, tk)

        def scores(kj, buf):
            k_tile = k_ref[0, pl.ds(pl.multiple_of(kj * tk, tk), tk), :]
            s = _dot_nt(k_tile, q_ref[0, pl.ds(base, tk), :])
            s_ref[buf] = s
            mx_ref[buf] = jnp.max(fold_rows(s, jnp.maximum), axis=0, keepdims=True)

        def absorb(kj, buf, masked):
            vt_tile = vt_ref[kj]
            for sub in range(ATT_SPLIT):
                cols = slice(sub * tqs, (sub + 1) * tqs)
                ss = s_ref[buf, :, cols]
                if masked:
                    ss = jnp.where(allowed[sub], ss, -jnp.inf)
                    tile_max = jnp.max(fold_rows(ss, jnp.maximum), axis=0, keepdims=True)
                else:
                    tile_max = mx_ref[buf, :, cols]
                m = m_ref[sub]
                m_new = jnp.maximum(m, tile_max)
                alpha = jnp.exp2(m - m_new)
                p = jnp.exp2(ss - m_new)
                m_ref[sub] = m_new
                acc_ref[sub] = alpha * acc_ref[sub] + _dot(vt_tile, p.astype(BF16))

        def run(first, count, last_masked):
            for j in range(count):
                if j + 1 < count or not last_masked:
                    scores(first + j + 1, (j + 1) % 2)
                absorb(first + j, j % 2, last_masked and j + 1 == count)

        def group(g, _):
            run(g * ATT_UNROLL, ATT_UNROLL, False)
            return 0

        m_ref[...] = jnp.full(m_ref.shape, -jnp.inf, F32)
        acc_ref[...] = jnp.zeros(acc_ref.shape, F32)
        scores(0, 0)
        lax.fori_loop(0, qi // ATT_UNROLL, group, 0)
        tail_first = (qi // ATT_UNROLL) * ATT_UNROLL
        for rem in range(ATT_UNROLL):
            @pl.when(qi - tail_first == rem)
            def _(rem=rem):
                run(tail_first, rem + 1, True)

        for sub in range(ATT_SPLIT):
            acc = acc_ref[sub]
            o_ref[0, pl.ds(pl.multiple_of(base + sub * tqs, tqs), tqs), :] = (
                (acc[:MLA_V] / acc[MLA_V:MLA_V + 1]).T.astype(o_ref.dtype))
        return 0

    lax.fori_loop(0, n_tiles, q_step, 0)


def _attention(q, k, v, *, batch, seq, heads):
    qk = pl.BlockSpec((1, seq, MLA_HEAD_PAD), lambda b, h: (b, 0, h))
    vo = pl.BlockSpec((1, seq, MLA_V), lambda b, h: (b, 0, h))
    o = pl.pallas_call(
        _attn_body,
        grid=(batch, heads),
        in_specs=[qk, qk, vo],
        out_specs=vo,
        out_shape=jax.ShapeDtypeStruct((batch, seq, heads * MLA_V), BF16),
        scratch_shapes=[pltpu.VMEM((seq // ATT_TILE, MLA_V + ATT_ONES, ATT_TILE), BF16),
                        pltpu.VMEM((2, ATT_TILE, ATT_TILE), F32),
                        pltpu.VMEM((2, 1, ATT_TILE), F32),
                        pltpu.VMEM((ATT_SPLIT, 1, ATT_TILE // ATT_SPLIT), F32),
                        pltpu.VMEM((ATT_SPLIT, MLA_V + ATT_ONES, ATT_TILE // ATT_SPLIT), F32)],
        compiler_params=_params(("parallel", "parallel")),
        name="attn",
    )(q.reshape(batch, seq, -1), k.reshape(batch, seq, -1), v.reshape(batch, seq, -1))
    return o.reshape(batch * seq, heads * MLA_V)


def _merge_body(yh_ref, ym_ref, gate_ref, h_ref, wh_ref, wm_ref, wo_ref, o_ref):
    d = h_ref.shape[1]
    mix = (gate_ref[:, :d].astype(F32) * _dot(yh_ref[...], wh_ref[...])
           + gate_ref[:, d:].astype(F32) * _dot(ym_ref[...], wm_ref[...]))
    o_ref[...] = h_ref[...] + _dot(mix.astype(BF16), wo_ref[...])


def _merge(yh, ym, gates, h, w_h, w_m, w_o):
    t, d = h.shape
    tm = TOKEN_TILE

    def row(n):
        return pl.BlockSpec((tm, n), lambda i: (i, 0))

    return pl.pallas_call(
        _merge_body,
        grid=(t // tm,),
        in_specs=[row(yh.shape[1]), row(ym.shape[1]), row(2 * d), row(d), _resident(w_h.shape),
                  _resident(w_m.shape), _resident(w_o.shape)],
        out_specs=row(d),
        out_shape=jax.ShapeDtypeStruct((t, d), F32),
        compiler_params=_params(("parallel",)),
        name="merge",
    )(yh, ym, gates, h, w_h.astype(BF16), w_m.astype(BF16), w_o.astype(BF16))


def kernel(x, positions, ffn1_norm, ffn1_w_in, ffn1_w_out, mix_norm, w_in, hg_lb_table, hg_out_norm, w_hg_branch, mla_q_lora_norm, w_q_up, mla_kv_lora_norm, w_kv_up, q_head_norm, k_head_norm, w_mla_branch, w_merge, b_merge, w_out, ffn2_norm, ffn2_w_in, ffn2_w_out, final_norm):
    batch, seq, d = x.shape
    depth = ffn1_norm.shape[0]
    hg_width = hg_lb_table.shape[1]
    heads = w_q_up.shape[2] // MLA_QK
    assert hg_out_norm.shape[1] == HG_HEAD and w_hg_branch.shape[1] == hg_width
    assert q_head_norm.shape[1] == MLA_QK and w_kv_up.shape[2] == heads * (MLA_NOPE + MLA_V)
    assert seq % max(HG_BLOCK, ATT_TILE) == 0 and (batch * seq) % TOKEN_TILE == 0
    t = batch * seq
    pos = positions.reshape(t, 1)
    h = x.reshape(t, d)
    for l in range(depth):
        h1, u = _ffn(h, ffn1_norm[l], ffn1_w_in[l], ffn1_w_out[l], mix_norm[l], emit_h=True, norm_dtype=BF16)
        hq, lf, kk, hv, hg, gates, q, k, v = _proj(
            u, pos, w_in[l], w_merge[l], b_merge[l], hg_lb_table, mla_q_lora_norm[l], w_q_up[l],
            mla_kv_lora_norm[l], w_kv_up[l], q_head_norm[l], k_head_norm[l], layer=l, hg_width=hg_width,
            heads=heads)
        y_hg = _hgrn(hq, lf, kk, hv, hg, hg_out_norm[l], batch=batch, seq=seq)
        y_mla = _attention(q, k, v, batch=batch, seq=seq, heads=heads)
        h2 = _merge(y_hg, y_mla, gates, h1, w_hg_branch[l], w_mla_branch[l], w_out[l])
        (h,) = _ffn(h2, ffn2_norm[l], ffn2_w_in[l], ffn2_w_out[l], final_norm[l], emit_h=False, norm_dtype=F32)
    return h.reshape(batch, seq, d)
```

```python
import functools

import numpy as np
import jax
import jax.numpy as jnp
from jax import lax
from jax.experimental import pallas as pl
from jax.experimental.pallas import tpu as pltpu

F32 = jnp.float32
BF16 = jnp.bfloat16

EPS = 1e-6
CHUNK = 64
HG_HEAD = 128
MLA_NOPE = 128
MLA_ROPE = 64
MLA_V = 128
MLA_QK = MLA_NOPE + MLA_ROPE
MLA_HEAD_PAD = 256
ROPE_THETA = 10000.0
LOG2E = float(np.log2(np.e))

LANES = 128
TOKEN_TILE = 512
PROJ_TILE = 256
FF_CHUNK = 256
HG_BLOCK = 512
ATT_TILE = 512
ATT_SPLIT = 2
ATT_UNROLL = 4
ATT_FOLD = 32
ATT_ONES = 16
VMEM_LIMIT = 56 * 1024 * 1024


def _rms(x, gain):
    return x * lax.rsqrt(jnp.mean(x * x, axis=-1, keepdims=True) + EPS) * gain


def _silu(x):
    return x * jax.nn.sigmoid(x)


def _dot(a, b):
    return jnp.dot(a, b, preferred_element_type=F32)


def _dot_nt(a, b):
    return lax.dot_general(a, b, (((1,), (1,)), ((), ())), preferred_element_type=F32)


def _dot_tn(a, b):
    return lax.dot_general(a, b, (((0,), (0,)), ((), ())), preferred_element_type=F32)


def _resident(shape):
    zeros = (0,) * len(shape)
    return pl.BlockSpec(shape, lambda *_: zeros, pipeline_mode=pl.Buffered(1))


def _params(semantics):
    return pltpu.CompilerParams(dimension_semantics=semantics, vmem_limit_bytes=VMEM_LIMIT)


def _ffn_body(x_ref, gin_ref, win_ref, wout_ref, gout_ref, *rest, emit_h):
    if emit_h:
        h_ref, n_ref, a_ref = rest
    else:
        n_ref, a_ref = rest
    n_chunks, _, two_fc = win_ref.shape
    fc = two_fc // 2
    x = x_ref[...]
    xn = _rms(x, gin_ref[...]).astype(BF16)
    for c in range(n_chunks):
        hc = _dot(xn, win_ref[c])
        a_ref[:, c * fc:(c + 1) * fc] = (_silu(hc[:, :fc]) * hc[:, fc:]).astype(BF16)
    h = x + 0.5 * _dot(a_ref[...], wout_ref[...])
    if emit_h:
        h_ref[...] = h
    n_ref[...] = _rms(h, gout_ref[...]).astype(n_ref.dtype)


def _ffn(x, gin, w_in, w_out, gout, *, emit_h, norm_dtype):
    t, d = x.shape
    d_ff = w_out.shape[0]
    nc = d_ff // FF_CHUNK
    gate = w_in[:, :d_ff].reshape(d, nc, 1, FF_CHUNK)
    up = w_in[:, d_ff:].reshape(d, nc, 1, FF_CHUNK)
    win = jnp.concatenate([gate, up], axis=2).transpose(1, 0, 2, 3).reshape(nc, d, 2 * FF_CHUNK).astype(BF16)
    tm = TOKEN_TILE
    row = pl.BlockSpec((tm, d), lambda i: (i, 0))
    out_shape = [jax.ShapeDtypeStruct((t, d), norm_dtype)]
    out_specs = [row]
    if emit_h:
        out_shape = [jax.ShapeDtypeStruct((t, d), F32)] + out_shape
        out_specs = [row] + out_specs
    return pl.pallas_call(
        functools.partial(_ffn_body, emit_h=emit_h),
        grid=(t // tm,),
        in_specs=[row, _resident((1, d)), _resident(win.shape), _resident((d_ff, d)), _resident((1, d))],
        out_specs=out_specs,
        out_shape=out_shape,
        scratch_shapes=[pltpu.VMEM((tm, d_ff), BF16)],
        compiler_params=_params(("parallel",)),
        name="ffn1" if emit_h else "ffn2",
    )(x, gin.reshape(1, d), win, w_out.astype(BF16), gout.reshape(1, d))


def _proj_body(u_ref, pos_ref, w_ref, wm_ref, bm_ref, tab_ref, gq_ref, wq_ref, gkv_ref, wkv_ref, qr_gain_ref,
               kn_gain_ref, kr_gain_ref, invf_ref,
               hq_ref, lf_ref, kk_ref, hv_ref, hg_ref, gate_ref, q_ref, k_ref, v_ref, *, layer, heads):
    w = hq_ref.shape[1]
    tm, d = u_ref.shape
    q_lora, kv_lora = wq_ref.shape[0], wkv_ref.shape[0]
    u = u_ref[...]

    c = _dot(u, w_ref[:, 4 * w:])
    qraw = _dot(_rms(c[:, :q_lora], gq_ref[...]).astype(BF16), wq_ref[...])
    kvraw = _dot(_rms(c[:, q_lora:q_lora + kv_lora], gkv_ref[...]).astype(BF16), wkv_ref[...])
    kpe = c[:, q_lora + kv_lora:]

    hq_ref[...] = _silu(_dot(u, w_ref[:, 0:w])).astype(hq_ref.dtype)
    tab = tab_ref[...]
    e = jnp.exp(tab - jnp.max(tab, axis=0, keepdims=True))
    lb = jnp.sum(e[:layer + 1], axis=0, keepdims=True) / jnp.sum(e, axis=0, keepdims=True)
    z = _dot(u, w_ref[:, w:2 * w])
    ez = jnp.exp(-jnp.abs(z))
    r = 1.0 / (1.0 + ez)
    a = ez * r
    pos = z >= 0
    lf_ref[...] = jnp.log(lb + (1.0 - lb) * jnp.where(pos, r, a)) * LOG2E
    kk_ref[...] = ((1.0 - lb) * jnp.where(pos, a, r)).astype(BF16)
    hv_ref[...] = _dot(u, w_ref[:, 2 * w:3 * w]).astype(BF16)
    hg_ref[...] = _silu(_dot(u, w_ref[:, 3 * w:4 * w])).astype(BF16)
    for j in range(2):
        gm = _dot(u, wm_ref[:, j * d:(j + 1) * d]) + bm_ref[:, j * d:(j + 1) * d]
        gate_ref[:, j * d:(j + 1) * d] = jax.nn.sigmoid(gm).astype(BF16)

    ang = pos_ref[...].astype(F32) * invf_ref[...]
    first = lax.broadcasted_iota(jnp.int32, (tm, LANES), 1) < MLA_ROPE
    cs = jnp.where(first, jnp.cos(ang), jnp.sin(ang))

    def rope(block, gain_cs):
        t = block * gain_cs
        return t + pltpu.roll(t, MLA_ROPE, 1)

    def sumsq(x):
        return jnp.sum(x * x, axis=-1, keepdims=True)

    k_rope = rope(kpe, kr_gain_ref[...] * cs)
    k_pe_ss = 0.5 * sumsq(kpe)
    q_gain_cs = qr_gain_ref[...] * cs
    scale = MLA_QK ** -0.5 * LOG2E
    nope_w = heads * MLA_NOPE
    for h in range(heads):
        lo, hi = h * MLA_NOPE, (h + 1) * MLA_NOPE
        qn = qraw[:, lo:hi]
        qr = qraw[:, nope_w + lo:nope_w + hi]
        sq = lax.rsqrt((sumsq(qn) + 0.5 * sumsq(qr)) / MLA_QK + EPS) * scale
        o = h * MLA_HEAD_PAD
        q_ref[:, o:o + MLA_NOPE] = (qn * sq).astype(BF16)
        q_ref[:, o + MLA_NOPE:o + MLA_HEAD_PAD] = (rope(qr, q_gain_cs) * sq).astype(BF16)
        kn = kvraw[:, lo:hi]
        sk = lax.rsqrt((sumsq(kn) + k_pe_ss) / MLA_QK + EPS)
        k_ref[:, o:o + MLA_NOPE] = (kn * sk * kn_gain_ref[...]).astype(BF16)
        k_ref[:, o + MLA_NOPE:o + MLA_HEAD_PAD] = jnp.where(first, k_rope * sk, 0.0).astype(BF16)
    v_ref[...] = kvraw[:, nope_w:].astype(BF16)


def _rot_cols(w):
    half = w.shape[-1] // 2
    return jnp.concatenate([-w[..., half:], w[..., :half]], axis=-1)


def _rot_gain(g):
    half = g.shape[-1] // 2
    return jnp.concatenate([g[..., half:], g[..., :half]], axis=-1)


def _proj(u, pos, w_in, w_merge, b_merge, lb_table, gq, w_q_up, gkv, w_kv_up, q_head_gain, k_head_gain,
          *, layer, hg_width, heads):
    t, d = u.shape
    q_lora, kv_lora = w_q_up.shape[0], w_kv_up.shape[0]
    w_in_ext = jnp.concatenate([w_in, _rot_cols(w_in[:, -MLA_ROPE:])], axis=1).astype(BF16)
    cols = w_in_ext.shape[1]
    assert cols == 4 * hg_width + q_lora + kv_lora + 2 * MLA_ROPE
    wq = w_q_up.reshape(q_lora, heads, MLA_QK)
    wq_rope = wq[:, :, MLA_NOPE:]
    wq_ext = jnp.concatenate(
        [wq[:, :, :MLA_NOPE].reshape(q_lora, heads * MLA_NOPE),
         jnp.concatenate([wq_rope, _rot_cols(wq_rope)], axis=-1).reshape(q_lora, heads * 2 * MLA_ROPE)],
        axis=1).astype(BF16)
    wkv = w_kv_up.reshape(kv_lora, heads, MLA_NOPE + MLA_V)
    wkv_ext = jnp.concatenate([wkv[:, :, :MLA_NOPE].reshape(kv_lora, heads * MLA_NOPE),
                               wkv[:, :, MLA_NOPE:].reshape(kv_lora, heads * MLA_V)], axis=1).astype(BF16)

    def rope_gain(g):
        return jnp.concatenate([g[MLA_NOPE:], _rot_gain(g[MLA_NOPE:])]).reshape(1, 2 * MLA_ROPE)

    inv_freq = ROPE_THETA ** (-jnp.arange(0, MLA_ROPE, 2, dtype=F32) / MLA_ROPE)
    invf = jnp.tile(inv_freq, 4).reshape(1, LANES)
    nope_gain = (q_head_gain[:MLA_NOPE] * k_head_gain[:MLA_NOPE]).reshape(1, MLA_NOPE)
    tm = PROJ_TILE

    def row(n):
        return pl.BlockSpec((tm, n), lambda i: (i, 0))

    qk_w = heads * MLA_HEAD_PAD
    return pl.pallas_call(
        functools.partial(_proj_body, layer=layer, heads=heads),
        grid=(t // tm,),
        in_specs=[row(d), row(1), _resident((d, cols)), _resident((d, 2 * d)), _resident((1, 2 * d)),
                  _resident(lb_table.shape), _resident((1, q_lora)), _resident(wq_ext.shape),
                  _resident((1, kv_lora)), _resident(wkv_ext.shape), _resident((1, LANES)),
                  _resident((1, MLA_NOPE)), _resident((1, LANES)), _resident((1, LANES))],
        out_specs=[row(hg_width), row(hg_width), row(hg_width), row(hg_width), row(hg_width), row(2 * d),
                   row(qk_w), row(qk_w), row(heads * MLA_V)],
        out_shape=[jax.ShapeDtypeStruct((t, hg_width), BF16),
                   jax.ShapeDtypeStruct((t, hg_width), F32),
                   jax.ShapeDtypeStruct((t, hg_width), BF16),
                   jax.ShapeDtypeStruct((t, hg_width), BF16),
                   jax.ShapeDtypeStruct((t, hg_width), BF16),
                   jax.ShapeDtypeStruct((t, 2 * d), BF16),
                   jax.ShapeDtypeStruct((t, qk_w), BF16),
                   jax.ShapeDtypeStruct((t, qk_w), BF16),
                   jax.ShapeDtypeStruct((t, heads * MLA_V), BF16)],
        compiler_params=_params(("parallel",)),
        name="proj",
    )(u, pos, w_in_ext, w_merge.astype(BF16), b_merge.reshape(1, 2 * d), lb_table, gq.reshape(1, q_lora), wq_ext,
      gkv.reshape(1, kv_lora), wkv_ext, rope_gain(q_head_gain), nope_gain, rope_gain(k_head_gain), invf)


def _split3(x):
    hi = x.astype(BF16)
    r = x - hi.astype(F32)
    mid = r.astype(BF16)
    lo = (r - mid.astype(F32)).astype(BF16)
    return hi, mid, lo


def _hgrn_body(q_ref, lf_ref, kk_ref, v_ref, g_ref, gain_ref, tri_ref, y_ref, state_ref):
    @pl.when(pl.program_id(2) == 0)
    def _():
        state_ref[...] = jnp.zeros_like(state_ref)

    n = CHUNK
    tb = q_ref.shape[1]
    chunks = [slice(c * n, (c + 1) * n) for c in range(tb // n)]
    trow = lax.broadcasted_iota(jnp.int32, (n, n), 0)
    tcol = lax.broadcasted_iota(jnp.int32, (n, n), 1)
    tri = tri_ref[...]
    halves = [n >> i for i in range(1, n.bit_length())]
    pairs = {half: (trow // (2 * half) == tcol // (2 * half)) & (trow % (2 * half) >= half)
             & (tcol % (2 * half) < half) for half in halves}

    def ref_rows(x, size, row):
        blocks = x.reshape(tb // size, size, HG_HEAD)[:, row:row + 1, :]
        return jnp.broadcast_to(blocks, (tb // size, size, HG_HEAD)).reshape(tb, HG_HEAD)

    q = q_ref[0].astype(F32)
    lf = lf_ref[0]
    k = kk_ref[0].astype(F32)
    v = v_ref[0]
    lf3 = lf.reshape(tb // 8, 8, HG_HEAD)
    r8 = lax.broadcasted_iota(jnp.int32, (1, 8, HG_HEAD), 1)
    split = jnp.concatenate(_split3(lf), axis=1)
    cums = []
    for sl in chunks:
        r = _dot(tri, split[sl])
        cums.append(r[:, :HG_HEAD] + r[:, HG_HEAD:2 * HG_HEAD] + r[:, 2 * HG_HEAD:])
    cum = jnp.concatenate(cums, axis=0)
    diag = jnp.sum(q * k, axis=-1, keepdims=True)
    scores = [jnp.where(trow == tcol, diag[sl], 0.0) for sl in chunks]
    for half in halves:
        size = 2 * half
        if half >= 4:
            diff = pltpu.bitcast(cum - ref_rows(cum, size, half - 1), jnp.uint32)
            dec = pltpu.bitcast(diff | jnp.uint32(0x80000000), F32)
        elif half == 2:
            dec = jnp.where(r8 % 4 == 3, lf3 + pltpu.roll(lf3, 1, 1),
                            jnp.where(r8 % 4 == 2, lf3,
                                      jnp.where(r8 % 4 == 0, pltpu.roll(lf3, 7, 1), 0.0))).reshape(tb, HG_HEAD)
        else:
            dec = jnp.where(r8 % 2 == 1, lf3, 0.0).reshape(tb, HG_HEAD)
        e = jnp.exp2(dec)
        qe = (q * e).astype(BF16)
        ke = (k * e).astype(BF16)
        for c, sl in enumerate(chunks):
            scores[c] = scores[c] + jnp.where(pairs[half], _dot_nt(qe[sl], ke[sl]), 0.0)
    last = ref_rows(cum, n, n - 1)
    qdec = (q * jnp.exp2(cum)).astype(BF16)
    kdec = (k * jnp.exp2(last - cum)).astype(BF16)
    updates = [_dot_tn(v[sl], kdec[sl]) for sl in chunks]
    state = state_ref[...]
    outs = []
    for c, sl in enumerate(chunks):
        outs.append(_dot(scores[c].astype(BF16), v[sl]) + _dot_nt(qdec[sl], state.astype(BF16)))
        state = state * jnp.exp2(cum[sl][n - 1:n, :]) + updates[c]
    state_ref[...] = state
    o = jnp.concatenate(outs, axis=0)
    y_ref[0] = (_rms(o, gain_ref[...]) * g_ref[0].astype(F32)).astype(y_ref.dtype)


def _hgrn(hq, lf, kk, v, g, out_gain, *, batch, seq):
    width = hq.shape[1]
    heads = width // HG_HEAD
    tb = HG_BLOCK
    tri = jnp.asarray(np.tril(np.ones((CHUNK, CHUNK), np.float32)), BF16)

    def r3(a):
        return a.reshape(batch, seq, width)

    blk = pl.BlockSpec((1, tb, HG_HEAD), lambda b, h, j: (b, j, h))
    y = pl.pallas_call(
        _hgrn_body,
        grid=(batch, heads, seq // tb),
        in_specs=[blk, blk, blk, blk, blk, _resident((1, HG_HEAD)), _resident((CHUNK, CHUNK))],
        out_specs=blk,
        out_shape=jax.ShapeDtypeStruct((batch, seq, width), BF16),
        scratch_shapes=[pltpu.VMEM((HG_HEAD, HG_HEAD), F32)],
        compiler_params=_params(("parallel", "parallel", "arbitrary")),
        name="hgrn",
    )(r3(hq), r3(lf), r3(kk), r3(v), r3(g), out_gain.reshape(1, HG_HEAD), tri)
    return y.reshape(batch * seq, width)


def _attn_body(q_ref, k_ref, v_ref, o_ref, vt_ref, s_ref, mx_ref, m_ref, acc_ref):
    seq = q_ref.shape[1]
    tk = ATT_TILE
    tqs = ATT_TILE // ATT_SPLIT
    n_tiles = seq // tk

    def transpose_v(j, _):
        vt_ref[j, :MLA_V] = v_ref[0, pl.ds(pl.multiple_of(j * tk, tk), tk), :].astype(F32).T.astype(BF16)
        vt_ref[j, MLA_V:] = jnp.ones((ATT_ONES, tk), BF16)
        return 0

    lax.fori_loop(0, n_tiles, transpose_v, 0)

    key_chunk = lax.broadcasted_iota(jnp.int32, (tk, tqs), 0) // CHUNK
    qry_chunk = lax.broadcasted_iota(jnp.int32, (tk, tqs), 1) // CHUNK
    allowed = [key_chunk <= qry_chunk + (sub * tqs) // CHUNK for sub in range(ATT_SPLIT)]

    def fold_rows(x, op):
        rows, cols = x.shape
        x = x.reshape(rows // ATT_FOLD, ATT_FOLD, cols)
        out = x[0]
        for i in range(1, rows // ATT_FOLD):
            out = op(out, x[i])
        return out

    def aligned(x, m):
        return x if isinstance(x, int) else pl.multiple_of(x, m)

    def scores(base, kj, buf):
        k_tile = k_ref[0, pl.ds(aligned(kj * tk, tk), tk), :]
        s = _dot_nt(k_tile, q_ref[0, pl.ds(base, tk), :])
        s_ref[buf] = s
        mx_ref[buf] = jnp.max(fold_rows(s, jnp.maximum), axis=0, keepdims=True)

    def start():
        m_ref[...] = jnp.full(m_ref.shape, -jnp.inf, F32)
        acc_ref[...] = jnp.zeros(acc_ref.shape, F32)

    def finish(base):
        for sub in range(ATT_SPLIT):
            acc = acc_ref[sub]
            o_ref[0, pl.ds(aligned(base + sub * tqs, tqs), tqs), :] = (
                (acc[:MLA_V] / acc[MLA_V:MLA_V + 1]).T.astype(o_ref.dtype))

    def absorb(kj, buf, masked):
        vt_tile = vt_ref[kj]
        for sub in range(ATT_SPLIT):
            cols = slice(sub * tqs, (sub + 1) * tqs)
            ss = s_ref[buf, :, cols]
            if masked:
                ss = jnp.where(allowed[sub], ss, -jnp.inf)
                tile_max = jnp.max(fold_rows(ss, jnp.maximum), axis=0, keepdims=True)
            else:
                tile_max = mx_ref[buf, :, cols]
            m = m_ref[sub]
            m_new = jnp.maximum(m, tile_max)
            alpha = jnp.exp2(m - m_new)
            p = jnp.exp2(ss - m_new)
            m_ref[sub] = m_new
            acc_ref[sub] = alpha * acc_ref[sub] + _dot(vt_tile, p.astype(BF16))

    def run(base, next_base, first, count, last_masked):
        for j in range(count):
            if last_masked and j + 1 == count:
                scores(next_base, 0, 2)
            else:
                scores(base, first + j + 1, j % 2)
            absorb(first + j, (j + 1) % 2, last_masked and j + 1 == count)

    def q_step(qi, _):
        base = pl.multiple_of(qi * tk, tk)
        next_base = pl.multiple_of(jnp.minimum(qi + 1, n_tiles - 1) * tk, tk)
        start()
        scores(base, 1, 1)
        absorb(0, 2, False)
        full = qi - 1

        def group(g, _):
            run(base, next_base, 1 + g * ATT_UNROLL, ATT_UNROLL, False)
            return 0

        lax.fori_loop(0, full // ATT_UNROLL, group, 0)
        tail_first = 1 + (full // ATT_UNROLL) * ATT_UNROLL
        for rem in range(ATT_UNROLL):
            @pl.when(full % ATT_UNROLL == rem)
            def _(rem=rem):
                run(base, next_base, tail_first, rem + 1, True)

        finish(base)
        return 0

    start()
    scores(0, 0, 0)
    scores(min(1, n_tiles - 1) * tk, 0, 2)
    absorb(0, 0, True)
    finish(0)
    lax.fori_loop(1, n_tiles, q_step, 0)


def _attention(q, k, v, *, batch, seq, heads):
    qk = pl.BlockSpec((1, seq, MLA_HEAD_PAD), lambda b, h: (b, 0, h))
    vo = pl.BlockSpec((1, seq, MLA_V), lambda b, h: (b, 0, h))
    o = pl.pallas_call(
        _attn_body,
        grid=(batch, heads),
        in_specs=[qk, qk, vo],
        out_specs=vo,
        out_shape=jax.ShapeDtypeStruct((batch, seq, heads * MLA_V), BF16),
        scratch_shapes=[pltpu.VMEM((seq // ATT_TILE, MLA_V + ATT_ONES, ATT_TILE), BF16),
                        pltpu.VMEM((3, ATT_TILE, ATT_TILE), F32),
                        pltpu.VMEM((3, 1, ATT_TILE), F32),
                        pltpu.VMEM((ATT_SPLIT, 1, ATT_TILE // ATT_SPLIT), F32),
                        pltpu.VMEM((ATT_SPLIT, MLA_V + ATT_ONES, ATT_TILE // ATT_SPLIT), F32)],
        compiler_params=_params(("parallel", "parallel")),
        name="attn",
    )(q.reshape(batch, seq, -1), k.reshape(batch, seq, -1), v.reshape(batch, seq, -1))
    return o.reshape(batch * seq, heads * MLA_V)


def _merge_body(yh_ref, ym_ref, gate_ref, h_ref, wh_ref, wm_ref, wo_ref, o_ref):
    d = h_ref.shape[1]
    mix = (gate_ref[:, :d].astype(F32) * _dot(yh_ref[...], wh_ref[...])
           + gate_ref[:, d:].astype(F32) * _dot(ym_ref[...], wm_ref[...]))
    o_ref[...] = h_ref[...] + _dot(mix.astype(BF16), wo_ref[...])


def _merge(yh, ym, gates, h, w_h, w_m, w_o):
    t, d = h.shape
    tm = TOKEN_TILE

    def row(n):
        return pl.BlockSpec((tm, n), lambda i: (i, 0))

    return pl.pallas_call(
        _merge_body,
        grid=(t // tm,),
        in_specs=[row(yh.shape[1]), row(ym.shape[1]), row(2 * d), row(d), _resident(w_h.shape),
                  _resident(w_m.shape), _resident(w_o.shape)],
        out_specs=row(d),
        out_shape=jax.ShapeDtypeStruct((t, d), F32),
        compiler_params=_params(("parallel",)),
        name="merge",
    )(yh, ym, gates, h, w_h.astype(BF16), w_m.astype(BF16), w_o.astype(BF16))


def kernel(x, positions, ffn1_norm, ffn1_w_in, ffn1_w_out, mix_norm, w_in, hg_lb_table, hg_out_norm, w_hg_branch, mla_q_lora_norm, w_q_up, mla_kv_lora_norm, w_kv_up, q_head_norm, k_head_norm, w_mla_branch, w_merge, b_merge, w_out, ffn2_norm, ffn2_w_in, ffn2_w_out, final_norm):
    batch, seq, d = x.shape
    depth = ffn1_norm.shape[0]
    hg_width = hg_lb_table.shape[1]
    heads = w_q_up.shape[2] // MLA_QK
    assert hg_out_norm.shape[1] == HG_HEAD and w_hg_branch.shape[1] == hg_width
    assert q_head_norm.shape[1] == MLA_QK and w_kv_up.shape[2] == heads * (MLA_NOPE + MLA_V)
    assert seq % max(HG_BLOCK, ATT_TILE) == 0 and (batch * seq) % TOKEN_TILE == 0
    t = batch * seq
    pos = positions.reshape(t, 1)
    h = x.reshape(t, d)
    for l in range(depth):
        h1, u = _ffn(h, ffn1_norm[l], ffn1_w_in[l], ffn1_w_out[l], mix_norm[l], emit_h=True, norm_dtype=BF16)
        hq, lf, kk, hv, hg, gates, q, k, v = _proj(
            u, pos, w_in[l], w_merge[l], b_merge[l], hg_lb_table, mla_q_lora_norm[l], w_q_up[l],
            mla_kv_lora_norm[l], w_kv_up[l], q_head_norm[l], k_head_norm[l], layer=l, hg_width=hg_width,
            heads=heads)
        y_hg = _hgrn(hq, lf, kk, hv, hg, hg_out_norm[l], batch=batch, seq=seq)
        y_mla = _attention(q, k, v, batch=batch, seq=seq, heads=heads)
        h2 = _merge(y_hg, y_mla, gates, h1, w_hg_branch[l], w_mla_branch[l], w_out[l])
        (h,) = _ffn(h2, ffn2_norm[l], ffn2_w_in[l], ffn2_w_out[l], final_norm[l], emit_h=False, norm_dtype=F32)
    return h.reshape(batch, seq, d)
```

```python
import functools

import numpy as np
import jax
import jax.numpy as jnp
from jax import lax
from jax.experimental import pallas as pl
from jax.experimental.pallas import tpu as pltpu

F32 = jnp.float32
BF16 = jnp.bfloat16

EPS = 1e-6
CHUNK = 64
HG_HEAD = 128
MLA_NOPE = 128
MLA_ROPE = 64
MLA_V = 128
MLA_QK = MLA_NOPE + MLA_ROPE
MLA_HEAD_PAD = 256
ROPE_THETA = 10000.0
LOG2E = float(np.log2(np.e))

LANES = 128
TOKEN_TILE = 512
PROJ_TILE = 256
FF_CHUNK = 256
HG_BLOCK = 512
ATT_TILE = 512
ATT_SPLIT = 2
ATT_UNROLL = 4
ATT_FOLD = 32
ATT_ONES = 16
VMEM_LIMIT = 56 * 1024 * 1024


def _rms(x, gain):
    return x * lax.rsqrt(jnp.mean(x * x, axis=-1, keepdims=True) + EPS) * gain


def _silu(x):
    return x * jax.nn.sigmoid(x)


def _dot(a, b):
    return jnp.dot(a, b, preferred_element_type=F32)


def _dot_nt(a, b):
    return lax.dot_general(a, b, (((1,), (1,)), ((), ())), preferred_element_type=F32)


def _dot_tn(a, b):
    return lax.dot_general(a, b, (((0,), (0,)), ((), ())), preferred_element_type=F32)


def _resident(shape):
    zeros = (0,) * len(shape)
    return pl.BlockSpec(shape, lambda *_: zeros, pipeline_mode=pl.Buffered(1))


def _params(semantics):
    return pltpu.CompilerParams(dimension_semantics=semantics, vmem_limit_bytes=VMEM_LIMIT)


def _ffn_body(x_ref, gin_ref, win_ref, wout_ref, gout_ref, *rest, emit_h):
    if emit_h:
        h_ref, n_ref, a_ref = rest
    else:
        n_ref, a_ref = rest
    n_chunks, _, two_fc = win_ref.shape
    fc = two_fc // 2
    x = x_ref[...]
    xn = _rms(x, gin_ref[...]).astype(BF16)
    for c in range(n_chunks):
        hc = _dot(xn, win_ref[c])
        a_ref[:, c * fc:(c + 1) * fc] = (_silu(hc[:, :fc]) * hc[:, fc:]).astype(BF16)
    h = x + 0.5 * _dot(a_ref[...], wout_ref[...])
    if emit_h:
        h_ref[...] = h
    n_ref[...] = _rms(h, gout_ref[...]).astype(n_ref.dtype)


def _ffn(x, gin, w_in, w_out, gout, *, emit_h, norm_dtype):
    t, d = x.shape
    d_ff = w_out.shape[0]
    nc = d_ff // FF_CHUNK
    gate = w_in[:, :d_ff].reshape(d, nc, 1, FF_CHUNK)
    up = w_in[:, d_ff:].reshape(d, nc, 1, FF_CHUNK)
    win = jnp.concatenate([gate, up], axis=2).transpose(1, 0, 2, 3).reshape(nc, d, 2 * FF_CHUNK).astype(BF16)
    tm = TOKEN_TILE
    row = pl.BlockSpec((tm, d), lambda i: (i, 0))
    out_shape = [jax.ShapeDtypeStruct((t, d), norm_dtype)]
    out_specs = [row]
    if emit_h:
        out_shape = [jax.ShapeDtypeStruct((t, d), F32)] + out_shape
        out_specs = [row] + out_specs
    return pl.pallas_call(
        functools.partial(_ffn_body, emit_h=emit_h),
        grid=(t // tm,),
        in_specs=[row, _resident((1, d)), _resident(win.shape), _resident((d_ff, d)), _resident((1, d))],
        out_specs=out_specs,
        out_shape=out_shape,
        scratch_shapes=[pltpu.VMEM((tm, d_ff), BF16)],
        compiler_params=_params(("parallel",)),
        name="ffn1" if emit_h else "ffn2",
    )(x, gin.reshape(1, d), win, w_out.astype(BF16), gout.reshape(1, d))


def _proj_body(u_ref, pos_ref, w_ref, wm_ref, bm_ref, tab_ref, gq_ref, wq_ref, gkv_ref, wkv_ref, qr_gain_ref,
               kn_gain_ref, kr_gain_ref, invf_ref,
               hq_ref, lf_ref, kk_ref, hv_ref, hg_ref, gate_ref, q_ref, k_ref, v_ref, *, layer, heads):
    w = hq_ref.shape[1]
    tm, d = u_ref.shape
    q_lora, kv_lora = wq_ref.shape[0], wkv_ref.shape[0]
    u = u_ref[...]

    c = _dot(u, w_ref[:, 4 * w:])
    qraw = _dot(_rms(c[:, :q_lora], gq_ref[...]).astype(BF16), wq_ref[...])
    kvraw = _dot(_rms(c[:, q_lora:q_lora + kv_lora], gkv_ref[...]).astype(BF16), wkv_ref[...])
    kpe = c[:, q_lora + kv_lora:]

    hq_ref[...] = _silu(_dot(u, w_ref[:, 0:w])).astype(hq_ref.dtype)
    tab = tab_ref[...]
    e = jnp.exp(tab - jnp.max(tab, axis=0, keepdims=True))
    lb = jnp.sum(e[:layer + 1], axis=0, keepdims=True) / jnp.sum(e, axis=0, keepdims=True)
    z = _dot(u, w_ref[:, w:2 * w])
    ez = jnp.exp(-jnp.abs(z))
    r = 1.0 / (1.0 + ez)
    a = ez * r
    pos = z >= 0
    lf_ref[...] = jnp.log(lb + (1.0 - lb) * jnp.where(pos, r, a)) * LOG2E
    kk_ref[...] = ((1.0 - lb) * jnp.where(pos, a, r)).astype(BF16)
    hv_ref[...] = _dot(u, w_ref[:, 2 * w:3 * w]).astype(BF16)
    hg_ref[...] = _silu(_dot(u, w_ref[:, 3 * w:4 * w])).astype(BF16)
    for j in range(2):
        gm = _dot(u, wm_ref[:, j * d:(j + 1) * d]) + bm_ref[:, j * d:(j + 1) * d]
        gate_ref[:, j * d:(j + 1) * d] = jax.nn.sigmoid(gm).astype(BF16)

    ang = pos_ref[...].astype(F32) * invf_ref[...]
    first = lax.broadcasted_iota(jnp.int32, (tm, LANES), 1) < MLA_ROPE
    cs = jnp.where(first, jnp.cos(ang), jnp.sin(ang))

    def rope(block, gain_cs):
        t = block * gain_cs
        return t + pltpu.roll(t, MLA_ROPE, 1)

    def sumsq(x):
        return jnp.sum(x * x, axis=-1, keepdims=True)

    k_rope = rope(kpe, kr_gain_ref[...] * cs)
    k_pe_ss = 0.5 * sumsq(kpe)
    q_gain_cs = qr_gain_ref[...] * cs
    scale = MLA_QK ** -0.5 * LOG2E
    nope_w = heads * MLA_NOPE
    for h in range(heads):
        lo, hi = h * MLA_NOPE, (h + 1) * MLA_NOPE
        qn = qraw[:, lo:hi]
        qr = qraw[:, nope_w + lo:nope_w + hi]
        sq = lax.rsqrt((sumsq(qn) + 0.5 * sumsq(qr)) / MLA_QK + EPS) * scale
        o = h * MLA_HEAD_PAD
        q_head = jnp.concatenate([qn * sq, rope(qr, q_gain_cs) * sq], axis=1)
        q_ref[0, h, 0] = q_head.T.astype(BF16)
        kn = kvraw[:, lo:hi]
        sk = lax.rsqrt((sumsq(kn) + k_pe_ss) / MLA_QK + EPS)
        k_ref[:, o:o + MLA_NOPE] = (kn * sk * kn_gain_ref[...]).astype(BF16)
        k_ref[:, o + MLA_NOPE:o + MLA_HEAD_PAD] = jnp.where(first, k_rope * sk, 0.0).astype(BF16)
    v_ref[...] = kvraw[:, nope_w:].astype(BF16)


def _rot_cols(w):
    half = w.shape[-1] // 2
    return jnp.concatenate([-w[..., half:], w[..., :half]], axis=-1)


def _rot_gain(g):
    half = g.shape[-1] // 2
    return jnp.concatenate([g[..., half:], g[..., :half]], axis=-1)


def _proj(u, pos, w_in, w_merge, b_merge, lb_table, gq, w_q_up, gkv, w_kv_up, q_head_gain, k_head_gain,
          *, layer, hg_width, heads, seq):
    t, d = u.shape
    q_lora, kv_lora = w_q_up.shape[0], w_kv_up.shape[0]
    w_in_ext = jnp.concatenate([w_in, _rot_cols(w_in[:, -MLA_ROPE:])], axis=1).astype(BF16)
    cols = w_in_ext.shape[1]
    assert cols == 4 * hg_width + q_lora + kv_lora + 2 * MLA_ROPE
    wq = w_q_up.reshape(q_lora, heads, MLA_QK)
    wq_rope = wq[:, :, MLA_NOPE:]
    wq_ext = jnp.concatenate(
        [wq[:, :, :MLA_NOPE].reshape(q_lora, heads * MLA_NOPE),
         jnp.concatenate([wq_rope, _rot_cols(wq_rope)], axis=-1).reshape(q_lora, heads * 2 * MLA_ROPE)],
        axis=1).astype(BF16)
    wkv = w_kv_up.reshape(kv_lora, heads, MLA_NOPE + MLA_V)
    wkv_ext = jnp.concatenate([wkv[:, :, :MLA_NOPE].reshape(kv_lora, heads * MLA_NOPE),
                               wkv[:, :, MLA_NOPE:].reshape(kv_lora, heads * MLA_V)], axis=1).astype(BF16)

    def rope_gain(g):
        return jnp.concatenate([g[MLA_NOPE:], _rot_gain(g[MLA_NOPE:])]).reshape(1, 2 * MLA_ROPE)

    inv_freq = ROPE_THETA ** (-jnp.arange(0, MLA_ROPE, 2, dtype=F32) / MLA_ROPE)
    invf = jnp.tile(inv_freq, 4).reshape(1, LANES)
    nope_gain = (q_head_gain[:MLA_NOPE] * k_head_gain[:MLA_NOPE]).reshape(1, MLA_NOPE)
    tm = PROJ_TILE

    def row(n):
        return pl.BlockSpec((tm, n), lambda i: (i, 0))

    qk_w = heads * MLA_HEAD_PAD
    per_seq = seq // tm
    per_att = ATT_TILE // tm
    qt_spec = pl.BlockSpec((1, heads, 1, MLA_HEAD_PAD, tm),
                           lambda i: (i // per_seq, 0, (i % per_seq) // per_att, 0, i % per_att))
    return pl.pallas_call(
        functools.partial(_proj_body, layer=layer, heads=heads),
        grid=(t // tm,),
        in_specs=[row(d), row(1), _resident((d, cols)), _resident((d, 2 * d)), _resident((1, 2 * d)),
                  _resident(lb_table.shape), _resident((1, q_lora)), _resident(wq_ext.shape),
                  _resident((1, kv_lora)), _resident(wkv_ext.shape), _resident((1, LANES)),
                  _resident((1, MLA_NOPE)), _resident((1, LANES)), _resident((1, LANES))],
        out_specs=[row(hg_width), row(hg_width), row(hg_width), row(hg_width), row(hg_width), row(2 * d),
                   qt_spec, row(qk_w), row(heads * MLA_V)],
        out_shape=[jax.ShapeDtypeStruct((t, hg_width), BF16),
                   jax.ShapeDtypeStruct((t, hg_width), F32),
                   jax.ShapeDtypeStruct((t, hg_width), BF16),
                   jax.ShapeDtypeStruct((t, hg_width), BF16),
                   jax.ShapeDtypeStruct((t, hg_width), BF16),
                   jax.ShapeDtypeStruct((t, 2 * d), BF16),
                   jax.ShapeDtypeStruct((t // seq, heads, seq // ATT_TILE, MLA_HEAD_PAD, ATT_TILE), BF16),
                   jax.ShapeDtypeStruct((t, qk_w), BF16),
                   jax.ShapeDtypeStruct((t, heads * MLA_V), BF16)],
        compiler_params=_params(("parallel",)),
        name="proj",
    )(u, pos, w_in_ext, w_merge.astype(BF16), b_merge.reshape(1, 2 * d), lb_table, gq.reshape(1, q_lora), wq_ext,
      gkv.reshape(1, kv_lora), wkv_ext, rope_gain(q_head_gain), nope_gain, rope_gain(k_head_gain), invf)


def _split3(x):
    hi = x.astype(BF16)
    r = x - hi.astype(F32)
    mid = r.astype(BF16)
    lo = (r - mid.astype(F32)).astype(BF16)
    return hi, mid, lo


def _hgrn_body(q_ref, lf_ref, kk_ref, v_ref, g_ref, gain_ref, tri_ref, y_ref, state_ref):
    @pl.when(pl.program_id(2) == 0)
    def _():
        state_ref[...] = jnp.zeros_like(state_ref)

    n = CHUNK
    tb = q_ref.shape[1]
    chunks = [slice(c * n, (c + 1) * n) for c in range(tb // n)]
    trow = lax.broadcasted_iota(jnp.int32, (n, n), 0)
    tcol = lax.broadcasted_iota(jnp.int32, (n, n), 1)
    tri = tri_ref[...]
    halves = [n >> i for i in range(1, n.bit_length())]
    pairs = {half: (trow // (2 * half) == tcol // (2 * half)) & (trow % (2 * half) >= half)
             & (tcol % (2 * half) < half) for half in halves}

    def ref_rows(x, size, row):
        blocks = x.reshape(tb // size, size, HG_HEAD)[:, row:row + 1, :]
        return jnp.broadcast_to(blocks, (tb // size, size, HG_HEAD)).reshape(tb, HG_HEAD)

    q = q_ref[0].astype(F32)
    lf = lf_ref[0]
    k = kk_ref[0].astype(F32)
    v = v_ref[0]
    lf3 = lf.reshape(tb // 8, 8, HG_HEAD)
    r8 = lax.broadcasted_iota(jnp.int32, (1, 8, HG_HEAD), 1)
    split = jnp.concatenate(_split3(lf), axis=1)
    cums = []
    for sl in chunks:
        r = _dot(tri, split[sl])
        cums.append(r[:, :HG_HEAD] + r[:, HG_HEAD:2 * HG_HEAD] + r[:, 2 * HG_HEAD:])
    cum = jnp.concatenate(cums, axis=0)
    diag = jnp.sum(q * k, axis=-1, keepdims=True)
    scores = [jnp.where(trow == tcol, diag[sl], 0.0) for sl in chunks]
    for half in halves:
        size = 2 * half
        if half >= 4:
            diff = pltpu.bitcast(cum - ref_rows(cum, size, half - 1), jnp.uint32)
            dec = pltpu.bitcast(diff | jnp.uint32(0x80000000), F32)
        elif half == 2:
            dec = jnp.where(r8 % 4 == 3, lf3 + pltpu.roll(lf3, 1, 1),
                            jnp.where(r8 % 4 == 2, lf3,
                                      jnp.where(r8 % 4 == 0, pltpu.roll(lf3, 7, 1), 0.0))).reshape(tb, HG_HEAD)
        else:
            dec = jnp.where(r8 % 2 == 1, lf3, 0.0).reshape(tb, HG_HEAD)
        e = jnp.exp2(dec)
        qe = (q * e).astype(BF16)
        ke = (k * e).astype(BF16)
        for c, sl in enumerate(chunks):
            scores[c] = scores[c] + jnp.where(pairs[half], _dot_nt(qe[sl], ke[sl]), 0.0)
    last = ref_rows(cum, n, n - 1)
    qdec = (q * jnp.exp2(cum)).astype(BF16)
    kdec = (k * jnp.exp2(last - cum)).astype(BF16)
    updates = [_dot_tn(v[sl], kdec[sl]) for sl in chunks]
    state = state_ref[...]
    outs = []
    for c, sl in enumerate(chunks):
        outs.append(_dot(scores[c].astype(BF16), v[sl]) + _dot_nt(qdec[sl], state.astype(BF16)))
        state = state * jnp.exp2(cum[sl][n - 1:n, :]) + updates[c]
    state_ref[...] = state
    o = jnp.concatenate(outs, axis=0)
    y_ref[0] = (_rms(o, gain_ref[...]) * g_ref[0].astype(F32)).astype(y_ref.dtype)


def _hgrn(hq, lf, kk, v, g, out_gain, *, batch, seq):
    width = hq.shape[1]
    heads = width // HG_HEAD
    tb = HG_BLOCK
    tri = jnp.asarray(np.tril(np.ones((CHUNK, CHUNK), np.float32)), BF16)

    def r3(a):
        return a.reshape(batch, seq, width)

    blk = pl.BlockSpec((1, tb, HG_HEAD), lambda b, h, j: (b, j, h))
    y = pl.pallas_call(
        _hgrn_body,
        grid=(batch, heads, seq // tb),
        in_specs=[blk, blk, blk, blk, blk, _resident((1, HG_HEAD)), _resident((CHUNK, CHUNK))],
        out_specs=blk,
        out_shape=jax.ShapeDtypeStruct((batch, seq, width), BF16),
        scratch_shapes=[pltpu.VMEM((HG_HEAD, HG_HEAD), F32)],
        compiler_params=_params(("parallel", "parallel", "arbitrary")),
        name="hgrn",
    )(r3(hq), r3(lf), r3(kk), r3(v), r3(g), out_gain.reshape(1, HG_HEAD), tri)
    return y.reshape(batch * seq, width)


def _attn_body(q_ref, k_ref, v_ref, o_ref, vt_ref, s_ref, mx_ref, m_ref, acc_ref):
    seq = k_ref.shape[1]
    tk = ATT_TILE
    tqs = ATT_TILE // ATT_SPLIT
    n_tiles = seq // tk

    def transpose_v(j, _):
        vt_ref[j, :MLA_V] = v_ref[0, pl.ds(pl.multiple_of(j * tk, tk), tk), :].astype(F32).T.astype(BF16)
        vt_ref[j, MLA_V:] = jnp.ones((ATT_ONES, tk), BF16)
        return 0

    lax.fori_loop(0, n_tiles, transpose_v, 0)

    key_chunk = lax.broadcasted_iota(jnp.int32, (tk, tqs), 0) // CHUNK
    qry_chunk = lax.broadcasted_iota(jnp.int32, (tk, tqs), 1) // CHUNK
    allowed = [key_chunk <= qry_chunk + (sub * tqs) // CHUNK for sub in range(ATT_SPLIT)]

    def fold_rows(x, op):
        rows, cols = x.shape
        x = x.reshape(rows // ATT_FOLD, ATT_FOLD, cols)
        out = x[0]
        for i in range(1, rows // ATT_FOLD):
            out = op(out, x[i])
        return out

    def aligned(x, m):
        return x if isinstance(x, int) else pl.multiple_of(x, m)

    def scores(qt, kj, buf):
        k_tile = k_ref[0, pl.ds(aligned(kj * tk, tk), tk), :]
        s = _dot(k_tile, q_ref[0, 0, qt])
        s_ref[buf] = s
        mx_ref[buf] = jnp.max(fold_rows(s, jnp.maximum), axis=0, keepdims=True)

    def start():
        m_ref[...] = jnp.full(m_ref.shape, -jnp.inf, F32)
        acc_ref[...] = jnp.zeros(acc_ref.shape, F32)

    def finish(base):
        for sub in range(ATT_SPLIT):
            acc = acc_ref[sub]
            o_ref[0, pl.ds(aligned(base + sub * tqs, tqs), tqs), :] = (
                (acc[:MLA_V] / acc[MLA_V:MLA_V + 1]).T.astype(o_ref.dtype))

    def absorb(kj, buf, masked):
        vt_tile = vt_ref[kj]
        for sub in range(ATT_SPLIT):
            cols = slice(sub * tqs, (sub + 1) * tqs)
            ss = s_ref[buf, :, cols]
            if masked:
                ss = jnp.where(allowed[sub], ss, -jnp.inf)
                tile_max = jnp.max(fold_rows(ss, jnp.maximum), axis=0, keepdims=True)
            else:
                tile_max = mx_ref[buf, :, cols]
            m = m_ref[sub]
            m_new = jnp.maximum(m, tile_max)
            alpha = jnp.exp2(m - m_new)
            p = jnp.exp2(ss - m_new)
            m_ref[sub] = m_new
            acc_ref[sub] = alpha * acc_ref[sub] + _dot(vt_tile, p.astype(BF16))

    def run(qi, next_q, first, count, last_masked):
        for j in range(count):
            if last_masked and j + 1 == count:
                scores(next_q, 0, 2)
            else:
                scores(qi, first + j + 1, j % 2)
            absorb(first + j, (j + 1) % 2, last_masked and j + 1 == count)

    def q_step(qi, _):
        next_q = jnp.minimum(qi + 1, n_tiles - 1)
        start()
        scores(qi, 1, 1)
        absorb(0, 2, False)
        full = qi - 1

        def group(g, _):
            run(qi, next_q, 1 + g * ATT_UNROLL, ATT_UNROLL, False)
            return 0

        lax.fori_loop(0, full // ATT_UNROLL, group, 0)
        tail_first = 1 + (full // ATT_UNROLL) * ATT_UNROLL
        for rem in range(ATT_UNROLL):
            @pl.when(full % ATT_UNROLL == rem)
            def _(rem=rem):
                run(qi, next_q, tail_first, rem + 1, True)

        finish(pl.multiple_of(qi * tk, tk))
        return 0

    start()
    scores(0, 0, 0)
    scores(min(1, n_tiles - 1), 0, 2)
    absorb(0, 0, True)
    finish(0)
    lax.fori_loop(1, n_tiles, q_step, 0)


def _attention(q, k, v, *, batch, seq, heads):
    qt = pl.BlockSpec((1, 1, seq // ATT_TILE, MLA_HEAD_PAD, ATT_TILE), lambda b, h: (b, h, 0, 0, 0))
    kk = pl.BlockSpec((1, seq, MLA_HEAD_PAD), lambda b, h: (b, 0, h))
    vo = pl.BlockSpec((1, seq, MLA_V), lambda b, h: (b, 0, h))
    o = pl.pallas_call(
        _attn_body,
        grid=(batch, heads),
        in_specs=[qt, kk, vo],
        out_specs=vo,
        out_shape=jax.ShapeDtypeStruct((batch, seq, heads * MLA_V), BF16),
        scratch_shapes=[pltpu.VMEM((seq // ATT_TILE, MLA_V + ATT_ONES, ATT_TILE), BF16),
                        pltpu.VMEM((3, ATT_TILE, ATT_TILE), F32),
                        pltpu.VMEM((3, 1, ATT_TILE), F32),
                        pltpu.VMEM((ATT_SPLIT, 1, ATT_TILE // ATT_SPLIT), F32),
                        pltpu.VMEM((ATT_SPLIT, MLA_V + ATT_ONES, ATT_TILE // ATT_SPLIT), F32)],
        compiler_params=_params(("parallel", "parallel")),
        name="attn",
    )(q, k.reshape(batch, seq, -1), v.reshape(batch, seq, -1))
    return o.reshape(batch * seq, heads * MLA_V)


def _merge_body(yh_ref, ym_ref, gate_ref, h_ref, wh_ref, wm_ref, wo_ref, o_ref):
    d = h_ref.shape[1]
    mix = (gate_ref[:, :d].astype(F32) * _dot(yh_ref[...], wh_ref[...])
           + gate_ref[:, d:].astype(F32) * _dot(ym_ref[...], wm_ref[...]))
    o_ref[...] = h_ref[...] + _dot(mix.astype(BF16), wo_ref[...])


def _merge(yh, ym, gates, h, w_h, w_m, w_o):
    t, d = h.shape
    tm = TOKEN_TILE

    def row(n):
        return pl.BlockSpec((tm, n), lambda i: (i, 0))

    return pl.pallas_call(
        _merge_body,
        grid=(t // tm,),
        in_specs=[row(yh.shape[1]), row(ym.shape[1]), row(2 * d), row(d), _resident(w_h.shape),
                  _resident(w_m.shape), _resident(w_o.shape)],
        out_specs=row(d),
        out_shape=jax.ShapeDtypeStruct((t, d), F32),
        compiler_params=_params(("parallel",)),
        name="merge",
    )(yh, ym, gates, h, w_h.astype(BF16), w_m.astype(BF16), w_o.astype(BF16))


def kernel(x, positions, ffn1_norm, ffn1_w_in, ffn1_w_out, mix_norm, w_in, hg_lb_table, hg_out_norm, w_hg_branch, mla_q_lora_norm, w_q_up, mla_kv_lora_norm, w_kv_up, q_head_norm, k_head_norm, w_mla_branch, w_merge, b_merge, w_out, ffn2_norm, ffn2_w_in, ffn2_w_out, final_norm):
    batch, seq, d = x.shape
    depth = ffn1_norm.shape[0]
    hg_width = hg_lb_table.shape[1]
    heads = w_q_up.shape[2] // MLA_QK
    assert hg_out_norm.shape[1] == HG_HEAD and w_hg_branch.shape[1] == hg_width
    assert q_head_norm.shape[1] == MLA_QK and w_kv_up.shape[2] == heads * (MLA_NOPE + MLA_V)
    assert seq % max(HG_BLOCK, ATT_TILE) == 0 and (batch * seq) % TOKEN_TILE == 0
    t = batch * seq
    pos = positions.reshape(t, 1)
    h = x.reshape(t, d)
    for l in range(depth):
        h1, u = _ffn(h, ffn1_norm[l], ffn1_w_in[l], ffn1_w_out[l], mix_norm[l], emit_h=True, norm_dtype=BF16)
        hq, lf, kk, hv, hg, gates, q, k, v = _proj(
            u, pos, w_in[l], w_merge[l], b_merge[l], hg_lb_table, mla_q_lora_norm[l], w_q_up[l],
            mla_kv_lora_norm[l], w_kv_up[l], q_head_norm[l], k_head_norm[l], layer=l, hg_width=hg_width,
            heads=heads, seq=seq)
        y_hg = _hgrn(hq, lf, kk, hv, hg, hg_out_norm[l], batch=batch, seq=seq)
        y_mla = _attention(q, k, v, batch=batch, seq=seq, heads=heads)
        h2 = _merge(y_hg, y_mla, gates, h1, w_hg_branch[l], w_mla_branch[l], w_out[l])
        (h,) = _ffn(h2, ffn2_norm[l], ffn2_w_in[l], ffn2_w_out[l], final_norm[l], emit_h=False, norm_dtype=F32)
    return h.reshape(batch, seq, d)
```

```python
import functools

import numpy as np
import jax
import jax.numpy as jnp
from jax import lax
from jax.experimental import pallas as pl
from jax.experimental.pallas import tpu as pltpu

F32 = jnp.float32
BF16 = jnp.bfloat16

EPS = 1e-6
CHUNK = 64
HG_HEAD = 128
MLA_NOPE = 128
MLA_ROPE = 64
MLA_V = 128
MLA_QK = MLA_NOPE + MLA_ROPE
MLA_HEAD_PAD = 256
ROPE_THETA = 10000.0
LOG2E = float(np.log2(np.e))

LANES = 128
TOKEN_TILE = 512
PROJ_TILE = 256
FF_CHUNK = 256
HG_BLOCK = 512
ATT_TILE = 512
ATT_SPLIT = 2
ATT_UNROLL = 4
ATT_FOLD = 32
ATT_ONES = 16
VMEM_LIMIT = 56 * 1024 * 1024


def _rms(x, gain):
    return x * lax.rsqrt(jnp.mean(x * x, axis=-1, keepdims=True) + EPS) * gain


def _silu(x):
    return x * jax.nn.sigmoid(x)


def _dot(a, b):
    return jnp.dot(a, b, preferred_element_type=F32)


def _dot_nt(a, b):
    return lax.dot_general(a, b, (((1,), (1,)), ((), ())), preferred_element_type=F32)


def _dot_tn(a, b):
    return lax.dot_general(a, b, (((0,), (0,)), ((), ())), preferred_element_type=F32)


def _resident(shape):
    zeros = (0,) * len(shape)
    return pl.BlockSpec(shape, lambda *_: zeros, pipeline_mode=pl.Buffered(1))


def _params(semantics):
    return pltpu.CompilerParams(dimension_semantics=semantics, vmem_limit_bytes=VMEM_LIMIT)


def _ffn_body(x_ref, gin_ref, win_ref, wout_ref, gout_ref, *rest, emit_h):
    if emit_h:
        h_ref, n_ref, a_ref = rest
    else:
        n_ref, a_ref = rest
    n_chunks, _, two_fc = win_ref.shape
    fc = two_fc // 2
    x = x_ref[...]
    xn = _rms(x, gin_ref[...]).astype(BF16)
    for c in range(n_chunks):
        hc = _dot(xn, win_ref[c])
        a_ref[:, c * fc:(c + 1) * fc] = (_silu(hc[:, :fc]) * hc[:, fc:]).astype(BF16)
    h = x + 0.5 * _dot(a_ref[...], wout_ref[...])
    if emit_h:
        h_ref[...] = h
    n_ref[...] = _rms(h, gout_ref[...]).astype(n_ref.dtype)


def _ffn(x, gin, w_in, w_out, gout, *, emit_h, norm_dtype):
    t, d = x.shape
    d_ff = w_out.shape[0]
    nc = d_ff // FF_CHUNK
    gate = w_in[:, :d_ff].reshape(d, nc, 1, FF_CHUNK)
    up = w_in[:, d_ff:].reshape(d, nc, 1, FF_CHUNK)
    win = jnp.concatenate([gate, up], axis=2).transpose(1, 0, 2, 3).reshape(nc, d, 2 * FF_CHUNK).astype(BF16)
    tm = TOKEN_TILE
    row = pl.BlockSpec((tm, d), lambda i: (i, 0))
    out_shape = [jax.ShapeDtypeStruct((t, d), norm_dtype)]
    out_specs = [row]
    if emit_h:
        out_shape = [jax.ShapeDtypeStruct((t, d), F32)] + out_shape
        out_specs = [row] + out_specs
    return pl.pallas_call(
        functools.partial(_ffn_body, emit_h=emit_h),
        grid=(t // tm,),
        in_specs=[row, _resident((1, d)), _resident(win.shape), _resident((d_ff, d)), _resident((1, d))],
        out_specs=out_specs,
        out_shape=out_shape,
        scratch_shapes=[pltpu.VMEM((tm, d_ff), BF16)],
        compiler_params=_params(("parallel",)),
        name="ffn1" if emit_h else "ffn2",
    )(x, gin.reshape(1, d), win, w_out.astype(BF16), gout.reshape(1, d))


def _proj_body(u_ref, pos_ref, w_ref, wm_ref, bm_ref, tab_ref, gq_ref, wq_ref, gkv_ref, wkv_ref, qr_gain_ref,
               kn_gain_ref, kr_gain_ref, invf_ref,
               hq_ref, lf_ref, kk_ref, hv_ref, hg_ref, gate_ref, q_ref, k_ref, v_ref, *, layer, heads):
    w = hq_ref.shape[1]
    tm, d = u_ref.shape
    q_lora, kv_lora = wq_ref.shape[0], wkv_ref.shape[0]
    u = u_ref[...]

    c = _dot(u, w_ref[:, 4 * w:])
    qraw = _dot(_rms(c[:, :q_lora], gq_ref[...]).astype(BF16), wq_ref[...])
    kvraw = _dot(_rms(c[:, q_lora:q_lora + kv_lora], gkv_ref[...]).astype(BF16), wkv_ref[...])
    kpe = c[:, q_lora + kv_lora:]

    hq_ref[...] = _silu(_dot(u, w_ref[:, 0:w])).astype(hq_ref.dtype)
    tab = tab_ref[...]
    e = jnp.exp(tab - jnp.max(tab, axis=0, keepdims=True))
    lb = jnp.sum(e[:layer + 1], axis=0, keepdims=True) / jnp.sum(e, axis=0, keepdims=True)
    z = _dot(u, w_ref[:, w:2 * w])
    ez = jnp.exp(-jnp.abs(z))
    r = 1.0 / (1.0 + ez)
    a = ez * r
    pos = z >= 0
    lf_ref[...] = jnp.log(lb + (1.0 - lb) * jnp.where(pos, r, a)) * LOG2E
    kk_ref[...] = ((1.0 - lb) * jnp.where(pos, a, r)).astype(BF16)
    hv_ref[...] = _dot(u, w_ref[:, 2 * w:3 * w]).astype(BF16)
    hg_ref[...] = _silu(_dot(u, w_ref[:, 3 * w:4 * w])).astype(BF16)
    for j in range(2):
        gm = _dot(u, wm_ref[:, j * d:(j + 1) * d]) + bm_ref[:, j * d:(j + 1) * d]
        gate_ref[:, j * d:(j + 1) * d] = jax.nn.sigmoid(gm).astype(BF16)

    ang = pos_ref[...].astype(F32) * invf_ref[...]
    first = lax.broadcasted_iota(jnp.int32, (tm, LANES), 1) < MLA_ROPE
    cs = jnp.where(first, jnp.cos(ang), jnp.sin(ang))

    def rope(block, gain_cs):
        t = block * gain_cs
        return t + pltpu.roll(t, MLA_ROPE, 1)

    def sumsq(x):
        return jnp.sum(x * x, axis=-1, keepdims=True)

    k_rope = rope(kpe, kr_gain_ref[...] * cs)
    k_pe_ss = 0.5 * sumsq(kpe)
    q_gain_cs = qr_gain_ref[...] * cs
    scale = MLA_QK ** -0.5 * LOG2E
    nope_w = heads * MLA_NOPE
    for h in range(heads):
        lo, hi = h * MLA_NOPE, (h + 1) * MLA_NOPE
        qn = qraw[:, lo:hi]
        qr = qraw[:, nope_w + lo:nope_w + hi]
        sq = lax.rsqrt((sumsq(qn) + 0.5 * sumsq(qr)) / MLA_QK + EPS) * scale
        o = h * MLA_HEAD_PAD
        q_head = jnp.concatenate([qn * sq, rope(qr, q_gain_cs) * sq], axis=1)
        q_ref[0, h, 0] = q_head.T.astype(BF16)
        kn = kvraw[:, lo:hi]
        sk = lax.rsqrt((sumsq(kn) + k_pe_ss) / MLA_QK + EPS)
        k_ref[0, h, :, :MLA_NOPE] = (kn * sk * kn_gain_ref[...]).astype(BF16)
        k_ref[0, h, :, MLA_NOPE:] = jnp.where(first, k_rope * sk, 0.0).astype(BF16)
        v_ref[0, h] = kvraw[:, nope_w + h * MLA_V:nope_w + (h + 1) * MLA_V].astype(BF16)


def _rot_cols(w):
    half = w.shape[-1] // 2
    return jnp.concatenate([-w[..., half:], w[..., :half]], axis=-1)


def _rot_gain(g):
    half = g.shape[-1] // 2
    return jnp.concatenate([g[..., half:], g[..., :half]], axis=-1)


def _proj(u, pos, w_in, w_merge, b_merge, lb_table, gq, w_q_up, gkv, w_kv_up, q_head_gain, k_head_gain,
          *, layer, hg_width, heads, seq):
    t, d = u.shape
    q_lora, kv_lora = w_q_up.shape[0], w_kv_up.shape[0]
    w_in_ext = jnp.concatenate([w_in, _rot_cols(w_in[:, -MLA_ROPE:])], axis=1).astype(BF16)
    cols = w_in_ext.shape[1]
    assert cols == 4 * hg_width + q_lora + kv_lora + 2 * MLA_ROPE
    wq = w_q_up.reshape(q_lora, heads, MLA_QK)
    wq_rope = wq[:, :, MLA_NOPE:]
    wq_ext = jnp.concatenate(
        [wq[:, :, :MLA_NOPE].reshape(q_lora, heads * MLA_NOPE),
         jnp.concatenate([wq_rope, _rot_cols(wq_rope)], axis=-1).reshape(q_lora, heads * 2 * MLA_ROPE)],
        axis=1).astype(BF16)
    wkv = w_kv_up.reshape(kv_lora, heads, MLA_NOPE + MLA_V)
    wkv_ext = jnp.concatenate([wkv[:, :, :MLA_NOPE].reshape(kv_lora, heads * MLA_NOPE),
                               wkv[:, :, MLA_NOPE:].reshape(kv_lora, heads * MLA_V)], axis=1).astype(BF16)

    def rope_gain(g):
        return jnp.concatenate([g[MLA_NOPE:], _rot_gain(g[MLA_NOPE:])]).reshape(1, 2 * MLA_ROPE)

    inv_freq = ROPE_THETA ** (-jnp.arange(0, MLA_ROPE, 2, dtype=F32) / MLA_ROPE)
    invf = jnp.tile(inv_freq, 4).reshape(1, LANES)
    nope_gain = (q_head_gain[:MLA_NOPE] * k_head_gain[:MLA_NOPE]).reshape(1, MLA_NOPE)
    tm = PROJ_TILE

    def row(n):
        return pl.BlockSpec((tm, n), lambda i: (i, 0))

    def head_major(n):
        return pl.BlockSpec((1, heads, tm, n), lambda i: (i // per_seq, 0, i % per_seq, 0))

    per_seq = seq // tm
    per_att = ATT_TILE // tm
    qt_spec = pl.BlockSpec((1, heads, 1, MLA_HEAD_PAD, tm),
                           lambda i: (i // per_seq, 0, (i % per_seq) // per_att, 0, i % per_att))
    return pl.pallas_call(
        functools.partial(_proj_body, layer=layer, heads=heads),
        grid=(t // tm,),
        in_specs=[row(d), row(1), _resident((d, cols)), _resident((d, 2 * d)), _resident((1, 2 * d)),
                  _resident(lb_table.shape), _resident((1, q_lora)), _resident(wq_ext.shape),
                  _resident((1, kv_lora)), _resident(wkv_ext.shape), _resident((1, LANES)),
                  _resident((1, MLA_NOPE)), _resident((1, LANES)), _resident((1, LANES))],
        out_specs=[row(hg_width), row(hg_width), row(hg_width), row(hg_width), row(hg_width), row(2 * d),
                   qt_spec, head_major(MLA_HEAD_PAD), head_major(MLA_V)],
        out_shape=[jax.ShapeDtypeStruct((t, hg_width), BF16),
                   jax.ShapeDtypeStruct((t, hg_width), F32),
                   jax.ShapeDtypeStruct((t, hg_width), BF16),
                   jax.ShapeDtypeStruct((t, hg_width), BF16),
                   jax.ShapeDtypeStruct((t, hg_width), BF16),
                   jax.ShapeDtypeStruct((t, 2 * d), BF16),
                   jax.ShapeDtypeStruct((t // seq, heads, seq // ATT_TILE, MLA_HEAD_PAD, ATT_TILE), BF16),
                   jax.ShapeDtypeStruct((t // seq, heads, seq, MLA_HEAD_PAD), BF16),
                   jax.ShapeDtypeStruct((t // seq, heads, seq, MLA_V), BF16)],
        compiler_params=_params(("parallel",)),
        name="proj",
    )(u, pos, w_in_ext, w_merge.astype(BF16), b_merge.reshape(1, 2 * d), lb_table, gq.reshape(1, q_lora), wq_ext,
      gkv.reshape(1, kv_lora), wkv_ext, rope_gain(q_head_gain), nope_gain, rope_gain(k_head_gain), invf)


def _split3(x):
    hi = x.astype(BF16)
    r = x - hi.astype(F32)
    mid = r.astype(BF16)
    lo = (r - mid.astype(F32)).astype(BF16)
    return hi, mid, lo


def _hgrn_body(q_ref, lf_ref, kk_ref, v_ref, g_ref, gain_ref, tri_ref, y_ref, state_ref):
    @pl.when(pl.program_id(2) == 0)
    def _():
        state_ref[...] = jnp.zeros_like(state_ref)

    n = CHUNK
    tb = q_ref.shape[1]
    chunks = [slice(c * n, (c + 1) * n) for c in range(tb // n)]
    trow = lax.broadcasted_iota(jnp.int32, (n, n), 0)
    tcol = lax.broadcasted_iota(jnp.int32, (n, n), 1)
    tri = tri_ref[...]
    halves = [n >> i for i in range(1, n.bit_length())]
    pairs = {half: (trow // (2 * half) == tcol // (2 * half)) & (trow % (2 * half) >= half)
             & (tcol % (2 * half) < half) for half in halves}

    def ref_rows(x, size, row):
        blocks = x.reshape(tb // size, size, HG_HEAD)[:, row:row + 1, :]
        return jnp.broadcast_to(blocks, (tb // size, size, HG_HEAD)).reshape(tb, HG_HEAD)

    q = q_ref[0].astype(F32)
    lf = lf_ref[0]
    k = kk_ref[0].astype(F32)
    v = v_ref[0]
    lf3 = lf.reshape(tb // 8, 8, HG_HEAD)
    r8 = lax.broadcasted_iota(jnp.int32, (1, 8, HG_HEAD), 1)
    split = jnp.concatenate(_split3(lf), axis=1)
    cums = []
    for sl in chunks:
        r = _dot(tri, split[sl])
        cums.append(r[:, :HG_HEAD] + r[:, HG_HEAD:2 * HG_HEAD] + r[:, 2 * HG_HEAD:])
    cum = jnp.concatenate(cums, axis=0)
    diag = jnp.sum(q * k, axis=-1, keepdims=True)
    scores = [jnp.where(trow == tcol, diag[sl], 0.0) for sl in chunks]
    for half in halves:
        size = 2 * half
        if half >= 4:
            diff = pltpu.bitcast(cum - ref_rows(cum, size, half - 1), jnp.uint32)
            dec = pltpu.bitcast(diff | jnp.uint32(0x80000000), F32)
        elif half == 2:
            dec = jnp.where(r8 % 4 == 3, lf3 + pltpu.roll(lf3, 1, 1),
                            jnp.where(r8 % 4 == 2, lf3,
                                      jnp.where(r8 % 4 == 0, pltpu.roll(lf3, 7, 1), 0.0))).reshape(tb, HG_HEAD)
        else:
            dec = jnp.where(r8 % 2 == 1, lf3, 0.0).reshape(tb, HG_HEAD)
        e = jnp.exp2(dec)
        qe = (q * e).astype(BF16)
        ke = (k * e).astype(BF16)
        for c, sl in enumerate(chunks):
            scores[c] = scores[c] + jnp.where(pairs[half], _dot_nt(qe[sl], ke[sl]), 0.0)
    last = ref_rows(cum, n, n - 1)
    qdec = (q * jnp.exp2(cum)).astype(BF16)
    kdec = (k * jnp.exp2(last - cum)).astype(BF16)
    updates = [_dot_tn(v[sl], kdec[sl]) for sl in chunks]
    state = state_ref[...]
    outs = []
    for c, sl in enumerate(chunks):
        outs.append(_dot(scores[c].astype(BF16), v[sl]) + _dot_nt(qdec[sl], state.astype(BF16)))
        state = state * jnp.exp2(cum[sl][n - 1:n, :]) + updates[c]
    state_ref[...] = state
    o = jnp.concatenate(outs, axis=0)
    y_ref[0] = (_rms(o, gain_ref[...]) * g_ref[0].astype(F32)).astype(y_ref.dtype)


def _hgrn(hq, lf, kk, v, g, out_gain, *, batch, seq):
    width = hq.shape[1]
    heads = width // HG_HEAD
    tb = HG_BLOCK
    tri = jnp.asarray(np.tril(np.ones((CHUNK, CHUNK), np.float32)), BF16)

    def r3(a):
        return a.reshape(batch, seq, width)

    blk = pl.BlockSpec((1, tb, HG_HEAD), lambda b, h, j: (b, j, h))
    y = pl.pallas_call(
        _hgrn_body,
        grid=(batch, heads, seq // tb),
        in_specs=[blk, blk, blk, blk, blk, _resident((1, HG_HEAD)), _resident((CHUNK, CHUNK))],
        out_specs=blk,
        out_shape=jax.ShapeDtypeStruct((batch, seq, width), BF16),
        scratch_shapes=[pltpu.VMEM((HG_HEAD, HG_HEAD), F32)],
        compiler_params=_params(("parallel", "parallel", "arbitrary")),
        name="hgrn",
    )(r3(hq), r3(lf), r3(kk), r3(v), r3(g), out_gain.reshape(1, HG_HEAD), tri)
    return y.reshape(batch * seq, width)


def _attn_body(q_ref, k_ref, v_ref, o_ref, vt_ref, s_ref, mx_ref, m_ref, acc_ref):
    seq = k_ref.shape[2]
    tk = ATT_TILE
    tqs = ATT_TILE // ATT_SPLIT
    n_tiles = seq // tk

    def transpose_v(j, _):
        vt_ref[j, :MLA_V] = v_ref[0, 0, pl.ds(pl.multiple_of(j * tk, tk), tk), :].astype(F32).T.astype(BF16)
        vt_ref[j, MLA_V:] = jnp.ones((ATT_ONES, tk), BF16)
        return 0

    lax.fori_loop(0, n_tiles, transpose_v, 0)

    key_chunk = lax.broadcasted_iota(jnp.int32, (tk, tqs), 0) // CHUNK
    qry_chunk = lax.broadcasted_iota(jnp.int32, (tk, tqs), 1) // CHUNK
    allowed = [key_chunk <= qry_chunk + (sub * tqs) // CHUNK for sub in range(ATT_SPLIT)]

    def fold_rows(x, op):
        rows, cols = x.shape
        x = x.reshape(rows // ATT_FOLD, ATT_FOLD, cols)
        out = x[0]
        for i in range(1, rows // ATT_FOLD):
            out = op(out, x[i])
        return out

    def aligned(x, m):
        return x if isinstance(x, int) else pl.multiple_of(x, m)

    def scores(qt, kj, buf):
        k_tile = k_ref[0, 0, pl.ds(aligned(kj * tk, tk), tk), :]
        s = _dot(k_tile, q_ref[0, 0, qt])
        s_ref[buf] = s
        mx_ref[buf] = jnp.max(fold_rows(s, jnp.maximum), axis=0, keepdims=True)

    def start():
        m_ref[...] = jnp.full(m_ref.shape, -jnp.inf, F32)
        acc_ref[...] = jnp.zeros(acc_ref.shape, F32)

    def finish(base):
        for sub in range(ATT_SPLIT):
            acc = acc_ref[sub]
            o_ref[0, pl.ds(aligned(base + sub * tqs, tqs), tqs), :] = (
                (acc[:MLA_V] / acc[MLA_V:MLA_V + 1]).T.astype(o_ref.dtype))

    def absorb(kj, buf, masked):
        for sub in range(ATT_SPLIT):
            cols = slice(sub * tqs, (sub + 1) * tqs)
            keys = (sub + 1) * tqs if masked else tk
            vt_tile = vt_ref[kj, :, :keys]
            ss = s_ref[buf, :keys, cols]
            if masked:
                ss = jnp.where(allowed[sub][:keys], ss, -jnp.inf)
                tile_max = jnp.max(fold_rows(ss, jnp.maximum), axis=0, keepdims=True)
            else:
                tile_max = mx_ref[buf, :, cols]
            m = m_ref[sub]
            m_new = jnp.maximum(m, tile_max)
            alpha = jnp.exp2(m - m_new)
            p = jnp.exp2(ss - m_new)
            m_ref[sub] = m_new
            acc_ref[sub] = alpha * acc_ref[sub] + _dot(vt_tile, p.astype(BF16))

    def run(qi, next_q, first, count, last_masked):
        for j in range(count):
            if last_masked and j + 1 == count:
                scores(next_q, 0, 2)
            else:
                scores(qi, first + j + 1, j % 2)
            absorb(first + j, (j + 1) % 2, last_masked and j + 1 == count)

    def q_step(qi, _):
        next_q = jnp.minimum(qi + 1, n_tiles - 1)
        start()
        scores(qi, 1, 1)
        absorb(0, 2, False)
        full = qi - 1

        def group(g, _):
            run(qi, next_q, 1 + g * ATT_UNROLL, ATT_UNROLL, False)
            return 0

        lax.fori_loop(0, full // ATT_UNROLL, group, 0)
        tail_first = 1 + (full // ATT_UNROLL) * ATT_UNROLL
        for rem in range(ATT_UNROLL):
            @pl.when(full % ATT_UNROLL == rem)
            def _(rem=rem):
                run(qi, next_q, tail_first, rem + 1, True)

        finish(pl.multiple_of(qi * tk, tk))
        return 0

    start()
    scores(0, 0, 0)
    scores(min(1, n_tiles - 1), 0, 2)
    absorb(0, 0, True)
    finish(0)
    lax.fori_loop(1, n_tiles, q_step, 0)


def _attention(q, k, v, *, batch, seq, heads):
    qt = pl.BlockSpec((1, 1, seq // ATT_TILE, MLA_HEAD_PAD, ATT_TILE), lambda b, h: (b, h, 0, 0, 0))
    kk = pl.BlockSpec((1, 1, seq, MLA_HEAD_PAD), lambda b, h: (b, h, 0, 0))
    vv = pl.BlockSpec((1, 1, seq, MLA_V), lambda b, h: (b, h, 0, 0))
    vo = pl.BlockSpec((1, seq, MLA_V), lambda b, h: (b, 0, h))
    o = pl.pallas_call(
        _attn_body,
        grid=(batch, heads),
        in_specs=[qt, kk, vv],
        out_specs=vo,
        out_shape=jax.ShapeDtypeStruct((batch, seq, heads * MLA_V), BF16),
        scratch_shapes=[pltpu.VMEM((seq // ATT_TILE, MLA_V + ATT_ONES, ATT_TILE), BF16),
                        pltpu.VMEM((3, ATT_TILE, ATT_TILE), F32),
                        pltpu.VMEM((3, 1, ATT_TILE), F32),
                        pltpu.VMEM((ATT_SPLIT, 1, ATT_TILE // ATT_SPLIT), F32),
                        pltpu.VMEM((ATT_SPLIT, MLA_V + ATT_ONES, ATT_TILE // ATT_SPLIT), F32)],
        compiler_params=_params(("parallel", "parallel")),
        name="attn",
    )(q, k, v)
    return o.reshape(batch * seq, heads * MLA_V)


def _merge_body(yh_ref, ym_ref, gate_ref, h_ref, wh_ref, wm_ref, wo_ref, o_ref):
    d = h_ref.shape[1]
    mix = (gate_ref[:, :d].astype(F32) * _dot(yh_ref[...], wh_ref[...])
           + gate_ref[:, d:].astype(F32) * _dot(ym_ref[...], wm_ref[...]))
    o_ref[...] = h_ref[...] + _dot(mix.astype(BF16), wo_ref[...])


def _merge(yh, ym, gates, h, w_h, w_m, w_o):
    t, d = h.shape
    tm = TOKEN_TILE

    def row(n):
        return pl.BlockSpec((tm, n), lambda i: (i, 0))

    return pl.pallas_call(
        _merge_body,
        grid=(t // tm,),
        in_specs=[row(yh.shape[1]), row(ym.shape[1]), row(2 * d), row(d), _resident(w_h.shape),
                  _resident(w_m.shape), _resident(w_o.shape)],
        out_specs=row(d),
        out_shape=jax.ShapeDtypeStruct((t, d), F32),
        compiler_params=_params(("parallel",)),
        name="merge",
    )(yh, ym, gates, h, w_h.astype(BF16), w_m.astype(BF16), w_o.astype(BF16))


def kernel(x, positions, ffn1_norm, ffn1_w_in, ffn1_w_out, mix_norm, w_in, hg_lb_table, hg_out_norm, w_hg_branch, mla_q_lora_norm, w_q_up, mla_kv_lora_norm, w_kv_up, q_head_norm, k_head_norm, w_mla_branch, w_merge, b_merge, w_out, ffn2_norm, ffn2_w_in, ffn2_w_out, final_norm):
    batch, seq, d = x.shape
    depth = ffn1_norm.shape[0]
    hg_width = hg_lb_table.shape[1]
    heads = w_q_up.shape[2] // MLA_QK
    assert hg_out_norm.shape[1] == HG_HEAD and w_hg_branch.shape[1] == hg_width
    assert q_head_norm.shape[1] == MLA_QK and w_kv_up.shape[2] == heads * (MLA_NOPE + MLA_V)
    assert seq % max(HG_BLOCK, ATT_TILE) == 0 and (batch * seq) % TOKEN_TILE == 0
    t = batch * seq
    pos = positions.reshape(t, 1)
    h = x.reshape(t, d)
    for l in range(depth):
        h1, u = _ffn(h, ffn1_norm[l], ffn1_w_in[l], ffn1_w_out[l], mix_norm[l], emit_h=True, norm_dtype=BF16)
        hq, lf, kk, hv, hg, gates, q, k, v = _proj(
            u, pos, w_in[l], w_merge[l], b_merge[l], hg_lb_table, mla_q_lora_norm[l], w_q_up[l],
            mla_kv_lora_norm[l], w_kv_up[l], q_head_norm[l], k_head_norm[l], layer=l, hg_width=hg_width,
            heads=heads, seq=seq)
        y_hg = _hgrn(hq, lf, kk, hv, hg, hg_out_norm[l], batch=batch, seq=seq)
        y_mla = _attention(q, k, v, batch=batch, seq=seq, heads=heads)
        h2 = _merge(y_hg, y_mla, gates, h1, w_hg_branch[l], w_mla_branch[l], w_out[l])
        (h,) = _ffn(h2, ffn2_norm[l], ffn2_w_in[l], ffn2_w_out[l], final_norm[l], emit_h=False, norm_dtype=F32)
    return h.reshape(batch, seq, d)
```

```python
import functools

import numpy as np
import jax
import jax.numpy as jnp
from jax import lax
from jax.experimental import pallas as pl
from jax.experimental.pallas import tpu as pltpu

F32 = jnp.float32
BF16 = jnp.bfloat16

EPS = 1e-6
CHUNK = 64
HG_HEAD = 128
MLA_NOPE = 128
MLA_ROPE = 64
MLA_V = 128
MLA_QK = MLA_NOPE + MLA_ROPE
MLA_HEAD_PAD = 256
ROPE_THETA = 10000.0
LOG2E = float(np.log2(np.e))

LANES = 128
TOKEN_TILE = 512
PROJ_TILE = 256
FF_CHUNK = 256
HG_BLOCK = 512
HG_HEADS_PER_STEP = 4
ATT_TILE = 512
ATT_SPLIT = 2
ATT_UNROLL = 4
ATT_FOLD = 32
ATT_ONES = 16
VMEM_LIMIT = 56 * 1024 * 1024


def _rms(x, gain):
    return x * lax.rsqrt(jnp.mean(x * x, axis=-1, keepdims=True) + EPS) * gain


def _silu(x):
    return x * jax.nn.sigmoid(x)


def _dot(a, b):
    return jnp.dot(a, b, preferred_element_type=F32)


def _dot_nt(a, b):
    return lax.dot_general(a, b, (((1,), (1,)), ((), ())), preferred_element_type=F32)


def _dot_tn(a, b):
    return lax.dot_general(a, b, (((0,), (0,)), ((), ())), preferred_element_type=F32)


def _resident(shape):
    zeros = (0,) * len(shape)
    return pl.BlockSpec(shape, lambda *_: zeros, pipeline_mode=pl.Buffered(1))


def _params(semantics):
    return pltpu.CompilerParams(dimension_semantics=semantics, vmem_limit_bytes=VMEM_LIMIT)


def _ffn_body(x_ref, gin_ref, win_ref, wout_ref, gout_ref, *rest, emit_h):
    if emit_h:
        h_ref, n_ref, a_ref = rest
    else:
        n_ref, a_ref = rest
    n_chunks, _, two_fc = win_ref.shape
    fc = two_fc // 2
    x = x_ref[...]
    xn = _rms(x, gin_ref[...]).astype(BF16)
    for c in range(n_chunks):
        hc = _dot(xn, win_ref[c])
        a_ref[:, c * fc:(c + 1) * fc] = (_silu(hc[:, :fc]) * hc[:, fc:]).astype(BF16)
    h = x + 0.5 * _dot(a_ref[...], wout_ref[...])
    if emit_h:
        h_ref[...] = h
    n_ref[...] = _rms(h, gout_ref[...]).astype(n_ref.dtype)


def _ffn(x, gin, w_in, w_out, gout, *, emit_h, norm_dtype):
    t, d = x.shape
    d_ff = w_out.shape[0]
    nc = d_ff // FF_CHUNK
    gate = w_in[:, :d_ff].reshape(d, nc, 1, FF_CHUNK)
    up = w_in[:, d_ff:].reshape(d, nc, 1, FF_CHUNK)
    win = jnp.concatenate([gate, up], axis=2).transpose(1, 0, 2, 3).reshape(nc, d, 2 * FF_CHUNK).astype(BF16)
    tm = TOKEN_TILE
    row = pl.BlockSpec((tm, d), lambda i: (i, 0))
    out_shape = [jax.ShapeDtypeStruct((t, d), norm_dtype)]
    out_specs = [row]
    if emit_h:
        out_shape = [jax.ShapeDtypeStruct((t, d), F32)] + out_shape
        out_specs = [row] + out_specs
    return pl.pallas_call(
        functools.partial(_ffn_body, emit_h=emit_h),
        grid=(t // tm,),
        in_specs=[row, _resident((1, d)), _resident(win.shape), _resident((d_ff, d)), _resident((1, d))],
        out_specs=out_specs,
        out_shape=out_shape,
        scratch_shapes=[pltpu.VMEM((tm, d_ff), BF16)],
        compiler_params=_params(("parallel",)),
        name="ffn1" if emit_h else "ffn2",
    )(x, gin.reshape(1, d), win, w_out.astype(BF16), gout.reshape(1, d))


def _proj_body(u_ref, pos_ref, w_ref, wm_ref, bm_ref, tab_ref, gq_ref, wq_ref, gkv_ref, wkv_ref, qr_gain_ref,
               kn_gain_ref, kr_gain_ref, invf_ref,
               hq_ref, lf_ref, kk_ref, hv_ref, hg_ref, gate_ref, q_ref, k_ref, v_ref, *, layer, heads):
    w = hq_ref.shape[1]
    tm, d = u_ref.shape
    q_lora, kv_lora = wq_ref.shape[0], wkv_ref.shape[0]
    u = u_ref[...]

    c = _dot(u, w_ref[:, 4 * w:])
    qraw = _dot(_rms(c[:, :q_lora], gq_ref[...]).astype(BF16), wq_ref[...])
    kvraw = _dot(_rms(c[:, q_lora:q_lora + kv_lora], gkv_ref[...]).astype(BF16), wkv_ref[...])
    kpe = c[:, q_lora + kv_lora:]

    hq_ref[...] = _silu(_dot(u, w_ref[:, 0:w])).astype(hq_ref.dtype)
    tab = tab_ref[...]
    e = jnp.exp(tab - jnp.max(tab, axis=0, keepdims=True))
    lb = jnp.sum(e[:layer + 1], axis=0, keepdims=True) / jnp.sum(e, axis=0, keepdims=True)
    z = _dot(u, w_ref[:, w:2 * w])
    ez = jnp.exp(-jnp.abs(z))
    r = 1.0 / (1.0 + ez)
    a = ez * r
    pos = z >= 0
    lf_ref[...] = jnp.log(lb + (1.0 - lb) * jnp.where(pos, r, a)) * LOG2E
    kk_ref[...] = ((1.0 - lb) * jnp.where(pos, a, r)).astype(BF16)
    hv_ref[...] = _dot(u, w_ref[:, 2 * w:3 * w]).astype(BF16)
    hg_ref[...] = _silu(_dot(u, w_ref[:, 3 * w:4 * w])).astype(BF16)
    for j in range(2):
        gm = _dot(u, wm_ref[:, j * d:(j + 1) * d]) + bm_ref[:, j * d:(j + 1) * d]
        gate_ref[:, j * d:(j + 1) * d] = jax.nn.sigmoid(gm).astype(BF16)

    ang = pos_ref[...].astype(F32) * invf_ref[...]
    first = lax.broadcasted_iota(jnp.int32, (tm, LANES), 1) < MLA_ROPE
    cs = jnp.where(first, jnp.cos(ang), jnp.sin(ang))

    def rope(block, gain_cs):
        t = block * gain_cs
        return t + pltpu.roll(t, MLA_ROPE, 1)

    def sumsq(x):
        return jnp.sum(x * x, axis=-1, keepdims=True)

    k_rope = rope(kpe, kr_gain_ref[...] * cs)
    k_pe_ss = 0.5 * sumsq(kpe)
    q_gain_cs = qr_gain_ref[...] * cs
    scale = MLA_QK ** -0.5 * LOG2E
    nope_w = heads * MLA_NOPE
    for h in range(heads):
        lo, hi = h * MLA_NOPE, (h + 1) * MLA_NOPE
        qn = qraw[:, lo:hi]
        qr = qraw[:, nope_w + lo:nope_w + hi]
        sq = lax.rsqrt((sumsq(qn) + 0.5 * sumsq(qr)) / MLA_QK + EPS) * scale
        o = h * MLA_HEAD_PAD
        q_head = jnp.concatenate([qn * sq, rope(qr, q_gain_cs) * sq], axis=1)
        q_ref[0, h, 0] = q_head.T.astype(BF16)
        kn = kvraw[:, lo:hi]
        sk = lax.rsqrt((sumsq(kn) + k_pe_ss) / MLA_QK + EPS)
        k_ref[0, h, :, :MLA_NOPE] = (kn * sk * kn_gain_ref[...]).astype(BF16)
        k_ref[0, h, :, MLA_NOPE:] = jnp.where(first, k_rope * sk, 0.0).astype(BF16)
        v_ref[0, h] = kvraw[:, nope_w + h * MLA_V:nope_w + (h + 1) * MLA_V].astype(BF16)


def _rot_cols(w):
    half = w.shape[-1] // 2
    return jnp.concatenate([-w[..., half:], w[..., :half]], axis=-1)


def _rot_gain(g):
    half = g.shape[-1] // 2
    return jnp.concatenate([g[..., half:], g[..., :half]], axis=-1)


def _proj(u, pos, w_in, w_merge, b_merge, lb_table, gq, w_q_up, gkv, w_kv_up, q_head_gain, k_head_gain,
          *, layer, hg_width, heads, seq):
    t, d = u.shape
    q_lora, kv_lora = w_q_up.shape[0], w_kv_up.shape[0]
    w_in_ext = jnp.concatenate([w_in, _rot_cols(w_in[:, -MLA_ROPE:])], axis=1).astype(BF16)
    cols = w_in_ext.shape[1]
    assert cols == 4 * hg_width + q_lora + kv_lora + 2 * MLA_ROPE
    wq = w_q_up.reshape(q_lora, heads, MLA_QK)
    wq_rope = wq[:, :, MLA_NOPE:]
    wq_ext = jnp.concatenate(
        [wq[:, :, :MLA_NOPE].reshape(q_lora, heads * MLA_NOPE),
         jnp.concatenate([wq_rope, _rot_cols(wq_rope)], axis=-1).reshape(q_lora, heads * 2 * MLA_ROPE)],
        axis=1).astype(BF16)
    wkv = w_kv_up.reshape(kv_lora, heads, MLA_NOPE + MLA_V)
    wkv_ext = jnp.concatenate([wkv[:, :, :MLA_NOPE].reshape(kv_lora, heads * MLA_NOPE),
                               wkv[:, :, MLA_NOPE:].reshape(kv_lora, heads * MLA_V)], axis=1).astype(BF16)

    def rope_gain(g):
        return jnp.concatenate([g[MLA_NOPE:], _rot_gain(g[MLA_NOPE:])]).reshape(1, 2 * MLA_ROPE)

    inv_freq = ROPE_THETA ** (-jnp.arange(0, MLA_ROPE, 2, dtype=F32) / MLA_ROPE)
    invf = jnp.tile(inv_freq, 4).reshape(1, LANES)
    nope_gain = (q_head_gain[:MLA_NOPE] * k_head_gain[:MLA_NOPE]).reshape(1, MLA_NOPE)
    tm = PROJ_TILE

    def row(n):
        return pl.BlockSpec((tm, n), lambda i: (i, 0))

    def head_major(n):
        return pl.BlockSpec((1, heads, tm, n), lambda i: (i // per_seq, 0, i % per_seq, 0))

    per_seq = seq // tm
    per_att = ATT_TILE // tm
    qt_spec = pl.BlockSpec((1, heads, 1, MLA_HEAD_PAD, tm),
                           lambda i: (i // per_seq, 0, (i % per_seq) // per_att, 0, i % per_att))
    return pl.pallas_call(
        functools.partial(_proj_body, layer=layer, heads=heads),
        grid=(t // tm,),
        in_specs=[row(d), row(1), _resident((d, cols)), _resident((d, 2 * d)), _resident((1, 2 * d)),
                  _resident(lb_table.shape), _resident((1, q_lora)), _resident(wq_ext.shape),
                  _resident((1, kv_lora)), _resident(wkv_ext.shape), _resident((1, LANES)),
                  _resident((1, MLA_NOPE)), _resident((1, LANES)), _resident((1, LANES))],
        out_specs=[row(hg_width), row(hg_width), row(hg_width), row(hg_width), row(hg_width), row(2 * d),
                   qt_spec, head_major(MLA_HEAD_PAD), head_major(MLA_V)],
        out_shape=[jax.ShapeDtypeStruct((t, hg_width), BF16),
                   jax.ShapeDtypeStruct((t, hg_width), F32),
                   jax.ShapeDtypeStruct((t, hg_width), BF16),
                   jax.ShapeDtypeStruct((t, hg_width), BF16),
                   jax.ShapeDtypeStruct((t, hg_width), BF16),
                   jax.ShapeDtypeStruct((t, 2 * d), BF16),
                   jax.ShapeDtypeStruct((t // seq, heads, seq // ATT_TILE, MLA_HEAD_PAD, ATT_TILE), BF16),
                   jax.ShapeDtypeStruct((t // seq, heads, seq, MLA_HEAD_PAD), BF16),
                   jax.ShapeDtypeStruct((t // seq, heads, seq, MLA_V), BF16)],
        compiler_params=_params(("parallel",)),
        name="proj",
    )(u, pos, w_in_ext, w_merge.astype(BF16), b_merge.reshape(1, 2 * d), lb_table, gq.reshape(1, q_lora), wq_ext,
      gkv.reshape(1, kv_lora), wkv_ext, rope_gain(q_head_gain), nope_gain, rope_gain(k_head_gain), invf)


def _split3(x):
    hi = x.astype(BF16)
    r = x - hi.astype(F32)
    mid = r.astype(BF16)
    lo = (r - mid.astype(F32)).astype(BF16)
    return hi, mid, lo


def _hgrn_body(q_ref, lf_ref, kk_ref, v_ref, g_ref, gain_ref, tri_ref, y_ref, state_ref):
    @pl.when(pl.program_id(2) == 0)
    def _():
        state_ref[...] = jnp.zeros_like(state_ref)

    n = CHUNK
    tb = q_ref.shape[1]
    chunks = [slice(c * n, (c + 1) * n) for c in range(tb // n)]
    trow = lax.broadcasted_iota(jnp.int32, (n, n), 0)
    tcol = lax.broadcasted_iota(jnp.int32, (n, n), 1)
    tri = tri_ref[...]
    halves = [1 << i for i in range(n.bit_length() - 1)]
    pairs = {half: (trow // (2 * half) == tcol // (2 * half)) & (trow % (2 * half) >= half)
             & (tcol % (2 * half) < half) for half in halves}

    def ref_rows(x, size, row):
        blocks = x.reshape(tb // size, size, HG_HEAD)[:, row:row + 1, :]
        return jnp.broadcast_to(blocks, (tb // size, size, HG_HEAD)).reshape(tb, HG_HEAD)

    r8 = lax.broadcasted_iota(jnp.int32, (1, 8, HG_HEAD), 1)

    def head(q, lf, k, v, g, s_ref):
        lf3 = lf.reshape(tb // 8, 8, HG_HEAD)
        split = jnp.concatenate(_split3(lf), axis=1)
        cums = []
        for sl in chunks:
            r = _dot(tri, split[sl])
            cums.append(r[:, :HG_HEAD] + r[:, HG_HEAD:2 * HG_HEAD] + r[:, 2 * HG_HEAD:])
        cum = jnp.concatenate(cums, axis=0)
        diag = jnp.sum(q * k, axis=-1, keepdims=True)
        scores = [jnp.where(trow == tcol, diag[sl], 0.0) for sl in chunks]
        for half in halves:
            size = 2 * half
            if half >= 4:
                diff = pltpu.bitcast(cum - ref_rows(cum, size, half - 1), jnp.uint32)
                dec = pltpu.bitcast(diff | jnp.uint32(0x80000000), F32)
            elif half == 2:
                dec = jnp.where(r8 % 4 == 3, lf3 + pltpu.roll(lf3, 1, 1),
                                jnp.where(r8 % 4 == 2, lf3,
                                          jnp.where(r8 % 4 == 0, pltpu.roll(lf3, 7, 1), 0.0))).reshape(tb, HG_HEAD)
            else:
                dec = jnp.where(r8 % 2 == 1, lf3, 0.0).reshape(tb, HG_HEAD)
            e = jnp.exp2(dec)
            qe = (q * e).astype(BF16)
            ke = (k * e).astype(BF16)
            for c, sl in enumerate(chunks):
                scores[c] = scores[c] + jnp.where(pairs[half], _dot_nt(qe[sl], ke[sl]), 0.0)
        last = ref_rows(cum, n, n - 1)
        qdec = (q * jnp.exp2(cum)).astype(BF16)
        kdec = (k * jnp.exp2(last - cum)).astype(BF16)
        updates = [_dot_tn(kdec[sl], v[sl]) for sl in chunks]
        lasts = jnp.concatenate([cum[sl][n - 1:n, :] for sl in chunks], axis=0)
        decay_cols = jnp.concatenate([jnp.exp2(lasts)] * (HG_HEAD // len(chunks)), axis=0).T
        state = s_ref[...]
        outs = []
        for c, sl in enumerate(chunks):
            lhs = jnp.concatenate([qdec[sl], scores[c].astype(BF16)], axis=1)
            rhs = jnp.concatenate([state.astype(BF16), v[sl]], axis=0)
            outs.append(_dot(lhs, rhs))
            state = state * decay_cols[:, c:c + 1] + updates[c]
        s_ref[...] = state
        return _rms(jnp.concatenate(outs, axis=0), gain_ref[...]) * g

    for hd in range(q_ref.shape[2] // HG_HEAD):
        lanes = slice(hd * HG_HEAD, (hd + 1) * HG_HEAD)
        y_ref[0, :, lanes] = head(q_ref[0, :, lanes].astype(F32), lf_ref[0, :, lanes],
                                  kk_ref[0, :, lanes].astype(F32), v_ref[0, :, lanes],
                                  g_ref[0, :, lanes].astype(F32), state_ref.at[hd]).astype(y_ref.dtype)


def _hgrn(hq, lf, kk, v, g, out_gain, *, batch, seq):
    width = hq.shape[1]
    heads = width // HG_HEAD
    tb = HG_BLOCK
    tri = jnp.asarray(np.tril(np.ones((CHUNK, CHUNK), np.float32)), BF16)

    def r3(a):
        return a.reshape(batch, seq, width)

    blk = pl.BlockSpec((1, tb, HG_HEADS_PER_STEP * HG_HEAD), lambda b, h, j: (b, j, h))
    y = pl.pallas_call(
        _hgrn_body,
        grid=(batch, heads // HG_HEADS_PER_STEP, seq // tb),
        in_specs=[blk, blk, blk, blk, blk, _resident((1, HG_HEAD)), _resident((CHUNK, CHUNK))],
        out_specs=blk,
        out_shape=jax.ShapeDtypeStruct((batch, seq, width), BF16),
        scratch_shapes=[pltpu.VMEM((HG_HEADS_PER_STEP, HG_HEAD, HG_HEAD), F32)],
        compiler_params=_params(("parallel", "parallel", "arbitrary")),
        name="hgrn",
    )(r3(hq), r3(lf), r3(kk), r3(v), r3(g), out_gain.reshape(1, HG_HEAD), tri)
    return y.reshape(batch * seq, width)


def _attn_body(q_ref, k_ref, v_ref, o_ref, vt_ref, s_ref, mx_ref, m_ref, acc_ref):
    seq = k_ref.shape[2]
    tk = ATT_TILE
    tqs = ATT_TILE // ATT_SPLIT
    n_tiles = seq // tk

    def transpose_v(j, _):
        vt_ref[j, :MLA_V] = v_ref[0, 0, pl.ds(pl.multiple_of(j * tk, tk), tk), :].astype(F32).T.astype(BF16)
        vt_ref[j, MLA_V:] = jnp.ones((ATT_ONES, tk), BF16)
        return 0

    lax.fori_loop(0, n_tiles, transpose_v, 0)

    key_chunk = lax.broadcasted_iota(jnp.int32, (tk, tqs), 0) // CHUNK
    qry_chunk = lax.broadcasted_iota(jnp.int32, (tk, tqs), 1) // CHUNK
    allowed = [key_chunk <= qry_chunk + (sub * tqs) // CHUNK for sub in range(ATT_SPLIT)]

    def fold_rows(x, op):
        rows, cols = x.shape
        x = x.reshape(rows // ATT_FOLD, ATT_FOLD, cols)
        out = x[0]
        for i in range(1, rows // ATT_FOLD):
            out = op(out, x[i])
        return out

    def aligned(x, m):
        return x if isinstance(x, int) else pl.multiple_of(x, m)

    def scores(qt, kj, buf):
        k_tile = k_ref[0, 0, pl.ds(aligned(kj * tk, tk), tk), :]
        s = _dot(k_tile, q_ref[0, 0, qt])
        s_ref[buf] = s
        mx_ref[buf] = jnp.max(fold_rows(s, jnp.maximum), axis=0, keepdims=True)

    def start():
        m_ref[...] = jnp.full(m_ref.shape, -jnp.inf, F32)
        acc_ref[...] = jnp.zeros(acc_ref.shape, F32)

    def finish(base):
        for sub in range(ATT_SPLIT):
            acc = acc_ref[sub]
            o_ref[0, pl.ds(aligned(base + sub * tqs, tqs), tqs), :] = (
                (acc[:MLA_V] / acc[MLA_V:MLA_V + 1]).T.astype(o_ref.dtype))

    def absorb(kj, buf, masked):
        for sub in range(ATT_SPLIT):
            cols = slice(sub * tqs, (sub + 1) * tqs)
            keys = (sub + 1) * tqs if masked else tk
            vt_tile = vt_ref[kj, :, :keys]
            ss = s_ref[buf, :keys, cols]
            if masked:
                ss = jnp.where(allowed[sub][:keys], ss, -jnp.inf)
                tile_max = jnp.max(fold_rows(ss, jnp.maximum), axis=0, keepdims=True)
            else:
                tile_max = mx_ref[buf, :, cols]
            m = m_ref[sub]
            m_new = jnp.maximum(m, tile_max)
            alpha = jnp.exp2(m - m_new)
            p = jnp.exp2(ss - m_new)
            m_ref[sub] = m_new
            acc_ref[sub] = alpha * acc_ref[sub] + _dot(vt_tile, p.astype(BF16))

    def run(qi, next_q, first, count, last_masked):
        for j in range(count):
            if last_masked and j + 1 == count:
                scores(next_q, 0, 2)
            else:
                scores(qi, first + j + 1, j % 2)
            absorb(first + j, (j + 1) % 2, last_masked and j + 1 == count)

    def q_step(qi, _):
        next_q = jnp.minimum(qi + 1, n_tiles - 1)
        start()
        scores(qi, 1, 1)
        absorb(0, 2, False)
        full = qi - 1

        def group(g, _):
            run(qi, next_q, 1 + g * ATT_UNROLL, ATT_UNROLL, False)
            return 0

        lax.fori_loop(0, full // ATT_UNROLL, group, 0)
        tail_first = 1 + (full // ATT_UNROLL) * ATT_UNROLL
        for rem in range(ATT_UNROLL):
            @pl.when(full % ATT_UNROLL == rem)
            def _(rem=rem):
                run(qi, next_q, tail_first, rem + 1, True)

        finish(pl.multiple_of(qi * tk, tk))
        return 0

    start()
    scores(0, 0, 0)
    scores(min(1, n_tiles - 1), 0, 2)
    absorb(0, 0, True)
    finish(0)
    lax.fori_loop(1, n_tiles, q_step, 0)


def _attention(q, k, v, *, batch, seq, heads):
    qt = pl.BlockSpec((1, 1, seq // ATT_TILE, MLA_HEAD_PAD, ATT_TILE), lambda b, h: (b, h, 0, 0, 0))
    kk = pl.BlockSpec((1, 1, seq, MLA_HEAD_PAD), lambda b, h: (b, h, 0, 0))
    vv = pl.BlockSpec((1, 1, seq, MLA_V), lambda b, h: (b, h, 0, 0))
    vo = pl.BlockSpec((1, seq, MLA_V), lambda b, h: (b, 0, h))
    o = pl.pallas_call(
        _attn_body,
        grid=(batch, heads),
        in_specs=[qt, kk, vv],
        out_specs=vo,
        out_shape=jax.ShapeDtypeStruct((batch, seq, heads * MLA_V), BF16),
        scratch_shapes=[pltpu.VMEM((seq // ATT_TILE, MLA_V + ATT_ONES, ATT_TILE), BF16),
                        pltpu.VMEM((3, ATT_TILE, ATT_TILE), F32),
                        pltpu.VMEM((3, 1, ATT_TILE), F32),
                        pltpu.VMEM((ATT_SPLIT, 1, ATT_TILE // ATT_SPLIT), F32),
                        pltpu.VMEM((ATT_SPLIT, MLA_V + ATT_ONES, ATT_TILE // ATT_SPLIT), F32)],
        compiler_params=_params(("parallel", "parallel")),
        name="attn",
    )(q, k, v)
    return o.reshape(batch * seq, heads * MLA_V)


def _merge_body(yh_ref, ym_ref, gate_ref, h_ref, wh_ref, wm_ref, wo_ref, o_ref):
    d = h_ref.shape[1]
    mix = (gate_ref[:, :d].astype(F32) * _dot(yh_ref[...], wh_ref[...])
           + gate_ref[:, d:].astype(F32) * _dot(ym_ref[...], wm_ref[...]))
    o_ref[...] = h_ref[...] + _dot(mix.astype(BF16), wo_ref[...])


def _merge(yh, ym, gates, h, w_h, w_m, w_o):
    t, d = h.shape
    tm = TOKEN_TILE

    def row(n):
        return pl.BlockSpec((tm, n), lambda i: (i, 0))

    return pl.pallas_call(
        _merge_body,
        grid=(t // tm,),
        in_specs=[row(yh.shape[1]), row(ym.shape[1]), row(2 * d), row(d), _resident(w_h.shape),
                  _resident(w_m.shape), _resident(w_o.shape)],
        out_specs=row(d),
        out_shape=jax.ShapeDtypeStruct((t, d), F32),
        compiler_params=_params(("parallel",)),
        name="merge",
    )(yh, ym, gates, h, w_h.astype(BF16), w_m.astype(BF16), w_o.astype(BF16))


def kernel(x, positions, ffn1_norm, ffn1_w_in, ffn1_w_out, mix_norm, w_in, hg_lb_table, hg_out_norm, w_hg_branch, mla_q_lora_norm, w_q_up, mla_kv_lora_norm, w_kv_up, q_head_norm, k_head_norm, w_mla_branch, w_merge, b_merge, w_out, ffn2_norm, ffn2_w_in, ffn2_w_out, final_norm):
    batch, seq, d = x.shape
    depth = ffn1_norm.shape[0]
    hg_width = hg_lb_table.shape[1]
    heads = w_q_up.shape[2] // MLA_QK
    assert hg_out_norm.shape[1] == HG_HEAD and w_hg_branch.shape[1] == hg_width
    assert q_head_norm.shape[1] == MLA_QK and w_kv_up.shape[2] == heads * (MLA_NOPE + MLA_V)
    assert seq % max(HG_BLOCK, ATT_TILE) == 0 and (batch * seq) % TOKEN_TILE == 0
    t = batch * seq
    pos = positions.reshape(t, 1)
    h = x.reshape(t, d)
    for l in range(depth):
        h1, u = _ffn(h, ffn1_norm[l], ffn1_w_in[l], ffn1_w_out[l], mix_norm[l], emit_h=True, norm_dtype=BF16)
        hq, lf, kk, hv, hg, gates, q, k, v = _proj(
            u, pos, w_in[l], w_merge[l], b_merge[l], hg_lb_table, mla_q_lora_norm[l], w_q_up[l],
            mla_kv_lora_norm[l], w_kv_up[l], q_head_norm[l], k_head_norm[l], layer=l, hg_width=hg_width,
            heads=heads, seq=seq)
        y_hg = _hgrn(hq, lf, kk, hv, hg, hg_out_norm[l], batch=batch, seq=seq)
        y_mla = _attention(q, k, v, batch=batch, seq=seq, heads=heads)
        h2 = _merge(y_hg, y_mla, gates, h1, w_hg_branch[l], w_mla_branch[l], w_out[l])
        (h,) = _ffn(h2, ffn2_norm[l], ffn2_w_in[l], ffn2_w_out[l], final_norm[l], emit_h=False, norm_dtype=F32)
    return h.reshape(batch, seq, d)
```

```python
import functools

import numpy as np
import jax
import jax.numpy as jnp
from jax import lax
from jax.experimental import pallas as pl
from jax.experimental.pallas import tpu as pltpu

F32 = jnp.float32
BF16 = jnp.bfloat16

EPS = 1e-6
CHUNK = 64
HG_HEAD = 128
MLA_NOPE = 128
MLA_ROPE = 64
MLA_V = 128
MLA_QK = MLA_NOPE + MLA_ROPE
MLA_HEAD_PAD = 256
ROPE_THETA = 10000.0
LOG2E = float(np.log2(np.e))

LANES = 128
TOKEN_TILE = 512
PROJ_TILE = 256
FF_CHUNK = 256
HG_BLOCK = 512
HG_HEADS_PER_STEP = 4
ATT_TILE = 512
ATT_SPLIT = 2
ATT_UNROLL = 4
ATT_FOLD = 32
ATT_ONES = 16
VMEM_LIMIT = 56 * 1024 * 1024


def _rms(x, gain):
    return x * lax.rsqrt(jnp.mean(x * x, axis=-1, keepdims=True) + EPS) * gain


def _silu(x):
    return x * jax.nn.sigmoid(x)


def _dot(a, b):
    return jnp.dot(a, b, preferred_element_type=F32)


def _dot_nt(a, b):
    return lax.dot_general(a, b, (((1,), (1,)), ((), ())), preferred_element_type=F32)


def _dot_tn(a, b):
    return lax.dot_general(a, b, (((0,), (0,)), ((), ())), preferred_element_type=F32)


def _resident(shape):
    zeros = (0,) * len(shape)
    return pl.BlockSpec(shape, lambda *_: zeros, pipeline_mode=pl.Buffered(1))


def _params(semantics):
    return pltpu.CompilerParams(dimension_semantics=semantics, vmem_limit_bytes=VMEM_LIMIT)


def _ffn_body(x_ref, gin_ref, win_ref, wout_ref, gout_ref, *rest, emit_h):
    if emit_h:
        h_ref, n_ref, a_ref = rest
    else:
        n_ref, a_ref = rest
    n_chunks, _, two_fc = win_ref.shape
    fc = two_fc // 2
    x = x_ref[...]
    xn = _rms(x, gin_ref[...]).astype(BF16)
    for c in range(n_chunks):
        hc = _dot(xn, win_ref[c])
        a_ref[:, c * fc:(c + 1) * fc] = (_silu(hc[:, :fc]) * hc[:, fc:]).astype(BF16)
    h = x + 0.5 * _dot(a_ref[...], wout_ref[...])
    if emit_h:
        h_ref[...] = h
    n_ref[...] = _rms(h, gout_ref[...]).astype(n_ref.dtype)


def _ffn(x, gin, w_in, w_out, gout, *, emit_h, norm_dtype):
    t, d = x.shape
    d_ff = w_out.shape[0]
    nc = d_ff // FF_CHUNK
    gate = w_in[:, :d_ff].reshape(d, nc, 1, FF_CHUNK)
    up = w_in[:, d_ff:].reshape(d, nc, 1, FF_CHUNK)
    win = jnp.concatenate([gate, up], axis=2).transpose(1, 0, 2, 3).reshape(nc, d, 2 * FF_CHUNK).astype(BF16)
    tm = TOKEN_TILE
    row = pl.BlockSpec((tm, d), lambda i: (i, 0))
    out_shape = [jax.ShapeDtypeStruct((t, d), norm_dtype)]
    out_specs = [row]
    if emit_h:
        out_shape = [jax.ShapeDtypeStruct((t, d), F32)] + out_shape
        out_specs = [row] + out_specs
    return pl.pallas_call(
        functools.partial(_ffn_body, emit_h=emit_h),
        grid=(t // tm,),
        in_specs=[row, _resident((1, d)), _resident(win.shape), _resident((d_ff, d)), _resident((1, d))],
        out_specs=out_specs,
        out_shape=out_shape,
        scratch_shapes=[pltpu.VMEM((tm, d_ff), BF16)],
        compiler_params=_params(("parallel",)),
        name="ffn1" if emit_h else "ffn2",
    )(x, gin.reshape(1, d), win, w_out.astype(BF16), gout.reshape(1, d))


def _proj_body(u_ref, pos_ref, w_ref, wm_ref, bm_ref, tab_ref, gq_ref, wq_ref, gkv_ref, wkv_ref, qr_gain_ref,
               kn_gain_ref, kr_gain_ref, invf_ref,
               hq_ref, lf_ref, kk_ref, hv_ref, hg_ref, gate_ref, q_ref, k_ref, v_ref, *, layer, heads):
    w = hq_ref.shape[1]
    tm, d = u_ref.shape
    q_lora, kv_lora = wq_ref.shape[0], wkv_ref.shape[0]
    u = u_ref[...]

    c = _dot(u, w_ref[:, 4 * w:])
    hq_ref[...] = _silu(_dot(u, w_ref[:, 0:w])).astype(hq_ref.dtype)
    qraw = _dot(_rms(c[:, :q_lora], gq_ref[...]).astype(BF16), wq_ref[...])
    kvraw = _dot(_rms(c[:, q_lora:q_lora + kv_lora], gkv_ref[...]).astype(BF16), wkv_ref[...])
    kpe = c[:, q_lora + kv_lora:]

    tab = tab_ref[...]
    e = jnp.exp(tab - jnp.max(tab, axis=0, keepdims=True))
    lb = jnp.sum(e[:layer + 1], axis=0, keepdims=True) / jnp.sum(e, axis=0, keepdims=True)
    z = _dot(u, w_ref[:, w:2 * w])
    ez = jnp.exp(-jnp.abs(z))
    r = 1.0 / (1.0 + ez)
    a = ez * r
    pos = z >= 0
    lf_ref[...] = jnp.log(lb + (1.0 - lb) * jnp.where(pos, r, a)) * LOG2E
    kk_ref[...] = ((1.0 - lb) * jnp.where(pos, a, r)).astype(BF16)
    hv_ref[...] = _dot(u, w_ref[:, 2 * w:3 * w]).astype(BF16)
    hg_ref[...] = _silu(_dot(u, w_ref[:, 3 * w:4 * w])).astype(BF16)
    for j in range(2):
        gm = _dot(u, wm_ref[:, j * d:(j + 1) * d]) + bm_ref[:, j * d:(j + 1) * d]
        gate_ref[:, j * d:(j + 1) * d] = jax.nn.sigmoid(gm).astype(BF16)

    half = MLA_ROPE // 2
    rows4 = tm // 4
    lane4 = lax.broadcasted_iota(jnp.int32, (rows4, LANES), 1)
    posf = pos_ref[...].astype(F32)
    pos4 = [posf[g * rows4:(g + 1) * rows4] for g in range(4)]
    packed = jnp.where(lane4 < half, pos4[0],
                       jnp.where(lane4 < 2 * half, pos4[1],
                                 jnp.where(lane4 < 3 * half, pos4[2], pos4[3]))) * invf_ref[...]
    cos4, sin4 = jnp.cos(packed), jnp.sin(packed)
    spread = []
    for g in range(4):
        cg = pltpu.roll(cos4, LANES - half * g, 1) if g else cos4
        sg = pltpu.roll(sin4, LANES - half * g, 1) if g else sin4
        cc = jnp.where(lane4 < half, cg, pltpu.roll(cg, half, 1))
        ss = jnp.where(lane4 < 3 * half, pltpu.roll(sg, 2 * half, 1), pltpu.roll(sg, 3 * half, 1))
        spread.append(jnp.where(lane4 < 2 * half, cc, ss))
    cs = jnp.concatenate(spread, axis=0)
    first = lax.broadcasted_iota(jnp.int32, (tm, LANES), 1) < MLA_ROPE

    def rope(block, gain_cs):
        t = block * gain_cs
        return t + pltpu.roll(t, MLA_ROPE, 1)

    def sumsq(x):
        return jnp.sum(x * x, axis=-1, keepdims=True)

    k_rope = rope(kpe, kr_gain_ref[...] * cs)
    k_pe_ss = 0.5 * sumsq(kpe)
    q_gain_cs = qr_gain_ref[...] * cs
    scale = MLA_QK ** -0.5 * LOG2E
    nope_w = heads * MLA_NOPE
    for h in range(heads):
        lo, hi = h * MLA_NOPE, (h + 1) * MLA_NOPE
        qn = qraw[:, lo:hi]
        qr = qraw[:, nope_w + lo:nope_w + hi]
        sq = lax.rsqrt((sumsq(qn) + 0.5 * sumsq(qr)) / MLA_QK + EPS) * scale
        o = h * MLA_HEAD_PAD
        q_head = jnp.concatenate([qn * sq, rope(qr, q_gain_cs) * sq], axis=1)
        q_ref[0, h, 0] = q_head.T.astype(BF16)
        kn = kvraw[:, lo:hi]
        sk = lax.rsqrt((sumsq(kn) + k_pe_ss) / MLA_QK + EPS)
        k_ref[0, h, :, :MLA_NOPE] = (kn * sk * kn_gain_ref[...]).astype(BF16)
        k_ref[0, h, :, MLA_NOPE:] = jnp.where(first, k_rope * sk, 0.0).astype(BF16)
        v_ref[0, h] = kvraw[:, nope_w + h * MLA_V:nope_w + (h + 1) * MLA_V].astype(BF16)


def _rot_cols(w):
    half = w.shape[-1] // 2
    return jnp.concatenate([-w[..., half:], w[..., :half]], axis=-1)


def _rot_gain(g):
    half = g.shape[-1] // 2
    return jnp.concatenate([g[..., half:], g[..., :half]], axis=-1)


def _proj(u, pos, w_in, w_merge, b_merge, lb_table, gq, w_q_up, gkv, w_kv_up, q_head_gain, k_head_gain,
          *, layer, hg_width, heads, seq):
    t, d = u.shape
    q_lora, kv_lora = w_q_up.shape[0], w_kv_up.shape[0]
    w_in_ext = jnp.concatenate([w_in, _rot_cols(w_in[:, -MLA_ROPE:])], axis=1).astype(BF16)
    cols = w_in_ext.shape[1]
    assert cols == 4 * hg_width + q_lora + kv_lora + 2 * MLA_ROPE
    wq = w_q_up.reshape(q_lora, heads, MLA_QK)
    wq_rope = wq[:, :, MLA_NOPE:]
    wq_ext = jnp.concatenate(
        [wq[:, :, :MLA_NOPE].reshape(q_lora, heads * MLA_NOPE),
         jnp.concatenate([wq_rope, _rot_cols(wq_rope)], axis=-1).reshape(q_lora, heads * 2 * MLA_ROPE)],
        axis=1).astype(BF16)
    wkv = w_kv_up.reshape(kv_lora, heads, MLA_NOPE + MLA_V)
    wkv_ext = jnp.concatenate([wkv[:, :, :MLA_NOPE].reshape(kv_lora, heads * MLA_NOPE),
                               wkv[:, :, MLA_NOPE:].reshape(kv_lora, heads * MLA_V)], axis=1).astype(BF16)

    def rope_gain(g):
        return jnp.concatenate([g[MLA_NOPE:], _rot_gain(g[MLA_NOPE:])]).reshape(1, 2 * MLA_ROPE)

    inv_freq = ROPE_THETA ** (-jnp.arange(0, MLA_ROPE, 2, dtype=F32) / MLA_ROPE)
    invf = jnp.tile(inv_freq, 4).reshape(1, LANES)
    nope_gain = (q_head_gain[:MLA_NOPE] * k_head_gain[:MLA_NOPE]).reshape(1, MLA_NOPE)
    tm = PROJ_TILE

    def row(n):
        return pl.BlockSpec((tm, n), lambda i: (i, 0))

    def head_major(n):
        return pl.BlockSpec((1, heads, tm, n), lambda i: (i // per_seq, 0, i % per_seq, 0))

    per_seq = seq // tm
    per_att = ATT_TILE // tm
    qt_spec = pl.BlockSpec((1, heads, 1, MLA_HEAD_PAD, tm),
                           lambda i: (i // per_seq, 0, (i % per_seq) // per_att, 0, i % per_att))
    return pl.pallas_call(
        functools.partial(_proj_body, layer=layer, heads=heads),
        grid=(t // tm,),
        in_specs=[row(d), row(1), _resident((d, cols)), _resident((d, 2 * d)), _resident((1, 2 * d)),
                  _resident(lb_table.shape), _resident((1, q_lora)), _resident(wq_ext.shape),
                  _resident((1, kv_lora)), _resident(wkv_ext.shape), _resident((1, LANES)),
                  _resident((1, MLA_NOPE)), _resident((1, LANES)), _resident((1, LANES))],
        out_specs=[row(hg_width), row(hg_width), row(hg_width), row(hg_width), row(hg_width), row(2 * d),
                   qt_spec, head_major(MLA_HEAD_PAD), head_major(MLA_V)],
        out_shape=[jax.ShapeDtypeStruct((t, hg_width), BF16),
                   jax.ShapeDtypeStruct((t, hg_width), F32),
                   jax.ShapeDtypeStruct((t, hg_width), BF16),
                   jax.ShapeDtypeStruct((t, hg_width), BF16),
                   jax.ShapeDtypeStruct((t, hg_width), BF16),
                   jax.ShapeDtypeStruct((t, 2 * d), BF16),
                   jax.ShapeDtypeStruct((t // seq, heads, seq // ATT_TILE, MLA_HEAD_PAD, ATT_TILE), BF16),
                   jax.ShapeDtypeStruct((t // seq, heads, seq, MLA_HEAD_PAD), BF16),
                   jax.ShapeDtypeStruct((t // seq, heads, seq, MLA_V), BF16)],
        compiler_params=_params(("parallel",)),
        name="proj",
    )(u, pos, w_in_ext, w_merge.astype(BF16), b_merge.reshape(1, 2 * d), lb_table, gq.reshape(1, q_lora), wq_ext,
      gkv.reshape(1, kv_lora), wkv_ext, rope_gain(q_head_gain), nope_gain, rope_gain(k_head_gain), invf)


def _split3(x):
    hi = x.astype(BF16)
    r = x - hi.astype(F32)
    mid = r.astype(BF16)
    lo = (r - mid.astype(F32)).astype(BF16)
    return hi, mid, lo


def _hgrn_body(q_ref, lf_ref, kk_ref, v_ref, g_ref, gain_ref, tri_ref, y_ref, state_ref):
    @pl.when(pl.program_id(2) == 0)
    def _():
        state_ref[...] = jnp.zeros_like(state_ref)

    n = CHUNK
    tb = q_ref.shape[1]
    chunks = [slice(c * n, (c + 1) * n) for c in range(tb // n)]
    trow = lax.broadcasted_iota(jnp.int32, (n, n), 0)
    tcol = lax.broadcasted_iota(jnp.int32, (n, n), 1)
    tri = tri_ref[...]
    halves = [1 << i for i in range(n.bit_length() - 1)]
    pairs = {half: (trow // (2 * half) == tcol // (2 * half)) & (trow % (2 * half) >= half)
             & (tcol % (2 * half) < half) for half in halves}

    def ref_rows(x, size, row):
        blocks = x.reshape(tb // size, size, HG_HEAD)[:, row:row + 1, :]
        return jnp.broadcast_to(blocks, (tb // size, size, HG_HEAD)).reshape(tb, HG_HEAD)

    r8 = lax.broadcasted_iota(jnp.int32, (1, 8, HG_HEAD), 1)

    def head(q, lf, k, v, g, s_ref):
        lf3 = lf.reshape(tb // 8, 8, HG_HEAD)
        split = jnp.concatenate(_split3(lf), axis=1)
        cums = []
        for sl in chunks:
            r = _dot(tri, split[sl])
            cums.append(r[:, :HG_HEAD] + r[:, HG_HEAD:2 * HG_HEAD] + r[:, 2 * HG_HEAD:])
        cum = jnp.concatenate(cums, axis=0)
        diag = jnp.sum(q * k, axis=-1, keepdims=True)
        scores = [jnp.where(trow == tcol, diag[sl], 0.0) for sl in chunks]
        for half in halves:
            size = 2 * half
            if half >= 4:
                diff = pltpu.bitcast(cum - ref_rows(cum, size, half - 1), jnp.uint32)
                dec = pltpu.bitcast(diff | jnp.uint32(0x80000000), F32)
            elif half == 2:
                dec = jnp.where(r8 % 4 == 3, lf3 + pltpu.roll(lf3, 1, 1),
                                jnp.where(r8 % 4 == 2, lf3,
                                          jnp.where(r8 % 4 == 0, pltpu.roll(lf3, 7, 1), 0.0))).reshape(tb, HG_HEAD)
            else:
                dec = jnp.where(r8 % 2 == 1, lf3, 0.0).reshape(tb, HG_HEAD)
            e = jnp.exp2(dec)
            qe = (q * e).astype(BF16)
            ke = (k * e).astype(BF16)
            for c, sl in enumerate(chunks):
                scores[c] = scores[c] + jnp.where(pairs[half], _dot_nt(qe[sl], ke[sl]), 0.0)
        last = ref_rows(cum, n, n - 1)
        qdec = (q * jnp.exp2(cum)).astype(BF16)
        kdec = (k * jnp.exp2(last - cum)).astype(BF16)
        updates = [_dot_tn(kdec[sl], v[sl]) for sl in chunks]
        lasts = jnp.concatenate([cum[sl][n - 1:n, :] for sl in chunks], axis=0)
        decay_cols = jnp.concatenate([jnp.exp2(lasts)] * (HG_HEAD // len(chunks)), axis=0).T
        state = s_ref[...]
        outs = []
        for c, sl in enumerate(chunks):
            lhs = jnp.concatenate([qdec[sl], scores[c].astype(BF16)], axis=1)
            rhs = jnp.concatenate([state.astype(BF16), v[sl]], axis=0)
            outs.append(_dot(lhs, rhs))
            state = state * decay_cols[:, c:c + 1] + updates[c]
        s_ref[...] = state
        return _rms(jnp.concatenate(outs, axis=0), gain_ref[...]) * g

    for hd in range(q_ref.shape[2] // HG_HEAD):
        lanes = slice(hd * HG_HEAD, (hd + 1) * HG_HEAD)
        y_ref[0, :, lanes] = head(q_ref[0, :, lanes].astype(F32), lf_ref[0, :, lanes],
                                  kk_ref[0, :, lanes].astype(F32), v_ref[0, :, lanes],
                                  g_ref[0, :, lanes].astype(F32), state_ref.at[hd]).astype(y_ref.dtype)


def _hgrn(hq, lf, kk, v, g, out_gain, *, batch, seq):
    width = hq.shape[1]
    heads = width // HG_HEAD
    tb = HG_BLOCK
    tri = jnp.asarray(np.tril(np.ones((CHUNK, CHUNK), np.float32)), BF16)

    def r3(a):
        return a.reshape(batch, seq, width)

    blk = pl.BlockSpec((1, tb, HG_HEADS_PER_STEP * HG_HEAD), lambda b, h, j: (b, j, h))
    y = pl.pallas_call(
        _hgrn_body,
        grid=(batch, heads // HG_HEADS_PER_STEP, seq // tb),
        in_specs=[blk, blk, blk, blk, blk, _resident((1, HG_HEAD)), _resident((CHUNK, CHUNK))],
        out_specs=blk,
        out_shape=jax.ShapeDtypeStruct((batch, seq, width), BF16),
        scratch_shapes=[pltpu.VMEM((HG_HEADS_PER_STEP, HG_HEAD, HG_HEAD), F32)],
        compiler_params=_params(("parallel", "parallel", "arbitrary")),
        name="hgrn",
    )(r3(hq), r3(lf), r3(kk), r3(v), r3(g), out_gain.reshape(1, HG_HEAD), tri)
    return y.reshape(batch * seq, width)


def _attn_body(q_ref, k_ref, v_ref, o_ref, vt_ref, s_ref, mx_ref, m_ref, acc_ref):
    seq = k_ref.shape[2]
    tk = ATT_TILE
    tqs = ATT_TILE // ATT_SPLIT
    n_tiles = seq // tk

    def transpose_v(j, _):
        vt_ref[j, :MLA_V] = v_ref[0, 0, pl.ds(pl.multiple_of(j * tk, tk), tk), :].astype(F32).T.astype(BF16)
        vt_ref[j, MLA_V:] = jnp.ones((ATT_ONES, tk), BF16)
        return 0

    lax.fori_loop(0, n_tiles, transpose_v, 0)

    key_chunk = lax.broadcasted_iota(jnp.int32, (tk, tqs), 0) // CHUNK
    qry_chunk = lax.broadcasted_iota(jnp.int32, (tk, tqs), 1) // CHUNK
    allowed = [key_chunk <= qry_chunk + (sub * tqs) // CHUNK for sub in range(ATT_SPLIT)]

    def fold_rows(x, op):
        rows, cols = x.shape
        x = x.reshape(rows // ATT_FOLD, ATT_FOLD, cols)
        out = x[0]
        for i in range(1, rows // ATT_FOLD):
            out = op(out, x[i])
        return out

    def aligned(x, m):
        return x if isinstance(x, int) else pl.multiple_of(x, m)

    def scores(qt, kj, buf):
        k_tile = k_ref[0, 0, pl.ds(aligned(kj * tk, tk), tk), :]
        s = _dot(k_tile, q_ref[0, 0, qt])
        s_ref[buf] = s
        mx_ref[buf] = jnp.max(fold_rows(s, jnp.maximum), axis=0, keepdims=True)

    def start():
        m_ref[...] = jnp.full(m_ref.shape, -jnp.inf, F32)
        acc_ref[...] = jnp.zeros(acc_ref.shape, F32)

    def finish(base):
        for sub in range(ATT_SPLIT):
            acc = acc_ref[sub]
            o_ref[0, pl.ds(aligned(base + sub * tqs, tqs), tqs), :] = (
                (acc[:MLA_V] / acc[MLA_V:MLA_V + 1]).T.astype(o_ref.dtype))

    def absorb(kj, buf, masked):
        for sub in range(ATT_SPLIT):
            cols = slice(sub * tqs, (sub + 1) * tqs)
            keys = (sub + 1) * tqs if masked else tk
            vt_tile = vt_ref[kj, :, :keys]
            ss = s_ref[buf, :keys, cols]
            if masked:
                ss = jnp.where(allowed[sub][:keys], ss, -jnp.inf)
                tile_max = jnp.max(fold_rows(ss, jnp.maximum), axis=0, keepdims=True)
            else:
                tile_max = mx_ref[buf, :, cols]
            m = m_ref[sub]
            m_new = jnp.maximum(m, tile_max)
            alpha = jnp.exp2(m - m_new)
            p = jnp.exp2(ss - m_new)
            m_ref[sub] = m_new
            acc_ref[sub] = alpha * acc_ref[sub] + _dot(vt_tile, p.astype(BF16))

    def run(qi, next_q, first, count, last_masked):
        for j in range(count):
            if last_masked and j + 1 == count:
                scores(next_q, 0, 2)
            else:
                scores(qi, first + j + 1, j % 2)
            absorb(first + j, (j + 1) % 2, last_masked and j + 1 == count)

    def q_step(qi, _):
        next_q = jnp.minimum(qi + 1, n_tiles - 1)
        start()
        scores(qi, 1, 1)
        absorb(0, 2, False)
        full = qi - 1

        def group(g, _):
            run(qi, next_q, 1 + g * ATT_UNROLL, ATT_UNROLL, False)
            return 0

        lax.fori_loop(0, full // ATT_UNROLL, group, 0)
        tail_first = 1 + (full // ATT_UNROLL) * ATT_UNROLL
        for rem in range(ATT_UNROLL):
            @pl.when(full % ATT_UNROLL == rem)
            def _(rem=rem):
                run(qi, next_q, tail_first, rem + 1, True)

        finish(pl.multiple_of(qi * tk, tk))
        return 0

    start()
    scores(0, 0, 0)
    scores(min(1, n_tiles - 1), 0, 2)
    absorb(0, 0, True)
    finish(0)
    lax.fori_loop(1, n_tiles, q_step, 0)


def _attention(q, k, v, *, batch, seq, heads):
    qt = pl.BlockSpec((1, 1, seq // ATT_TILE, MLA_HEAD_PAD, ATT_TILE), lambda b, h: (b, h, 0, 0, 0))
    kk = pl.BlockSpec((1, 1, seq, MLA_HEAD_PAD), lambda b, h: (b, h, 0, 0))
    vv = pl.BlockSpec((1, 1, seq, MLA_V), lambda b, h: (b, h, 0, 0))
    vo = pl.BlockSpec((1, seq, MLA_V), lambda b, h: (b, 0, h))
    o = pl.pallas_call(
        _attn_body,
        grid=(batch, heads),
        in_specs=[qt, kk, vv],
        out_specs=vo,
        out_shape=jax.ShapeDtypeStruct((batch, seq, heads * MLA_V), BF16),
        scratch_shapes=[pltpu.VMEM((seq // ATT_TILE, MLA_V + ATT_ONES, ATT_TILE), BF16),
                        pltpu.VMEM((3, ATT_TILE, ATT_TILE), F32),
                        pltpu.VMEM((3, 1, ATT_TILE), F32),
                        pltpu.VMEM((ATT_SPLIT, 1, ATT_TILE // ATT_SPLIT), F32),
                        pltpu.VMEM((ATT_SPLIT, MLA_V + ATT_ONES, ATT_TILE // ATT_SPLIT), F32)],
        compiler_params=_params(("parallel", "parallel")),
        name="attn",
    )(q, k, v)
    return o.reshape(batch * seq, heads * MLA_V)


def _merge_body(yh_ref, ym_ref, gate_ref, h_ref, wh_ref, wm_ref, wo_ref, o_ref):
    d = h_ref.shape[1]
    mix = (gate_ref[:, :d].astype(F32) * _dot(yh_ref[...], wh_ref[...])
           + gate_ref[:, d:].astype(F32) * _dot(ym_ref[...], wm_ref[...]))
    o_ref[...] = h_ref[...] + _dot(mix.astype(BF16), wo_ref[...])


def _merge(yh, ym, gates, h, w_h, w_m, w_o):
    t, d = h.shape
    tm = TOKEN_TILE

    def row(n):
        return pl.BlockSpec((tm, n), lambda i: (i, 0))

    return pl.pallas_call(
        _merge_body,
        grid=(t // tm,),
        in_specs=[row(yh.shape[1]), row(ym.shape[1]), row(2 * d), row(d), _resident(w_h.shape),
                  _resident(w_m.shape), _resident(w_o.shape)],
        out_specs=row(d),
        out_shape=jax.ShapeDtypeStruct((t, d), F32),
        compiler_params=_params(("parallel",)),
        name="merge",
    )(yh, ym, gates, h, w_h.astype(BF16), w_m.astype(BF16), w_o.astype(BF16))


def kernel(x, positions, ffn1_norm, ffn1_w_in, ffn1_w_out, mix_norm, w_in, hg_lb_table, hg_out_norm, w_hg_branch, mla_q_lora_norm, w_q_up, mla_kv_lora_norm, w_kv_up, q_head_norm, k_head_norm, w_mla_branch, w_merge, b_merge, w_out, ffn2_norm, ffn2_w_in, ffn2_w_out, final_norm):
    batch, seq, d = x.shape
    depth = ffn1_norm.shape[0]
    hg_width = hg_lb_table.shape[1]
    heads = w_q_up.shape[2] // MLA_QK
    assert hg_out_norm.shape[1] == HG_HEAD and w_hg_branch.shape[1] == hg_width
    assert q_head_norm.shape[1] == MLA_QK and w_kv_up.shape[2] == heads * (MLA_NOPE + MLA_V)
    assert seq % max(HG_BLOCK, ATT_TILE) == 0 and (batch * seq) % TOKEN_TILE == 0
    t = batch * seq
    pos = positions.reshape(t, 1)
    h = x.reshape(t, d)
    for l in range(depth):
        h1, u = _ffn(h, ffn1_norm[l], ffn1_w_in[l], ffn1_w_out[l], mix_norm[l], emit_h=True, norm_dtype=BF16)
        hq, lf, kk, hv, hg, gates, q, k, v = _proj(
            u, pos, w_in[l], w_merge[l], b_merge[l], hg_lb_table, mla_q_lora_norm[l], w_q_up[l],
            mla_kv_lora_norm[l], w_kv_up[l], q_head_norm[l], k_head_norm[l], layer=l, hg_width=hg_width,
            heads=heads, seq=seq)
        y_hg = _hgrn(hq, lf, kk, hv, hg, hg_out_norm[l], batch=batch, seq=seq)
        y_mla = _attention(q, k, v, batch=batch, seq=seq, heads=heads)
        h2 = _merge(y_hg, y_mla, gates, h1, w_hg_branch[l], w_mla_branch[l], w_out[l])
        (h,) = _ffn(h2, ffn2_norm[l], ffn2_w_in[l], ffn2_w_out[l], final_norm[l], emit_h=False, norm_dtype=F32)
    return h.reshape(batch, seq, d)
```

```python
import functools

import numpy as np
import jax
import jax.numpy as jnp
from jax import lax
from jax.experimental import pallas as pl
from jax.experimental.pallas import tpu as pltpu

F32 = jnp.float32
BF16 = jnp.bfloat16

EPS = 1e-6
CHUNK = 64
HG_HEAD = 128
MLA_NOPE = 128
MLA_ROPE = 64
MLA_V = 128
MLA_QK = MLA_NOPE + MLA_ROPE
MLA_HEAD_PAD = 256
ROPE_THETA = 10000.0
LOG2E = float(np.log2(np.e))

LANES = 128
TOKEN_TILE = 512
FFN_TILE = 1024
PROJ_TILE = 256
FF_CHUNK = 512
HG_BLOCK = 512
HG_HEADS_PER_STEP = 4
ATT_TILE = 512
ATT_KEYS = 2 * ATT_TILE
ATT_SPLIT = 2
ATT_UNROLL = 4
ATT_FOLD = 32
ATT_ONES = 16
VMEM_LIMIT = 56 * 1024 * 1024


def _rms(x, gain):
    return x * lax.rsqrt(jnp.mean(x * x, axis=-1, keepdims=True) + EPS) * gain


def _silu(x):
    return x * jax.nn.sigmoid(x)


def _dot(a, b):
    return jnp.dot(a, b, preferred_element_type=F32)


def _dot_nt(a, b):
    return lax.dot_general(a, b, (((1,), (1,)), ((), ())), preferred_element_type=F32)


def _dot_tn(a, b):
    return lax.dot_general(a, b, (((0,), (0,)), ((), ())), preferred_element_type=F32)


def _resident(shape):
    zeros = (0,) * len(shape)
    return pl.BlockSpec(shape, lambda *_: zeros, pipeline_mode=pl.Buffered(1))


def _params(semantics):
    return pltpu.CompilerParams(dimension_semantics=semantics, vmem_limit_bytes=VMEM_LIMIT)


def _ffn_body(x_ref, gin_ref, win_ref, wout_ref, gout_ref, *rest, emit_h):
    if emit_h:
        h_ref, n_ref, a_ref = rest
    else:
        n_ref, a_ref = rest
    d_ff = wout_ref.shape[0]
    x = x_ref[...]
    xn = _rms(x, gin_ref[...]).astype(BF16)
    for lo in range(0, d_ff, FF_CHUNK):
        hi = min(lo + FF_CHUNK, d_ff)
        gate = _dot(xn, win_ref[:, lo:hi])
        up = _dot(xn, win_ref[:, d_ff + lo:d_ff + hi])
        a_ref[:, lo:hi] = (_silu(gate) * up).astype(BF16)
    h = x + 0.5 * _dot(a_ref[...], wout_ref[...])
    if emit_h:
        h_ref[...] = h
    n_ref[...] = _rms(h, gout_ref[...]).astype(n_ref.dtype)


def _ffn(x, gin, w_in, w_out, gout, *, emit_h, norm_dtype):
    t, d = x.shape
    d_ff = w_out.shape[0]
    win = w_in.astype(BF16)
    tm = FFN_TILE
    row = pl.BlockSpec((tm, d), lambda i: (i, 0))
    out_shape = [jax.ShapeDtypeStruct((t, d), norm_dtype)]
    out_specs = [row]
    if emit_h:
        out_shape = [jax.ShapeDtypeStruct((t, d), F32)] + out_shape
        out_specs = [row] + out_specs
    return pl.pallas_call(
        functools.partial(_ffn_body, emit_h=emit_h),
        grid=(t // tm,),
        in_specs=[row, _resident((1, d)), _resident(win.shape), _resident((d_ff, d)), _resident((1, d))],
        out_specs=out_specs,
        out_shape=out_shape,
        scratch_shapes=[pltpu.VMEM((tm, d_ff), BF16)],
        compiler_params=_params(("parallel",)),
        name="ffn1" if emit_h else "ffn2",
    )(x, gin.reshape(1, d), win, w_out.astype(BF16), gout.reshape(1, d))


def _proj_body(u_ref, pos_ref, w_ref, wm_ref, bm_ref, tab_ref, gq_ref, wq_ref, gkv_ref, wkv_ref, qr_gain_ref,
               kn_gain_ref, kr_gain_ref, invf_ref,
               hq_ref, lf_ref, kk_ref, hv_ref, hg_ref, gate_ref, q_ref, k_ref, v_ref, *, layer, heads):
    w = hq_ref.shape[1]
    tm, d = u_ref.shape
    q_lora, kv_lora = wq_ref.shape[0], wkv_ref.shape[0]
    u = u_ref[...]

    c = _dot(u, w_ref[:, 4 * w:])
    hq_ref[...] = _silu(_dot(u, w_ref[:, 0:w])).astype(hq_ref.dtype)
    qraw = _dot(_rms(c[:, :q_lora], gq_ref[...]).astype(BF16), wq_ref[...])
    kvraw = _dot(_rms(c[:, q_lora:q_lora + kv_lora], gkv_ref[...]).astype(BF16), wkv_ref[...])
    kpe = c[:, q_lora + kv_lora:]

    tab = tab_ref[...]
    e = jnp.exp(tab - jnp.max(tab, axis=0, keepdims=True))
    lb = jnp.sum(e[:layer + 1], axis=0, keepdims=True) / jnp.sum(e, axis=0, keepdims=True)
    z = _dot(u, w_ref[:, w:2 * w])
    ez = jnp.exp(-jnp.abs(z))
    r = 1.0 / (1.0 + ez)
    a = ez * r
    pos = z >= 0
    lf_ref[...] = jnp.log(lb + (1.0 - lb) * jnp.where(pos, r, a)) * LOG2E
    kk_ref[...] = ((1.0 - lb) * jnp.where(pos, a, r)).astype(BF16)
    hv_ref[...] = _dot(u, w_ref[:, 2 * w:3 * w]).astype(BF16)
    hg_ref[...] = _silu(_dot(u, w_ref[:, 3 * w:4 * w])).astype(BF16)
    for j in range(2):
        gm = _dot(u, wm_ref[:, j * d:(j + 1) * d]) + bm_ref[:, j * d:(j + 1) * d]
        gate_ref[:, j * d:(j + 1) * d] = jax.nn.sigmoid(gm).astype(BF16)

    half = MLA_ROPE // 2
    rows4 = tm // 4
    lane4 = lax.broadcasted_iota(jnp.int32, (rows4, LANES), 1)
    posf = pos_ref[...].astype(F32)
    pos4 = [posf[g * rows4:(g + 1) * rows4] for g in range(4)]
    packed = jnp.where(lane4 < half, pos4[0],
                       jnp.where(lane4 < 2 * half, pos4[1],
                                 jnp.where(lane4 < 3 * half, pos4[2], pos4[3]))) * invf_ref[...]
    cos4, sin4 = jnp.cos(packed), jnp.sin(packed)
    spread = []
    for g in range(4):
        cg = pltpu.roll(cos4, LANES - half * g, 1) if g else cos4
        sg = pltpu.roll(sin4, LANES - half * g, 1) if g else sin4
        cc = jnp.where(lane4 < half, cg, pltpu.roll(cg, half, 1))
        ss = jnp.where(lane4 < 3 * half, pltpu.roll(sg, 2 * half, 1), pltpu.roll(sg, 3 * half, 1))
        spread.append(jnp.where(lane4 < 2 * half, cc, ss))
    cs = jnp.concatenate(spread, axis=0)
    first = lax.broadcasted_iota(jnp.int32, (tm, LANES), 1) < MLA_ROPE

    def rope(block, gain_cs):
        t = block * gain_cs
        return t + pltpu.roll(t, MLA_ROPE, 1)

    def sumsq(x):
        return jnp.sum(x * x, axis=-1, keepdims=True)

    k_rope = rope(kpe, kr_gain_ref[...] * cs)
    k_pe_ss = 0.5 * sumsq(kpe)
    q_gain_cs = qr_gain_ref[...] * cs
    scale = MLA_QK ** -0.5 * LOG2E
    nope_w = heads * MLA_NOPE
    for h in range(heads):
        lo, hi = h * MLA_NOPE, (h + 1) * MLA_NOPE
        qn = qraw[:, lo:hi]
        qr = qraw[:, nope_w + lo:nope_w + hi]
        sq = lax.rsqrt((sumsq(qn) + 0.5 * sumsq(qr)) / MLA_QK + EPS) * scale
        o = h * MLA_HEAD_PAD
        q_head = jnp.concatenate([qn * sq, rope(qr, q_gain_cs) * sq], axis=1)
        q_ref[0, h, 0] = q_head.T.astype(BF16)
        kn = kvraw[:, lo:hi]
        sk = lax.rsqrt((sumsq(kn) + k_pe_ss) / MLA_QK + EPS)
        k_ref[0, h, :, :MLA_NOPE] = (kn * sk * kn_gain_ref[...]).astype(BF16)
        k_ref[0, h, :, MLA_NOPE:] = jnp.where(first, k_rope * sk, 0.0).astype(BF16)
        v_ref[0, h] = kvraw[:, nope_w + h * MLA_V:nope_w + (h + 1) * MLA_V].astype(BF16)


def _rot_cols(w):
    half = w.shape[-1] // 2
    return jnp.concatenate([-w[..., half:], w[..., :half]], axis=-1)


def _rot_gain(g):
    half = g.shape[-1] // 2
    return jnp.concatenate([g[..., half:], g[..., :half]], axis=-1)


def _proj(u, pos, w_in, w_merge, b_merge, lb_table, gq, w_q_up, gkv, w_kv_up, q_head_gain, k_head_gain,
          *, layer, hg_width, heads, seq):
    t, d = u.shape
    q_lora, kv_lora = w_q_up.shape[0], w_kv_up.shape[0]
    w_in_ext = jnp.concatenate([w_in, _rot_cols(w_in[:, -MLA_ROPE:])], axis=1).astype(BF16)
    cols = w_in_ext.shape[1]
    assert cols == 4 * hg_width + q_lora + kv_lora + 2 * MLA_ROPE
    wq = w_q_up.reshape(q_lora, heads, MLA_QK)
    wq_rope = wq[:, :, MLA_NOPE:]
    wq_ext = jnp.concatenate(
        [wq[:, :, :MLA_NOPE].reshape(q_lora, heads * MLA_NOPE),
         jnp.concatenate([wq_rope, _rot_cols(wq_rope)], axis=-1).reshape(q_lora, heads * 2 * MLA_ROPE)],
        axis=1).astype(BF16)
    wkv = w_kv_up.reshape(kv_lora, heads, MLA_NOPE + MLA_V)
    wkv_ext = jnp.concatenate([wkv[:, :, :MLA_NOPE].reshape(kv_lora, heads * MLA_NOPE),
                               wkv[:, :, MLA_NOPE:].reshape(kv_lora, heads * MLA_V)], axis=1).astype(BF16)

    def rope_gain(g):
        return jnp.concatenate([g[MLA_NOPE:], _rot_gain(g[MLA_NOPE:])]).reshape(1, 2 * MLA_ROPE)

    inv_freq = ROPE_THETA ** (-jnp.arange(0, MLA_ROPE, 2, dtype=F32) / MLA_ROPE)
    invf = jnp.tile(inv_freq, 4).reshape(1, LANES)
    nope_gain = (q_head_gain[:MLA_NOPE] * k_head_gain[:MLA_NOPE]).reshape(1, MLA_NOPE)
    tm = PROJ_TILE

    def row(n):
        return pl.BlockSpec((tm, n), lambda i: (i, 0))

    def head_major(n):
        return pl.BlockSpec((1, heads, tm, n), lambda i: (i // per_seq, 0, i % per_seq, 0))

    per_seq = seq // tm
    per_att = ATT_TILE // tm
    qt_spec = pl.BlockSpec((1, heads, 1, MLA_HEAD_PAD, tm),
                           lambda i: (i // per_seq, 0, (i % per_seq) // per_att, 0, i % per_att))
    return pl.pallas_call(
        functools.partial(_proj_body, layer=layer, heads=heads),
        grid=(t // tm,),
        in_specs=[row(d), row(1), _resident((d, cols)), _resident((d, 2 * d)), _resident((1, 2 * d)),
                  _resident(lb_table.shape), _resident((1, q_lora)), _resident(wq_ext.shape),
                  _resident((1, kv_lora)), _resident(wkv_ext.shape), _resident((1, LANES)),
                  _resident((1, MLA_NOPE)), _resident((1, LANES)), _resident((1, LANES))],
        out_specs=[row(hg_width), row(hg_width), row(hg_width), row(hg_width), row(hg_width), row(2 * d),
                   qt_spec, head_major(MLA_HEAD_PAD), head_major(MLA_V)],
        out_shape=[jax.ShapeDtypeStruct((t, hg_width), BF16),
                   jax.ShapeDtypeStruct((t, hg_width), F32),
                   jax.ShapeDtypeStruct((t, hg_width), BF16),
                   jax.ShapeDtypeStruct((t, hg_width), BF16),
                   jax.ShapeDtypeStruct((t, hg_width), BF16),
                   jax.ShapeDtypeStruct((t, 2 * d), BF16),
                   jax.ShapeDtypeStruct((t // seq, heads, seq // ATT_TILE, MLA_HEAD_PAD, ATT_TILE), BF16),
                   jax.ShapeDtypeStruct((t // seq, heads, seq, MLA_HEAD_PAD), BF16),
                   jax.ShapeDtypeStruct((t // seq, heads, seq, MLA_V), BF16)],
        compiler_params=_params(("parallel",)),
        name="proj",
    )(u, pos, w_in_ext, w_merge.astype(BF16), b_merge.reshape(1, 2 * d), lb_table, gq.reshape(1, q_lora), wq_ext,
      gkv.reshape(1, kv_lora), wkv_ext, rope_gain(q_head_gain), nope_gain, rope_gain(k_head_gain), invf)


def _split3(x):
    hi = x.astype(BF16)
    r = x - hi.astype(F32)
    mid = r.astype(BF16)
    lo = (r - mid.astype(F32)).astype(BF16)
    return hi, mid, lo


def _hgrn_body(q_ref, lf_ref, kk_ref, v_ref, g_ref, gain_ref, tri_ref, y_ref, state_ref):
    @pl.when(pl.program_id(2) == 0)
    def _():
        state_ref[...] = jnp.zeros_like(state_ref)

    n = CHUNK
    tb = q_ref.shape[1]
    chunks = [slice(c * n, (c + 1) * n) for c in range(tb // n)]
    trow = lax.broadcasted_iota(jnp.int32, (n, n), 0)
    tcol = lax.broadcasted_iota(jnp.int32, (n, n), 1)
    tri = tri_ref[...]
    halves = [1 << i for i in range(n.bit_length() - 1)]
    pairs = {half: (trow // (2 * half) == tcol // (2 * half)) & (trow % (2 * half) >= half)
             & (tcol % (2 * half) < half) for half in halves}

    def ref_rows(x, size, row):
        blocks = x.reshape(tb // size, size, HG_HEAD)[:, row:row + 1, :]
        return jnp.broadcast_to(blocks, (tb // size, size, HG_HEAD)).reshape(tb, HG_HEAD)

    r8 = lax.broadcasted_iota(jnp.int32, (1, 8, HG_HEAD), 1)

    def head(q, lf, k, v, g, s_ref):
        lf3 = lf.reshape(tb // 8, 8, HG_HEAD)
        split = jnp.concatenate(_split3(lf), axis=1)
        cums = []
        for sl in chunks:
            r = _dot(tri, split[sl])
            cums.append(r[:, :HG_HEAD] + r[:, HG_HEAD:2 * HG_HEAD] + r[:, 2 * HG_HEAD:])
        cum = jnp.concatenate(cums, axis=0)
        diag = jnp.sum(q * k, axis=-1, keepdims=True)
        scores = [jnp.where(trow == tcol, diag[sl], 0.0) for sl in chunks]
        for half in halves:
            size = 2 * half
            if half >= 4:
                diff = pltpu.bitcast(cum - ref_rows(cum, size, half - 1), jnp.uint32)
                dec = pltpu.bitcast(diff | jnp.uint32(0x80000000), F32)
            elif half == 2:
                dec = jnp.where(r8 % 4 == 3, lf3 + pltpu.roll(lf3, 1, 1),
                                jnp.where(r8 % 4 == 2, lf3,
                                          jnp.where(r8 % 4 == 0, pltpu.roll(lf3, 7, 1), 0.0))).reshape(tb, HG_HEAD)
            else:
                dec = jnp.where(r8 % 2 == 1, lf3, 0.0).reshape(tb, HG_HEAD)
            e = jnp.exp2(dec)
            qe = (q * e).astype(BF16)
            ke = (k * e).astype(BF16)
            for c, sl in enumerate(chunks):
                scores[c] = scores[c] + jnp.where(pairs[half], _dot_nt(qe[sl], ke[sl]), 0.0)
        last = ref_rows(cum, n, n - 1)
        qdec = (q * jnp.exp2(cum)).astype(BF16)
        kdec = (k * jnp.exp2(last - cum)).astype(BF16)
        updates = [_dot_tn(kdec[sl], v[sl]) for sl in chunks]
        lasts = jnp.concatenate([cum[sl][n - 1:n, :] for sl in chunks], axis=0)
        decay_cols = jnp.concatenate([jnp.exp2(lasts)] * (HG_HEAD // len(chunks)), axis=0).T
        state = s_ref[...]
        outs = []
        for c, sl in enumerate(chunks):
            lhs = jnp.concatenate([qdec[sl], scores[c].astype(BF16)], axis=1)
            rhs = jnp.concatenate([state.astype(BF16), v[sl]], axis=0)
            outs.append(_dot(lhs, rhs))
            state = state * decay_cols[:, c:c + 1] + updates[c]
        s_ref[...] = state
        return _rms(jnp.concatenate(outs, axis=0), gain_ref[...]) * g

    for hd in range(q_ref.shape[2] // HG_HEAD):
        lanes = slice(hd * HG_HEAD, (hd + 1) * HG_HEAD)
        y_ref[0, :, lanes] = head(q_ref[0, :, lanes].astype(F32), lf_ref[0, :, lanes],
                                  kk_ref[0, :, lanes].astype(F32), v_ref[0, :, lanes],
                                  g_ref[0, :, lanes].astype(F32), state_ref.at[hd]).astype(y_ref.dtype)


def _hgrn(hq, lf, kk, v, g, out_gain, *, batch, seq):
    width = hq.shape[1]
    heads = width // HG_HEAD
    tb = HG_BLOCK
    tri = jnp.asarray(np.tril(np.ones((CHUNK, CHUNK), np.float32)), BF16)

    def r3(a):
        return a.reshape(batch, seq, width)

    blk = pl.BlockSpec((1, tb, HG_HEADS_PER_STEP * HG_HEAD), lambda b, h, j: (b, j, h))
    y = pl.pallas_call(
        _hgrn_body,
        grid=(batch, heads // HG_HEADS_PER_STEP, seq // tb),
        in_specs=[blk, blk, blk, blk, blk, _resident((1, HG_HEAD)), _resident((CHUNK, CHUNK))],
        out_specs=blk,
        out_shape=jax.ShapeDtypeStruct((batch, seq, width), BF16),
        scratch_shapes=[pltpu.VMEM((HG_HEADS_PER_STEP, HG_HEAD, HG_HEAD), F32)],
        compiler_params=_params(("parallel", "parallel", "arbitrary")),
        name="hgrn",
    )(r3(hq), r3(lf), r3(kk), r3(v), r3(g), out_gain.reshape(1, HG_HEAD), tri)
    return y.reshape(batch * seq, width)


def _attn_body(q_ref, k_ref, v_ref, o_ref, vt_ref, s_ref, mx_ref, m_ref, acc_ref):
    seq = k_ref.shape[2]
    tq = ATT_TILE
    tk = ATT_KEYS
    tqs = tq // ATT_SPLIT
    n_q = seq // tq
    n_k = seq // tk

    def transpose_v(j, _):
        vt_ref[j, :MLA_V] = v_ref[0, 0, pl.ds(pl.multiple_of(j * tk, tk), tk), :].astype(F32).T.astype(BF16)
        vt_ref[j, MLA_V:] = jnp.ones((ATT_ONES, tk), BF16)
        return 0

    lax.fori_loop(0, n_k, transpose_v, 0)

    key_chunk = lax.broadcasted_iota(jnp.int32, (tk, tqs), 0) // CHUNK
    qry_chunk = lax.broadcasted_iota(jnp.int32, (tk, tqs), 1) // CHUNK
    first_query = {"half": 0, "whole": tq}
    allowed = {kind: [key_chunk <= qry_chunk + (off + sub * tqs) // CHUNK for sub in range(ATT_SPLIT)]
               for kind, off in first_query.items()}

    def fold_rows(x, op):
        rows, cols = x.shape
        x = x.reshape(rows // ATT_FOLD, ATT_FOLD, cols)
        out = x[0]
        for i in range(1, rows // ATT_FOLD):
            out = op(out, x[i])
        return out

    def aligned(x, m):
        return x if isinstance(x, int) else pl.multiple_of(x, m)

    def scores(qt, kj, buf):
        k_tile = k_ref[0, 0, pl.ds(aligned(kj * tk, tk), tk), :]
        s = _dot(k_tile, q_ref[0, 0, qt])
        s_ref[buf] = s
        mx_ref[buf] = jnp.max(fold_rows(s, jnp.maximum), axis=0, keepdims=True)

    def start():
        m_ref[...] = jnp.full(m_ref.shape, -jnp.inf, F32)
        acc_ref[...] = jnp.zeros(acc_ref.shape, F32)

    def finish(base):
        for sub in range(ATT_SPLIT):
            acc = acc_ref[sub]
            o_ref[0, pl.ds(aligned(base + sub * tqs, tqs), tqs), :] = (
                (acc[:MLA_V] / acc[MLA_V:MLA_V + 1]).T.astype(o_ref.dtype))

    def absorb(kj, buf, kind):
        for sub in range(ATT_SPLIT):
            cols = slice(sub * tqs, (sub + 1) * tqs)
            keys = tk if kind == "full" else first_query[kind] + (sub + 1) * tqs
            vt_tile = vt_ref[kj, :, :keys]
            ss = s_ref[buf, :keys, cols]
            if kind == "full":
                tile_max = mx_ref[buf, :, cols]
            else:
                ss = jnp.where(allowed[kind][sub][:keys], ss, -jnp.inf)
                tile_max = jnp.max(fold_rows(ss, jnp.maximum), axis=0, keepdims=True)
            m = m_ref[sub]
            m_new = jnp.maximum(m, tile_max)
            alpha = jnp.exp2(m - m_new)
            p = jnp.exp2(ss - m_new)
            m_ref[sub] = m_new
            acc_ref[sub] = alpha * acc_ref[sub] + _dot(vt_tile, p.astype(BF16))

    def run(qi, next_q, first, count, diagonal):
        for j in range(count):
            last = diagonal is not None and j + 1 == count
            if last:
                scores(next_q, 0, 2)
            else:
                scores(qi, first + j + 1, j % 2)
            absorb(first + j, (j + 1) % 2, diagonal if last else "full")

    def q_step(qi, _):
        next_q = jnp.minimum(qi + 1, n_q - 1)
        start()
        scores(qi, 1, 1)
        absorb(0, 2, "full")
        full = qi // 2 - 1

        def group(g, _):
            run(qi, next_q, 1 + g * ATT_UNROLL, ATT_UNROLL, None)
            return 0

        lax.fori_loop(0, full // ATT_UNROLL, group, 0)
        tail_first = 1 + (full // ATT_UNROLL) * ATT_UNROLL
        for rem in range(ATT_UNROLL):
            for odd, kind in enumerate(("half", "whole")):
                @pl.when((full % ATT_UNROLL == rem) & (qi % 2 == odd))
                def _(rem=rem, kind=kind):
                    run(qi, next_q, tail_first, rem + 1, kind)

        finish(pl.multiple_of(qi * tq, tq))
        return 0

    start()
    scores(0, 0, 1)
    scores(1, 0, 0)
    absorb(0, 1, "half")
    finish(0)
    start()
    scores(min(2, n_q - 1), 0, 2)
    absorb(0, 0, "whole")
    finish(tq)
    lax.fori_loop(2, n_q, q_step, 0)


def _attention(q, k, v, *, batch, seq, heads):
    qt = pl.BlockSpec((1, 1, seq // ATT_TILE, MLA_HEAD_PAD, ATT_TILE), lambda b, h: (b, h, 0, 0, 0))
    kk = pl.BlockSpec((1, 1, seq, MLA_HEAD_PAD), lambda b, h: (b, h, 0, 0))
    vv = pl.BlockSpec((1, 1, seq, MLA_V), lambda b, h: (b, h, 0, 0))
    vo = pl.BlockSpec((1, seq, MLA_V), lambda b, h: (b, 0, h))
    o = pl.pallas_call(
        _attn_body,
        grid=(batch, heads),
        in_specs=[qt, kk, vv],
        out_specs=vo,
        out_shape=jax.ShapeDtypeStruct((batch, seq, heads * MLA_V), BF16),
        scratch_shapes=[pltpu.VMEM((seq // ATT_KEYS, MLA_V + ATT_ONES, ATT_KEYS), BF16),
                        pltpu.VMEM((3, ATT_KEYS, ATT_TILE), F32),
                        pltpu.VMEM((3, 1, ATT_TILE), F32),
                        pltpu.VMEM((ATT_SPLIT, 1, ATT_TILE // ATT_SPLIT), F32),
                        pltpu.VMEM((ATT_SPLIT, MLA_V + ATT_ONES, ATT_TILE // ATT_SPLIT), F32)],
        compiler_params=_params(("parallel", "parallel")),
        name="attn",
    )(q, k, v)
    return o.reshape(batch * seq, heads * MLA_V)


def _merge_body(yh_ref, ym_ref, gate_ref, h_ref, wh_ref, wm_ref, wo_ref, o_ref):
    d = h_ref.shape[1]
    mix = (gate_ref[:, :d].astype(F32) * _dot(yh_ref[...], wh_ref[...])
           + gate_ref[:, d:].astype(F32) * _dot(ym_ref[...], wm_ref[...]))
    o_ref[...] = h_ref[...] + _dot(mix.astype(BF16), wo_ref[...])


def _merge(yh, ym, gates, h, w_h, w_m, w_o):
    t, d = h.shape
    tm = TOKEN_TILE

    def row(n):
        return pl.BlockSpec((tm, n), lambda i: (i, 0))

    return pl.pallas_call(
        _merge_body,
        grid=(t // tm,),
        in_specs=[row(yh.shape[1]), row(ym.shape[1]), row(2 * d), row(d), _resident(w_h.shape),
                  _resident(w_m.shape), _resident(w_o.shape)],
        out_specs=row(d),
        out_shape=jax.ShapeDtypeStruct((t, d), F32),
        compiler_params=_params(("parallel",)),
        name="merge",
    )(yh, ym, gates, h, w_h.astype(BF16), w_m.astype(BF16), w_o.astype(BF16))


def kernel(x, positions, ffn1_norm, ffn1_w_in, ffn1_w_out, mix_norm, w_in, hg_lb_table, hg_out_norm, w_hg_branch, mla_q_lora_norm, w_q_up, mla_kv_lora_norm, w_kv_up, q_head_norm, k_head_norm, w_mla_branch, w_merge, b_merge, w_out, ffn2_norm, ffn2_w_in, ffn2_w_out, final_norm):
    batch, seq, d = x.shape
    depth = ffn1_norm.shape[0]
    hg_width = hg_lb_table.shape[1]
    heads = w_q_up.shape[2] // MLA_QK
    assert hg_out_norm.shape[1] == HG_HEAD and w_hg_branch.shape[1] == hg_width
    assert q_head_norm.shape[1] == MLA_QK and w_kv_up.shape[2] == heads * (MLA_NOPE + MLA_V)
    assert seq % max(HG_BLOCK, ATT_TILE) == 0 and (batch * seq) % max(TOKEN_TILE, FFN_TILE) == 0
    t = batch * seq
    pos = positions.reshape(t, 1)
    h = x.reshape(t, d)
    for l in range(depth):
        h1, u = _ffn(h, ffn1_norm[l], ffn1_w_in[l], ffn1_w_out[l], mix_norm[l], emit_h=True, norm_dtype=BF16)
        hq, lf, kk, hv, hg, gates, q, k, v = _proj(
            u, pos, w_in[l], w_merge[l], b_merge[l], hg_lb_table, mla_q_lora_norm[l], w_q_up[l],
            mla_kv_lora_norm[l], w_kv_up[l], q_head_norm[l], k_head_norm[l], layer=l, hg_width=hg_width,
            heads=heads, seq=seq)
        y_hg = _hgrn(hq, lf, kk, hv, hg, hg_out_norm[l], batch=batch, seq=seq)
        y_mla = _attention(q, k, v, batch=batch, seq=seq, heads=heads)
        h2 = _merge(y_hg, y_mla, gates, h1, w_hg_branch[l], w_mla_branch[l], w_out[l])
        (h,) = _ffn(h2, ffn2_norm[l], ffn2_w_in[l], ffn2_w_out[l], final_norm[l], emit_h=False, norm_dtype=F32)
    return h.reshape(batch, seq, d)
```

```python
import functools

import numpy as np
import jax
import jax.numpy as jnp
from jax import lax
from jax.experimental import pallas as pl
from jax.experimental.pallas import tpu as pltpu

F32 = jnp.float32
BF16 = jnp.bfloat16

EPS = 1e-6
CHUNK = 64
HG_HEAD = 128
MLA_NOPE = 128
MLA_ROPE = 64
MLA_V = 128
MLA_QK = MLA_NOPE + MLA_ROPE
MLA_HEAD_PAD = 256
ROPE_THETA = 10000.0
LOG2E = float(np.log2(np.e))

LANES = 128
MERGE_TILE = 512
FFN_TILE = 1024
PROJ_TILE = 256
FF_CHUNK = 512
HG_BLOCK = 1024
HG_HEADS_PER_STEP = 4
ATT_TILE = 512
ATT_KEYS = 2 * ATT_TILE
ATT_SPLIT = 2
ATT_UNROLL = 8
ATT_FOLD = 32
ATT_ONES = 16
VMEM_LIMIT = 56 * 1024 * 1024


def _rms(x, gain):
    return x * lax.rsqrt(jnp.mean(x * x, axis=-1, keepdims=True) + EPS) * gain


def _silu(x):
    return x * jax.nn.sigmoid(x)


def _dot(a, b):
    return jnp.dot(a, b, preferred_element_type=F32)


def _dot_nt(a, b):
    return lax.dot_general(a, b, (((1,), (1,)), ((), ())), preferred_element_type=F32)


def _dot_tn(a, b):
    return lax.dot_general(a, b, (((0,), (0,)), ((), ())), preferred_element_type=F32)


def _resident(shape):
    zeros = (0,) * len(shape)
    return pl.BlockSpec(shape, lambda *_: zeros, pipeline_mode=pl.Buffered(1))


def _params(semantics):
    return pltpu.CompilerParams(dimension_semantics=semantics, vmem_limit_bytes=VMEM_LIMIT)


def _ffn_body(x_ref, gin_ref, win_ref, wout_ref, gout_ref, *rest, emit_h):
    if emit_h:
        h_ref, n_ref, a_ref = rest
    else:
        n_ref, a_ref = rest
    d_ff = wout_ref.shape[0]
    x = x_ref[...]
    xn = _rms(x, gin_ref[...]).astype(BF16)
    for lo in range(0, d_ff, FF_CHUNK):
        hi = min(lo + FF_CHUNK, d_ff)
        gate = _dot(xn, win_ref[:, lo:hi])
        up = _dot(xn, win_ref[:, d_ff + lo:d_ff + hi])
        a_ref[:, lo:hi] = (_silu(gate) * up).astype(BF16)
    h = x + 0.5 * _dot(a_ref[...], wout_ref[...])
    if emit_h:
        h_ref[...] = h
    n_ref[...] = _rms(h, gout_ref[...]).astype(n_ref.dtype)


def _ffn(x, gin, w_in, w_out, gout, *, emit_h, norm_dtype):
    t, d = x.shape
    d_ff = w_out.shape[0]
    win = w_in.astype(BF16)
    tm = FFN_TILE
    row = pl.BlockSpec((tm, d), lambda i: (i, 0))
    out_shape = [jax.ShapeDtypeStruct((t, d), norm_dtype)]
    out_specs = [row]
    if emit_h:
        out_shape = [jax.ShapeDtypeStruct((t, d), F32)] + out_shape
        out_specs = [row] + out_specs
    return pl.pallas_call(
        functools.partial(_ffn_body, emit_h=emit_h),
        grid=(t // tm,),
        in_specs=[row, _resident((1, d)), _resident(win.shape), _resident((d_ff, d)), _resident((1, d))],
        out_specs=out_specs,
        out_shape=out_shape,
        scratch_shapes=[pltpu.VMEM((tm, d_ff), BF16)],
        compiler_params=_params(("parallel",)),
        name="ffn1" if emit_h else "ffn2",
    )(x, gin.reshape(1, d), win, w_out.astype(BF16), gout.reshape(1, d))


def _proj_body(u_ref, pos_ref, w_ref, wm_ref, bm_ref, tab_ref, gq_ref, wq_ref, gkv_ref, wkv_ref, qr_gain_ref,
               kn_gain_ref, kr_gain_ref, invf_ref,
               hq_ref, lf_ref, kk_ref, hv_ref, hg_ref, gate_ref, q_ref, k_ref, v_ref, *, layer, heads):
    w = hq_ref.shape[1]
    tm, d = u_ref.shape
    q_lora, kv_lora = wq_ref.shape[0], wkv_ref.shape[0]
    u = u_ref[...]

    c = _dot(u, w_ref[:, 4 * w:])
    hq_ref[...] = _silu(_dot(u, w_ref[:, 0:w])).astype(hq_ref.dtype)
    qraw = _dot(_rms(c[:, :q_lora], gq_ref[...]).astype(BF16), wq_ref[...])
    kvraw = _dot(_rms(c[:, q_lora:q_lora + kv_lora], gkv_ref[...]).astype(BF16), wkv_ref[...])
    kpe = c[:, q_lora + kv_lora:]

    tab = tab_ref[...]
    e = jnp.exp(tab - jnp.max(tab, axis=0, keepdims=True))
    lb = jnp.sum(e[:layer + 1], axis=0, keepdims=True) / jnp.sum(e, axis=0, keepdims=True)
    z = _dot(u, w_ref[:, w:2 * w])
    ez = jnp.exp(-jnp.abs(z))
    r = 1.0 / (1.0 + ez)
    a = ez * r
    pos = z >= 0
    lf_ref[...] = jnp.log(lb + (1.0 - lb) * jnp.where(pos, r, a)) * LOG2E
    kk_ref[...] = ((1.0 - lb) * jnp.where(pos, a, r)).astype(BF16)
    hv_ref[...] = _dot(u, w_ref[:, 2 * w:3 * w]).astype(BF16)
    hg_ref[...] = _silu(_dot(u, w_ref[:, 3 * w:4 * w])).astype(BF16)
    for j in range(2):
        gm = _dot(u, wm_ref[:, j * d:(j + 1) * d]) + bm_ref[:, j * d:(j + 1) * d]
        gate_ref[:, j * d:(j + 1) * d] = jax.nn.sigmoid(gm).astype(BF16)

    half = MLA_ROPE // 2
    rows4 = tm // 4
    lane4 = lax.broadcasted_iota(jnp.int32, (rows4, LANES), 1)
    posf = pos_ref[...].astype(F32)
    pos4 = [posf[g * rows4:(g + 1) * rows4] for g in range(4)]
    packed = jnp.where(lane4 < half, pos4[0],
                       jnp.where(lane4 < 2 * half, pos4[1],
                                 jnp.where(lane4 < 3 * half, pos4[2], pos4[3]))) * invf_ref[...]
    cos4, sin4 = jnp.cos(packed), jnp.sin(packed)
    spread = []
    for g in range(4):
        cg = pltpu.roll(cos4, LANES - half * g, 1) if g else cos4
        sg = pltpu.roll(sin4, LANES - half * g, 1) if g else sin4
        cc = jnp.where(lane4 < half, cg, pltpu.roll(cg, half, 1))
        ss = jnp.where(lane4 < 3 * half, pltpu.roll(sg, 2 * half, 1), pltpu.roll(sg, 3 * half, 1))
        spread.append(jnp.where(lane4 < 2 * half, cc, ss))
    cs = jnp.concatenate(spread, axis=0)
    first = lax.broadcasted_iota(jnp.int32, (tm, LANES), 1) < MLA_ROPE

    def rope(block, gain_cs):
        t = block * gain_cs
        return t + pltpu.roll(t, MLA_ROPE, 1)

    def sumsq(x):
        return jnp.sum(x * x, axis=-1, keepdims=True)

    k_rope = rope(kpe, kr_gain_ref[...] * cs)
    k_pe_ss = 0.5 * sumsq(kpe)
    q_gain_cs = qr_gain_ref[...] * cs
    scale = MLA_QK ** -0.5 * LOG2E
    nope_w = heads * MLA_NOPE
    for h in range(heads):
        lo, hi = h * MLA_NOPE, (h + 1) * MLA_NOPE
        qn = qraw[:, lo:hi]
        qr = qraw[:, nope_w + lo:nope_w + hi]
        sq = lax.rsqrt((sumsq(qn) + 0.5 * sumsq(qr)) / MLA_QK + EPS) * scale
        o = h * MLA_HEAD_PAD
        q_head = jnp.concatenate([qn * sq, rope(qr, q_gain_cs) * sq], axis=1)
        q_ref[0, h, 0] = q_head.T.astype(BF16)
        kn = kvraw[:, lo:hi]
        sk = lax.rsqrt((sumsq(kn) + k_pe_ss) / MLA_QK + EPS)
        k_ref[0, h, :, :MLA_NOPE] = (kn * sk * kn_gain_ref[...]).astype(BF16)
        k_ref[0, h, :, MLA_NOPE:] = jnp.where(first, k_rope * sk, 0.0).astype(BF16)
        v_ref[0, h] = kvraw[:, nope_w + h * MLA_V:nope_w + (h + 1) * MLA_V].astype(BF16)


def _rot_cols(w):
    half = w.shape[-1] // 2
    return jnp.concatenate([-w[..., half:], w[..., :half]], axis=-1)


def _rot_gain(g):
    half = g.shape[-1] // 2
    return jnp.concatenate([g[..., half:], g[..., :half]], axis=-1)


def _proj(u, pos, w_in, w_merge, b_merge, lb_table, gq, w_q_up, gkv, w_kv_up, q_head_gain, k_head_gain,
          *, layer, hg_width, heads, seq):
    t, d = u.shape
    q_lora, kv_lora = w_q_up.shape[0], w_kv_up.shape[0]
    w_in_ext = jnp.concatenate([w_in, _rot_cols(w_in[:, -MLA_ROPE:])], axis=1).astype(BF16)
    cols = w_in_ext.shape[1]
    assert cols == 4 * hg_width + q_lora + kv_lora + 2 * MLA_ROPE
    wq = w_q_up.reshape(q_lora, heads, MLA_QK)
    wq_rope = wq[:, :, MLA_NOPE:]
    wq_ext = jnp.concatenate(
        [wq[:, :, :MLA_NOPE].reshape(q_lora, heads * MLA_NOPE),
         jnp.concatenate([wq_rope, _rot_cols(wq_rope)], axis=-1).reshape(q_lora, heads * 2 * MLA_ROPE)],
        axis=1).astype(BF16)
    wkv = w_kv_up.reshape(kv_lora, heads, MLA_NOPE + MLA_V)
    wkv_ext = jnp.concatenate([wkv[:, :, :MLA_NOPE].reshape(kv_lora, heads * MLA_NOPE),
                               wkv[:, :, MLA_NOPE:].reshape(kv_lora, heads * MLA_V)], axis=1).astype(BF16)

    def rope_gain(g):
        return jnp.concatenate([g[MLA_NOPE:], _rot_gain(g[MLA_NOPE:])]).reshape(1, 2 * MLA_ROPE)

    inv_freq = ROPE_THETA ** (-jnp.arange(0, MLA_ROPE, 2, dtype=F32) / MLA_ROPE)
    invf = jnp.tile(inv_freq, 4).reshape(1, LANES)
    nope_gain = (q_head_gain[:MLA_NOPE] * k_head_gain[:MLA_NOPE]).reshape(1, MLA_NOPE)
    tm = PROJ_TILE

    def row(n):
        return pl.BlockSpec((tm, n), lambda i: (i, 0))

    def head_major(n):
        return pl.BlockSpec((1, heads, tm, n), lambda i: (i // per_seq, 0, i % per_seq, 0))

    per_seq = seq // tm
    per_att = ATT_TILE // tm
    qt_spec = pl.BlockSpec((1, heads, 1, MLA_HEAD_PAD, tm),
                           lambda i: (i // per_seq, 0, (i % per_seq) // per_att, 0, i % per_att))
    return pl.pallas_call(
        functools.partial(_proj_body, layer=layer, heads=heads),
        grid=(t // tm,),
        in_specs=[row(d), row(1), _resident((d, cols)), _resident((d, 2 * d)), _resident((1, 2 * d)),
                  _resident(lb_table.shape), _resident((1, q_lora)), _resident(wq_ext.shape),
                  _resident((1, kv_lora)), _resident(wkv_ext.shape), _resident((1, LANES)),
                  _resident((1, MLA_NOPE)), _resident((1, LANES)), _resident((1, LANES))],
        out_specs=[row(hg_width), row(hg_width), row(hg_width), row(hg_width), row(hg_width), row(2 * d),
                   qt_spec, head_major(MLA_HEAD_PAD), head_major(MLA_V)],
        out_shape=[jax.ShapeDtypeStruct((t, hg_width), BF16),
                   jax.ShapeDtypeStruct((t, hg_width), F32),
                   jax.ShapeDtypeStruct((t, hg_width), BF16),
                   jax.ShapeDtypeStruct((t, hg_width), BF16),
                   jax.ShapeDtypeStruct((t, hg_width), BF16),
                   jax.ShapeDtypeStruct((t, 2 * d), BF16),
                   jax.ShapeDtypeStruct((t // seq, heads, seq // ATT_TILE, MLA_HEAD_PAD, ATT_TILE), BF16),
                   jax.ShapeDtypeStruct((t // seq, heads, seq, MLA_HEAD_PAD), BF16),
                   jax.ShapeDtypeStruct((t // seq, heads, seq, MLA_V), BF16)],
        compiler_params=_params(("parallel",)),
        name="proj",
    )(u, pos, w_in_ext, w_merge.astype(BF16), b_merge.reshape(1, 2 * d), lb_table, gq.reshape(1, q_lora), wq_ext,
      gkv.reshape(1, kv_lora), wkv_ext, rope_gain(q_head_gain), nope_gain, rope_gain(k_head_gain), invf)


def _split3(x):
    hi = x.astype(BF16)
    r = x - hi.astype(F32)
    mid = r.astype(BF16)
    lo = (r - mid.astype(F32)).astype(BF16)
    return hi, mid, lo


def _hgrn_body(q_ref, lf_ref, kk_ref, v_ref, g_ref, gain_ref, tri_ref, y_ref, state_ref):
    @pl.when(pl.program_id(2) == 0)
    def _():
        state_ref[...] = jnp.zeros_like(state_ref)

    n = CHUNK
    tb = q_ref.shape[1]
    chunks = [slice(c * n, (c + 1) * n) for c in range(tb // n)]
    trow = lax.broadcasted_iota(jnp.int32, (n, n), 0)
    tcol = lax.broadcasted_iota(jnp.int32, (n, n), 1)
    tri = tri_ref[...]
    halves = [1 << i for i in range(n.bit_length() - 1)]
    pairs = {half: (trow // (2 * half) == tcol // (2 * half)) & (trow % (2 * half) >= half)
             & (tcol % (2 * half) < half) for half in halves}

    def ref_rows(x, size, row):
        blocks = x.reshape(tb // size, size, HG_HEAD)[:, row:row + 1, :]
        return jnp.broadcast_to(blocks, (tb // size, size, HG_HEAD)).reshape(tb, HG_HEAD)

    r8 = lax.broadcasted_iota(jnp.int32, (1, 8, HG_HEAD), 1)

    def head(q, lf, k, v, g, s_ref):
        lf3 = lf.reshape(tb // 8, 8, HG_HEAD)
        split = jnp.concatenate(_split3(lf), axis=1)
        cums = []
        for sl in chunks:
            r = _dot(tri, split[sl])
            cums.append(r[:, :HG_HEAD] + r[:, HG_HEAD:2 * HG_HEAD] + r[:, 2 * HG_HEAD:])
        cum = jnp.concatenate(cums, axis=0)
        diag = jnp.sum(q * k, axis=-1, keepdims=True)
        scores = [jnp.where(trow == tcol, diag[sl], 0.0) for sl in chunks]
        for half in halves:
            size = 2 * half
            if half >= 4:
                diff = pltpu.bitcast(cum - ref_rows(cum, size, half - 1), jnp.uint32)
                dec = pltpu.bitcast(diff | jnp.uint32(0x80000000), F32)
            elif half == 2:
                dec = jnp.where(r8 % 4 == 3, lf3 + pltpu.roll(lf3, 1, 1),
                                jnp.where(r8 % 4 == 2, lf3,
                                          jnp.where(r8 % 4 == 0, pltpu.roll(lf3, 7, 1), 0.0))).reshape(tb, HG_HEAD)
            else:
                dec = jnp.where(r8 % 2 == 1, lf3, 0.0).reshape(tb, HG_HEAD)
            e = jnp.exp2(dec)
            qe = (q * e).astype(BF16)
            ke = (k * e).astype(BF16)
            for c, sl in enumerate(chunks):
                scores[c] = scores[c] + jnp.where(pairs[half], _dot_nt(qe[sl], ke[sl]), 0.0)
        last = ref_rows(cum, n, n - 1)
        qdec = (q * jnp.exp2(cum)).astype(BF16)
        kdec = (k * jnp.exp2(last - cum)).astype(BF16)
        updates = [_dot_tn(kdec[sl], v[sl]) for sl in chunks]
        lasts = jnp.concatenate([cum[sl][n - 1:n, :] for sl in chunks], axis=0)
        decay_cols = jnp.concatenate([jnp.exp2(lasts)] * (HG_HEAD // len(chunks)), axis=0).T
        state = s_ref[...]
        outs = []
        for c, sl in enumerate(chunks):
            lhs = jnp.concatenate([qdec[sl], scores[c].astype(BF16)], axis=1)
            rhs = jnp.concatenate([state.astype(BF16), v[sl]], axis=0)
            outs.append(_dot(lhs, rhs))
            state = state * decay_cols[:, c:c + 1] + updates[c]
        s_ref[...] = state
        return _rms(jnp.concatenate(outs, axis=0), gain_ref[...]) * g

    for hd in range(q_ref.shape[2] // HG_HEAD):
        lanes = slice(hd * HG_HEAD, (hd + 1) * HG_HEAD)
        y_ref[0, :, lanes] = head(q_ref[0, :, lanes].astype(F32), lf_ref[0, :, lanes],
                                  kk_ref[0, :, lanes].astype(F32), v_ref[0, :, lanes],
                                  g_ref[0, :, lanes].astype(F32), state_ref.at[hd]).astype(y_ref.dtype)


def _hgrn(hq, lf, kk, v, g, out_gain, *, batch, seq):
    width = hq.shape[1]
    heads = width // HG_HEAD
    tb = HG_BLOCK
    tri = jnp.asarray(np.tril(np.ones((CHUNK, CHUNK), np.float32)), BF16)

    def r3(a):
        return a.reshape(batch, seq, width)

    blk = pl.BlockSpec((1, tb, HG_HEADS_PER_STEP * HG_HEAD), lambda b, h, j: (b, j, h))
    y = pl.pallas_call(
        _hgrn_body,
        grid=(batch, heads // HG_HEADS_PER_STEP, seq // tb),
        in_specs=[blk, blk, blk, blk, blk, _resident((1, HG_HEAD)), _resident((CHUNK, CHUNK))],
        out_specs=blk,
        out_shape=jax.ShapeDtypeStruct((batch, seq, width), BF16),
        scratch_shapes=[pltpu.VMEM((HG_HEADS_PER_STEP, HG_HEAD, HG_HEAD), F32)],
        compiler_params=_params(("parallel", "parallel", "arbitrary")),
        name="hgrn",
    )(r3(hq), r3(lf), r3(kk), r3(v), r3(g), out_gain.reshape(1, HG_HEAD), tri)
    return y.reshape(batch * seq, width)


def _attn_body(q_ref, k_ref, v_ref, o_ref, vt_ref, s_ref, mx_ref, m_ref, acc_ref):
    seq = k_ref.shape[2]
    tq = ATT_TILE
    tk = ATT_KEYS
    tqs = tq // ATT_SPLIT
    n_q = seq // tq
    n_k = seq // tk
    max_full = (n_q - 1) // 2

    def transpose_v(j, _):
        vt_ref[j, :MLA_V] = v_ref[0, 0, pl.ds(pl.multiple_of(j * tk, tk), tk), :].astype(F32).T.astype(BF16)
        vt_ref[j, MLA_V:] = jnp.ones((ATT_ONES, tk), BF16)
        return 0

    lax.fori_loop(0, n_k, transpose_v, 0)

    key_chunk = lax.broadcasted_iota(jnp.int32, (tk, tqs), 0) // CHUNK
    qry_chunk = lax.broadcasted_iota(jnp.int32, (tk, tqs), 1) // CHUNK
    first_query = {"half": 0, "whole": tq}
    allowed = {kind: [key_chunk <= qry_chunk + (off + sub * tqs) // CHUNK for sub in range(ATT_SPLIT)]
               for kind, off in first_query.items()}

    def fold_rows(x, op):
        rows, cols = x.shape
        x = x.reshape(rows // ATT_FOLD, ATT_FOLD, cols)
        out = x[0]
        for i in range(1, rows // ATT_FOLD):
            out = op(out, x[i])
        return out

    def aligned(x, m):
        return x if isinstance(x, int) else pl.multiple_of(x, m)

    def scores(qt, kj, buf):
        k_tile = k_ref[0, 0, pl.ds(aligned(kj * tk, tk), tk), :]
        s = _dot(k_tile, q_ref[0, 0, qt])
        s_ref[buf] = s
        mx_ref[buf] = jnp.max(fold_rows(s, jnp.maximum), axis=0, keepdims=True)

    def start():
        m_ref[...] = jnp.full(m_ref.shape, -jnp.inf, F32)
        acc_ref[...] = jnp.zeros(acc_ref.shape, F32)

    def finish(base):
        for sub in range(ATT_SPLIT):
            acc = acc_ref[sub]
            o_ref[0, pl.ds(aligned(base + sub * tqs, tqs), tqs), :] = (
                (acc[:MLA_V] / acc[MLA_V:MLA_V + 1]).T.astype(o_ref.dtype))

    def absorb(kj, buf, kind):
        for sub in range(ATT_SPLIT):
            cols = slice(sub * tqs, (sub + 1) * tqs)
            keys = tk if kind == "full" else first_query[kind] + (sub + 1) * tqs
            vt_tile = vt_ref[kj, :, :keys]
            ss = s_ref[buf, :keys, cols]
            if kind == "full":
                tile_max = mx_ref[buf, :, cols]
            else:
                ss = jnp.where(allowed[kind][sub][:keys], ss, -jnp.inf)
                tile_max = jnp.max(fold_rows(ss, jnp.maximum), axis=0, keepdims=True)
            m = m_ref[sub]
            m_new = jnp.maximum(m, tile_max)
            alpha = jnp.exp2(m - m_new)
            p = jnp.exp2(ss - m_new)
            m_ref[sub] = m_new
            acc_ref[sub] = alpha * acc_ref[sub] + _dot(vt_tile, p.astype(BF16))

    def run(qi, next_q, first, count, diagonal, from_zero):
        for j in range(count):
            last = diagonal is not None and j + 1 == count
            if last:
                scores(next_q, 0, 2)
            else:
                scores(qi, first + j + 1, (j + 1) % 2)
            absorb(first + j, 2 if from_zero and j == 0 else j % 2, diagonal if last else "full")

    def q_step(qi, _):
        next_q = jnp.minimum(qi + 1, n_q - 1)
        kinds = tuple(enumerate(("half", "whole")))
        full = qi // 2
        base = pl.multiple_of(qi * tq, tq)
        for nf in range(1, min(ATT_UNROLL, max_full + 1)):
            for odd, kind in kinds:
                @pl.when((full == nf) & (qi % 2 == odd))
                def _(nf=nf, kind=kind):
                    start()
                    run(qi, next_q, 0, nf + 1, kind, True)
                    finish(base)

        if max_full < ATT_UNROLL:
            return 0

        @pl.when(full >= ATT_UNROLL)
        def _():
            start()
            run(qi, next_q, 0, ATT_UNROLL, None, True)

        def group(g, _):
            run(qi, next_q, (g + 1) * ATT_UNROLL, ATT_UNROLL, None, False)
            return 0

        later = full - ATT_UNROLL
        lax.fori_loop(0, later // ATT_UNROLL, group, 0)
        for rem in range(ATT_UNROLL):
            for odd, kind in kinds:
                @pl.when((later >= 0) & (later % ATT_UNROLL == rem) & (qi % 2 == odd))
                def _(rem=rem, kind=kind):
                    run(qi, next_q, full - rem, rem + 1, kind, False)
                    finish(base)

        return 0

    start()
    scores(0, 0, 1)
    scores(1, 0, 0)
    absorb(0, 1, "half")
    finish(0)
    start()
    scores(min(2, n_q - 1), 0, 2)
    absorb(0, 0, "whole")
    finish(tq)
    if n_q > 2:
        lax.fori_loop(2, n_q, q_step, 0)


def _attention(q, k, v, *, batch, seq, heads):
    qt = pl.BlockSpec((1, 1, seq // ATT_TILE, MLA_HEAD_PAD, ATT_TILE), lambda b, h: (b, h, 0, 0, 0))
    kk = pl.BlockSpec((1, 1, seq, MLA_HEAD_PAD), lambda b, h: (b, h, 0, 0))
    vv = pl.BlockSpec((1, 1, seq, MLA_V), lambda b, h: (b, h, 0, 0))
    vo = pl.BlockSpec((1, seq, MLA_V), lambda b, h: (b, 0, h))
    o = pl.pallas_call(
        _attn_body,
        grid=(batch, heads),
        in_specs=[qt, kk, vv],
        out_specs=vo,
        out_shape=jax.ShapeDtypeStruct((batch, seq, heads * MLA_V), BF16),
        scratch_shapes=[pltpu.VMEM((seq // ATT_KEYS, MLA_V + ATT_ONES, ATT_KEYS), BF16),
                        pltpu.VMEM((3, ATT_KEYS, ATT_TILE), F32),
                        pltpu.VMEM((3, 1, ATT_TILE), F32),
                        pltpu.VMEM((ATT_SPLIT, 1, ATT_TILE // ATT_SPLIT), F32),
                        pltpu.VMEM((ATT_SPLIT, MLA_V + ATT_ONES, ATT_TILE // ATT_SPLIT), F32)],
        compiler_params=_params(("parallel", "parallel")),
        name="attn",
    )(q, k, v)
    return o.reshape(batch * seq, heads * MLA_V)


def _merge_body(yh_ref, ym_ref, gate_ref, h_ref, wh_ref, wm_ref, wo_ref, o_ref):
    d = h_ref.shape[1]
    mix = (gate_ref[:, :d].astype(F32) * _dot(yh_ref[...], wh_ref[...])
           + gate_ref[:, d:].astype(F32) * _dot(ym_ref[...], wm_ref[...]))
    o_ref[...] = h_ref[...] + _dot(mix.astype(BF16), wo_ref[...])


def _merge(yh, ym, gates, h, w_h, w_m, w_o):
    t, d = h.shape
    tm = MERGE_TILE

    def row(n):
        return pl.BlockSpec((tm, n), lambda i: (i, 0))

    return pl.pallas_call(
        _merge_body,
        grid=(t // tm,),
        in_specs=[row(yh.shape[1]), row(ym.shape[1]), row(2 * d), row(d), _resident(w_h.shape),
                  _resident(w_m.shape), _resident(w_o.shape)],
        out_specs=row(d),
        out_shape=jax.ShapeDtypeStruct((t, d), F32),
        compiler_params=_params(("parallel",)),
        name="merge",
    )(yh, ym, gates, h, w_h.astype(BF16), w_m.astype(BF16), w_o.astype(BF16))


def kernel(x, positions, ffn1_norm, ffn1_w_in, ffn1_w_out, mix_norm, w_in, hg_lb_table, hg_out_norm, w_hg_branch, mla_q_lora_norm, w_q_up, mla_kv_lora_norm, w_kv_up, q_head_norm, k_head_norm, w_mla_branch, w_merge, b_merge, w_out, ffn2_norm, ffn2_w_in, ffn2_w_out, final_norm):
    batch, seq, d = x.shape
    depth = ffn1_norm.shape[0]
    hg_width = hg_lb_table.shape[1]
    heads = w_q_up.shape[2] // MLA_QK
    assert hg_out_norm.shape[1] == HG_HEAD and w_hg_branch.shape[1] == hg_width
    assert q_head_norm.shape[1] == MLA_QK and w_kv_up.shape[2] == heads * (MLA_NOPE + MLA_V)
    assert seq % max(HG_BLOCK, ATT_KEYS) == 0 and (batch * seq) % max(MERGE_TILE, FFN_TILE) == 0
    t = batch * seq
    pos = positions.reshape(t, 1)
    h = x.reshape(t, d)
    for l in range(depth):
        h1, u = _ffn(h, ffn1_norm[l], ffn1_w_in[l], ffn1_w_out[l], mix_norm[l], emit_h=True, norm_dtype=BF16)
        hq, lf, kk, hv, hg, gates, q, k, v = _proj(
            u, pos, w_in[l], w_merge[l], b_merge[l], hg_lb_table, mla_q_lora_norm[l], w_q_up[l],
            mla_kv_lora_norm[l], w_kv_up[l], q_head_norm[l], k_head_norm[l], layer=l, hg_width=hg_width,
            heads=heads, seq=seq)
        y_hg = _hgrn(hq, lf, kk, hv, hg, hg_out_norm[l], batch=batch, seq=seq)
        y_mla = _attention(q, k, v, batch=batch, seq=seq, heads=heads)
        h2 = _merge(y_hg, y_mla, gates, h1, w_hg_branch[l], w_mla_branch[l], w_out[l])
        (h,) = _ffn(h2, ffn2_norm[l], ffn2_w_in[l], ffn2_w_out[l], final_norm[l], emit_h=False, norm_dtype=F32)
    return h.reshape(batch, seq, d)
```

```python
import functools

import numpy as np
import jax
import jax.numpy as jnp
from jax import lax
from jax.experimental import pallas as pl
from jax.experimental.pallas import tpu as pltpu

F32 = jnp.float32
BF16 = jnp.bfloat16

EPS = 1e-6
CHUNK = 64
HG_HEAD = 128
MLA_NOPE = 128
MLA_ROPE = 64
MLA_V = 128
MLA_QK = MLA_NOPE + MLA_ROPE
MLA_HEAD_PAD = 256
ROPE_THETA = 10000.0
LOG2E = float(np.log2(np.e))

LANES = 128
MERGE_TILE = 512
FFN_TILE = 1024
PROJ_TILE = 256
FF_CHUNK = 512
HG_BLOCK = 1024
HG_HEADS_PER_STEP = 4
ATT_TILE = 512
ATT_KEYS = 2 * ATT_TILE
ATT_SPLIT = 2
ATT_UNROLL = 8
ATT_FOLD = 32
ATT_ONES = 16
VMEM_LIMIT = 56 * 1024 * 1024


def _rms(x, gain):
    return x * lax.rsqrt(jnp.mean(x * x, axis=-1, keepdims=True) + EPS) * gain


def _silu(x):
    return x * jax.nn.sigmoid(x)


def _dot(a, b):
    return jnp.dot(a, b, preferred_element_type=F32)


def _dot_nt(a, b):
    return lax.dot_general(a, b, (((1,), (1,)), ((), ())), preferred_element_type=F32)


def _dot_tn(a, b):
    return lax.dot_general(a, b, (((0,), (0,)), ((), ())), preferred_element_type=F32)


def _resident(shape):
    zeros = (0,) * len(shape)
    return pl.BlockSpec(shape, lambda *_: zeros, pipeline_mode=pl.Buffered(1))


def _params(semantics):
    return pltpu.CompilerParams(dimension_semantics=semantics, vmem_limit_bytes=VMEM_LIMIT)


def _ffn_body(x_ref, gin_ref, win_ref, wout_ref, gout_ref, *rest, emit_h):
    if emit_h:
        h_ref, n_ref, a_ref = rest
    else:
        n_ref, a_ref = rest
    d_ff = wout_ref.shape[0]
    x = x_ref[...]
    xn = _rms(x, gin_ref[...]).astype(BF16)
    for lo in range(0, d_ff, FF_CHUNK):
        hi = min(lo + FF_CHUNK, d_ff)
        gate = _dot(xn, win_ref[:, lo:hi])
        up = _dot(xn, win_ref[:, d_ff + lo:d_ff + hi])
        a_ref[:, lo:hi] = (_silu(gate) * up).astype(BF16)
    h = x + 0.5 * _dot(a_ref[...], wout_ref[...])
    if emit_h:
        h_ref[...] = h
    n_ref[...] = _rms(h, gout_ref[...]).astype(n_ref.dtype)


def _ffn(x, gin, w_in, w_out, gout, *, emit_h, norm_dtype):
    t, d = x.shape
    d_ff = w_out.shape[0]
    win = w_in.astype(BF16)
    tm = FFN_TILE
    row = pl.BlockSpec((tm, d), lambda i: (i, 0))
    out_shape = [jax.ShapeDtypeStruct((t, d), norm_dtype)]
    out_specs = [row]
    if emit_h:
        out_shape = [jax.ShapeDtypeStruct((t, d), F32)] + out_shape
        out_specs = [row] + out_specs
    return pl.pallas_call(
        functools.partial(_ffn_body, emit_h=emit_h),
        grid=(t // tm,),
        in_specs=[row, _resident((1, d)), _resident(win.shape), _resident((d_ff, d)), _resident((1, d))],
        out_specs=out_specs,
        out_shape=out_shape,
        scratch_shapes=[pltpu.VMEM((tm, d_ff), BF16)],
        compiler_params=_params(("parallel",)),
        name="ffn1" if emit_h else "ffn2",
    )(x, gin.reshape(1, d), win, w_out.astype(BF16), gout.reshape(1, d))


def _proj_body(u_ref, pos_ref, w_ref, wm_ref, bm_ref, tab_ref, gq_ref, wq_ref, gkv_ref, wkv_ref, qr_gain_ref,
               kn_gain_ref, kr_gain_ref, invf_ref,
               hq_ref, lf_ref, kk_ref, hv_ref, hg_ref, gate_ref, q_ref, k_ref, v_ref, *, layer, heads):
    w = hq_ref.shape[1]
    tm, d = u_ref.shape
    q_lora, kv_lora = wq_ref.shape[0], wkv_ref.shape[0]
    u = u_ref[...]

    c = _dot(u, w_ref[:, 4 * w:])
    hq_ref[...] = _silu(_dot(u, w_ref[:, 0:w])).astype(hq_ref.dtype)
    qraw = _dot(_rms(c[:, :q_lora], gq_ref[...]).astype(BF16), wq_ref[...])
    kvraw = _dot(_rms(c[:, q_lora:q_lora + kv_lora], gkv_ref[...]).astype(BF16), wkv_ref[...])
    kpe = c[:, q_lora + kv_lora:]

    tab = tab_ref[...]
    e = jnp.exp(tab - jnp.max(tab, axis=0, keepdims=True))
    lb = jnp.sum(e[:layer + 1], axis=0, keepdims=True) / jnp.sum(e, axis=0, keepdims=True)
    z = _dot(u, w_ref[:, w:2 * w])
    ez = jnp.exp(-jnp.abs(z))
    r = 1.0 / (1.0 + ez)
    a = ez * r
    pos = z >= 0
    lf_ref[...] = jnp.log(lb + (1.0 - lb) * jnp.where(pos, r, a)) * LOG2E
    kk_ref[...] = ((1.0 - lb) * jnp.where(pos, a, r)).astype(BF16)
    hv_ref[...] = _dot(u, w_ref[:, 2 * w:3 * w]).astype(BF16)
    hg_ref[...] = _silu(_dot(u, w_ref[:, 3 * w:4 * w])).astype(BF16)
    for j in range(2):
        gm = _dot(u, wm_ref[:, j * d:(j + 1) * d]) + bm_ref[:, j * d:(j + 1) * d]
        gate_ref[:, j * d:(j + 1) * d] = jax.nn.sigmoid(gm).astype(BF16)

    half = MLA_ROPE // 2
    rows4 = tm // 4
    lane4 = lax.broadcasted_iota(jnp.int32, (rows4, LANES), 1)
    posf = pos_ref[...].astype(F32)
    pos4 = [posf[g * rows4:(g + 1) * rows4] for g in range(4)]
    packed = jnp.where(lane4 < half, pos4[0],
                       jnp.where(lane4 < 2 * half, pos4[1],
                                 jnp.where(lane4 < 3 * half, pos4[2], pos4[3]))) * invf_ref[...]
    cos4, sin4 = jnp.cos(packed), jnp.sin(packed)
    spread = []
    for g in range(4):
        cg = pltpu.roll(cos4, LANES - half * g, 1) if g else cos4
        sg = pltpu.roll(sin4, LANES - half * g, 1) if g else sin4
        cc = jnp.where(lane4 < half, cg, pltpu.roll(cg, half, 1))
        ss = jnp.where(lane4 < 3 * half, pltpu.roll(sg, 2 * half, 1), pltpu.roll(sg, 3 * half, 1))
        spread.append(jnp.where(lane4 < 2 * half, cc, ss))
    cs = jnp.concatenate(spread, axis=0)
    first = lax.broadcasted_iota(jnp.int32, (tm, LANES), 1) < MLA_ROPE

    def rope(block, gain_cs):
        t = block * gain_cs
        return t + pltpu.roll(t, MLA_ROPE, 1)

    def sumsq(x):
        return jnp.sum(x * x, axis=-1, keepdims=True)

    k_rope = rope(kpe, kr_gain_ref[...] * cs)
    k_pe_ss = 0.5 * sumsq(kpe)
    q_gain_cs = qr_gain_ref[...] * cs
    scale = MLA_QK ** -0.5 * LOG2E
    nope_w = heads * MLA_NOPE
    for h in range(heads):
        lo, hi = h * MLA_NOPE, (h + 1) * MLA_NOPE
        qn = qraw[:, lo:hi]
        qr = qraw[:, nope_w + lo:nope_w + hi]
        sq = lax.rsqrt((sumsq(qn) + 0.5 * sumsq(qr)) / MLA_QK + EPS) * scale
        o = h * MLA_HEAD_PAD
        q_head = jnp.concatenate([qn * sq, rope(qr, q_gain_cs) * sq], axis=1)
        q_ref[0, h, 0] = q_head.T.astype(BF16)
        kn = kvraw[:, lo:hi]
        sk = lax.rsqrt((sumsq(kn) + k_pe_ss) / MLA_QK + EPS)
        k_ref[0, h, :, :MLA_NOPE] = (kn * sk * kn_gain_ref[...]).astype(BF16)
        k_ref[0, h, :, MLA_NOPE:] = jnp.where(first, k_rope * sk, 0.0).astype(BF16)
        v_ref[0, h] = kvraw[:, nope_w + h * MLA_V:nope_w + (h + 1) * MLA_V].astype(BF16)


def _rot_cols(w):
    half = w.shape[-1] // 2
    return jnp.concatenate([-w[..., half:], w[..., :half]], axis=-1)


def _rot_gain(g):
    half = g.shape[-1] // 2
    return jnp.concatenate([g[..., half:], g[..., :half]], axis=-1)


def _proj(u, pos, w_in, w_merge, b_merge, lb_table, gq, w_q_up, gkv, w_kv_up, q_head_gain, k_head_gain,
          *, layer, hg_width, heads, seq):
    t, d = u.shape
    q_lora, kv_lora = w_q_up.shape[0], w_kv_up.shape[0]
    w_in_ext = jnp.concatenate([w_in, _rot_cols(w_in[:, -MLA_ROPE:])], axis=1).astype(BF16)
    cols = w_in_ext.shape[1]
    assert cols == 4 * hg_width + q_lora + kv_lora + 2 * MLA_ROPE
    wq = w_q_up.reshape(q_lora, heads, MLA_QK)
    wq_rope = wq[:, :, MLA_NOPE:]
    wq_ext = jnp.concatenate(
        [wq[:, :, :MLA_NOPE].reshape(q_lora, heads * MLA_NOPE),
         jnp.concatenate([wq_rope, _rot_cols(wq_rope)], axis=-1).reshape(q_lora, heads * 2 * MLA_ROPE)],
        axis=1).astype(BF16)
    wkv = w_kv_up.reshape(kv_lora, heads, MLA_NOPE + MLA_V)
    wkv_ext = jnp.concatenate([wkv[:, :, :MLA_NOPE].reshape(kv_lora, heads * MLA_NOPE),
                               wkv[:, :, MLA_NOPE:].reshape(kv_lora, heads * MLA_V)], axis=1).astype(BF16)

    def rope_gain(g):
        return jnp.concatenate([g[MLA_NOPE:], _rot_gain(g[MLA_NOPE:])]).reshape(1, 2 * MLA_ROPE)

    inv_freq = ROPE_THETA ** (-jnp.arange(0, MLA_ROPE, 2, dtype=F32) / MLA_ROPE)
    invf = jnp.tile(inv_freq, 4).reshape(1, LANES)
    nope_gain = (q_head_gain[:MLA_NOPE] * k_head_gain[:MLA_NOPE]).reshape(1, MLA_NOPE)
    tm = PROJ_TILE

    def row(n):
        return pl.BlockSpec((tm, n), lambda i: (i, 0))

    def head_major(n):
        return pl.BlockSpec((1, heads, tm, n), lambda i: (i // per_seq, 0, i % per_seq, 0))

    per_seq = seq // tm
    per_att = ATT_TILE // tm
    qt_spec = pl.BlockSpec((1, heads, 1, MLA_HEAD_PAD, tm),
                           lambda i: (i // per_seq, 0, (i % per_seq) // per_att, 0, i % per_att))
    return pl.pallas_call(
        functools.partial(_proj_body, layer=layer, heads=heads),
        grid=(t // tm,),
        in_specs=[row(d), row(1), _resident((d, cols)), _resident((d, 2 * d)), _resident((1, 2 * d)),
                  _resident(lb_table.shape), _resident((1, q_lora)), _resident(wq_ext.shape),
                  _resident((1, kv_lora)), _resident(wkv_ext.shape), _resident((1, LANES)),
                  _resident((1, MLA_NOPE)), _resident((1, LANES)), _resident((1, LANES))],
        out_specs=[row(hg_width), row(hg_width), row(hg_width), row(hg_width), row(hg_width), row(2 * d),
                   qt_spec, head_major(MLA_HEAD_PAD), head_major(MLA_V)],
        out_shape=[jax.ShapeDtypeStruct((t, hg_width), BF16),
                   jax.ShapeDtypeStruct((t, hg_width), F32),
                   jax.ShapeDtypeStruct((t, hg_width), BF16),
                   jax.ShapeDtypeStruct((t, hg_width), BF16),
                   jax.ShapeDtypeStruct((t, hg_width), BF16),
                   jax.ShapeDtypeStruct((t, 2 * d), BF16),
                   jax.ShapeDtypeStruct((t // seq, heads, seq // ATT_TILE, MLA_HEAD_PAD, ATT_TILE), BF16),
                   jax.ShapeDtypeStruct((t // seq, heads, seq, MLA_HEAD_PAD), BF16),
                   jax.ShapeDtypeStruct((t // seq, heads, seq, MLA_V), BF16)],
        compiler_params=_params(("parallel",)),
        name="proj",
    )(u, pos, w_in_ext, w_merge.astype(BF16), b_merge.reshape(1, 2 * d), lb_table, gq.reshape(1, q_lora), wq_ext,
      gkv.reshape(1, kv_lora), wkv_ext, rope_gain(q_head_gain), nope_gain, rope_gain(k_head_gain), invf)


def _split3(x):
    hi = x.astype(BF16)
    r = x - hi.astype(F32)
    mid = r.astype(BF16)
    lo = (r - mid.astype(F32)).astype(BF16)
    return hi, mid, lo


def _hgrn_body(q_ref, lf_ref, kk_ref, v_ref, g_ref, gain_ref, tri_ref, y_ref, state_ref):
    @pl.when(pl.program_id(2) == 0)
    def _():
        state_ref[...] = jnp.zeros_like(state_ref)

    n = CHUNK
    tb = q_ref.shape[1]
    chunks = [slice(c * n, (c + 1) * n) for c in range(tb // n)]
    trow = lax.broadcasted_iota(jnp.int32, (n, n), 0)
    tcol = lax.broadcasted_iota(jnp.int32, (n, n), 1)
    tri = tri_ref[...]
    halves = [1 << i for i in range(n.bit_length() - 1)]
    pairs = {half: (trow // (2 * half) == tcol // (2 * half)) & (trow % (2 * half) >= half)
             & (tcol % (2 * half) < half) for half in halves}

    def ref_rows(x, size, row):
        blocks = x.reshape(tb // size, size, HG_HEAD)[:, row:row + 1, :]
        return jnp.broadcast_to(blocks, (tb // size, size, HG_HEAD)).reshape(tb, HG_HEAD)

    r8 = lax.broadcasted_iota(jnp.int32, (1, 8, HG_HEAD), 1)

    def head(q, lf, k, v, g, s_ref):
        lf3 = lf.reshape(tb // 8, 8, HG_HEAD)
        split = jnp.concatenate(_split3(lf), axis=1)
        cums = []
        for sl in chunks:
            r = _dot(tri, split[sl])
            cums.append(r[:, :HG_HEAD] + r[:, HG_HEAD:2 * HG_HEAD] + r[:, 2 * HG_HEAD:])
        cum = jnp.concatenate(cums, axis=0)
        diag = jnp.sum(q * k, axis=-1, keepdims=True)
        scores = [jnp.where(trow == tcol, diag[sl], 0.0) for sl in chunks]
        for half in halves:
            size = 2 * half
            if half >= 4:
                diff = pltpu.bitcast(cum - ref_rows(cum, size, half - 1), jnp.uint32)
                dec = pltpu.bitcast(diff | jnp.uint32(0x80000000), F32)
            elif half == 2:
                dec = jnp.where(r8 % 4 == 3, lf3 + pltpu.roll(lf3, 1, 1),
                                jnp.where(r8 % 4 == 2, lf3,
                                          jnp.where(r8 % 4 == 0, pltpu.roll(lf3, 7, 1), 0.0))).reshape(tb, HG_HEAD)
            else:
                dec = jnp.where(r8 % 2 == 1, lf3, 0.0).reshape(tb, HG_HEAD)
            e = jnp.exp2(dec)
            qe = (q * e).astype(BF16)
            ke = (k * e).astype(BF16)
            for c, sl in enumerate(chunks):
                scores[c] = scores[c] + jnp.where(pairs[half], _dot_nt(qe[sl], ke[sl]), 0.0)
        last = ref_rows(cum, n, n - 1)
        qdec = (q * jnp.exp2(cum)).astype(BF16)
        kdec = (k * jnp.exp2(last - cum)).astype(BF16)
        updates = [_dot_tn(kdec[sl], v[sl]) for sl in chunks]
        lasts = jnp.concatenate([cum[sl][n - 1:n, :] for sl in chunks], axis=0)
        decay_cols = jnp.concatenate([jnp.exp2(lasts)] * (HG_HEAD // len(chunks)), axis=0).T
        state = s_ref[...]
        outs = []
        for c, sl in enumerate(chunks):
            lhs = jnp.concatenate([qdec[sl], scores[c].astype(BF16)], axis=1)
            rhs = jnp.concatenate([state.astype(BF16), v[sl]], axis=0)
            outs.append(_dot(lhs, rhs))
            state = state * decay_cols[:, c:c + 1] + updates[c]
        s_ref[...] = state
        return _rms(jnp.concatenate(outs, axis=0), gain_ref[...]) * g

    for hd in range(q_ref.shape[2] // HG_HEAD):
        lanes = slice(hd * HG_HEAD, (hd + 1) * HG_HEAD)
        y_ref[0, :, lanes] = head(q_ref[0, :, lanes].astype(F32), lf_ref[0, :, lanes],
                                  kk_ref[0, :, lanes].astype(F32), v_ref[0, :, lanes],
                                  g_ref[0, :, lanes].astype(F32), state_ref.at[hd]).astype(y_ref.dtype)


def _hgrn(hq, lf, kk, v, g, out_gain, *, batch, seq):
    width = hq.shape[1]
    heads = width // HG_HEAD
    tb = HG_BLOCK
    tri = jnp.asarray(np.tril(np.ones((CHUNK, CHUNK), np.float32)), BF16)

    def r3(a):
        return a.reshape(batch, seq, width)

    blk = pl.BlockSpec((1, tb, HG_HEADS_PER_STEP * HG_HEAD), lambda b, h, j: (b, j, h))
    y = pl.pallas_call(
        _hgrn_body,
        grid=(batch, heads // HG_HEADS_PER_STEP, seq // tb),
        in_specs=[blk, blk, blk, blk, blk, _resident((1, HG_HEAD)), _resident((CHUNK, CHUNK))],
        out_specs=blk,
        out_shape=jax.ShapeDtypeStruct((batch, seq, width), BF16),
        scratch_shapes=[pltpu.VMEM((HG_HEADS_PER_STEP, HG_HEAD, HG_HEAD), F32)],
        compiler_params=_params(("parallel", "parallel", "arbitrary")),
        name="hgrn",
    )(r3(hq), r3(lf), r3(kk), r3(v), r3(g), out_gain.reshape(1, HG_HEAD), tri)
    return y.reshape(batch * seq, width)


def _attn_body(q_ref, k_ref, v_ref, o_ref, vt_ref, s_ref, mx_ref, m_ref, acc_ref):
    seq = k_ref.shape[2]
    tq = ATT_TILE
    tk = ATT_KEYS
    tqs = tq // ATT_SPLIT
    n_q = seq // tq
    n_k = seq // tk
    max_full = (n_q - 1) // 2

    def transpose_v(j, _):
        vt_ref[j, :MLA_V] = v_ref[0, 0, pl.ds(pl.multiple_of(j * tk, tk), tk), :].astype(F32).T.astype(BF16)
        vt_ref[j, MLA_V:] = jnp.ones((ATT_ONES, tk), BF16)
        return 0

    lax.fori_loop(0, n_k, transpose_v, 0)

    key_chunk = lax.broadcasted_iota(jnp.int32, (tk, tqs), 0) // CHUNK
    qry_chunk = lax.broadcasted_iota(jnp.int32, (tk, tqs), 1) // CHUNK
    first_query = {"half": 0, "whole": tq}
    allowed = {kind: [key_chunk <= qry_chunk + (off + sub * tqs) // CHUNK for sub in range(ATT_SPLIT)]
               for kind, off in first_query.items()}

    def fold_rows(x, op):
        rows, cols = x.shape
        x = x.reshape(rows // ATT_FOLD, ATT_FOLD, cols)
        out = x[0]
        for i in range(1, rows // ATT_FOLD):
            out = op(out, x[i])
        return out

    def aligned(x, m):
        return x if isinstance(x, int) else pl.multiple_of(x, m)

    def scores(qt, kj, buf):
        k_tile = k_ref[0, 0, pl.ds(aligned(kj * tk, tk), tk), :]
        s = _dot(k_tile, q_ref[0, 0, qt])
        s_ref[buf] = s
        mx_ref[buf] = jnp.max(fold_rows(s, jnp.maximum), axis=0, keepdims=True)

    def start():
        m_ref[...] = jnp.full(m_ref.shape, -jnp.inf, F32)
        acc_ref[...] = jnp.zeros(acc_ref.shape, F32)

    def finish(base):
        for sub in range(ATT_SPLIT):
            acc = acc_ref[sub]
            o_ref[0, pl.ds(aligned(base + sub * tqs, tqs), tqs), :] = (
                (acc[:MLA_V] / acc[MLA_V:MLA_V + 1]).T.astype(o_ref.dtype))

    def absorb(kj, buf, kind):
        for sub in range(ATT_SPLIT):
            cols = slice(sub * tqs, (sub + 1) * tqs)
            keys = tk if kind == "full" else first_query[kind] + (sub + 1) * tqs
            vt_tile = vt_ref[kj, :, :keys]
            ss = s_ref[buf, :keys, cols]
            if kind == "full":
                tile_max = mx_ref[buf, :, cols]
            else:
                ss = jnp.where(allowed[kind][sub][:keys], ss, -jnp.inf)
                tile_max = jnp.max(fold_rows(ss, jnp.maximum), axis=0, keepdims=True)
            m = m_ref[sub]
            m_new = jnp.maximum(m, tile_max)
            alpha = jnp.exp2(m - m_new)
            p = jnp.exp2(ss - m_new)
            m_ref[sub] = m_new
            acc_ref[sub] = alpha * acc_ref[sub] + _dot(vt_tile, p.astype(BF16))

    def run(qi, next_q, first, count, diagonal, from_zero):
        for j in range(count):
            last = diagonal is not None and j + 1 == count
            if last:
                scores(next_q, 0, 2)
            else:
                scores(qi, first + j + 1, (j + 1) % 2)
            absorb(first + j, 2 if from_zero and j == 0 else j % 2, diagonal if last else "full")

    def pair_step(full, _):
        pair = [(2 * full + odd, kind) for odd, kind in enumerate(("half", "whole"))]

        def whole_tile(qi, count, kind):
            start()
            run(qi, jnp.minimum(qi + 1, n_q - 1), 0, count, kind, True)
            finish(pl.multiple_of(qi * tq, tq))

        for nf in range(1, min(ATT_UNROLL, max_full + 1)):
            @pl.when(full == nf)
            def _(nf=nf):
                for qi, kind in pair:
                    whole_tile(qi, nf + 1, kind)

        if max_full < ATT_UNROLL:
            return 0

        for qi, kind in pair:
            next_q = jnp.minimum(qi + 1, n_q - 1)

            @pl.when(full >= ATT_UNROLL)
            def _(qi=qi, next_q=next_q):
                start()
                run(qi, next_q, 0, ATT_UNROLL, None, True)

            def group(g, _, qi=qi, next_q=next_q):
                run(qi, next_q, (g + 1) * ATT_UNROLL, ATT_UNROLL, None, False)
                return 0

            later = full - ATT_UNROLL
            lax.fori_loop(0, later // ATT_UNROLL, group, 0)
            for rem in range(ATT_UNROLL):
                @pl.when((later >= 0) & (later % ATT_UNROLL == rem))
                def _(rem=rem, qi=qi, next_q=next_q, kind=kind):
                    run(qi, next_q, full - rem, rem + 1, kind, False)
                    finish(pl.multiple_of(qi * tq, tq))

        return 0

    start()
    scores(0, 0, 1)
    scores(1, 0, 0)
    absorb(0, 1, "half")
    finish(0)
    start()
    scores(min(2, n_q - 1), 0, 2)
    absorb(0, 0, "whole")
    finish(tq)
    if n_q > 2:
        lax.fori_loop(1, n_q // 2, pair_step, 0)


def _attention(q, k, v, *, batch, seq, heads):
    qt = pl.BlockSpec((1, 1, seq // ATT_TILE, MLA_HEAD_PAD, ATT_TILE), lambda b, h: (b, h, 0, 0, 0))
    kk = pl.BlockSpec((1, 1, seq, MLA_HEAD_PAD), lambda b, h: (b, h, 0, 0))
    vv = pl.BlockSpec((1, 1, seq, MLA_V), lambda b, h: (b, h, 0, 0))
    vo = pl.BlockSpec((1, seq, MLA_V), lambda b, h: (b, 0, h))
    o = pl.pallas_call(
        _attn_body,
        grid=(batch, heads),
        in_specs=[qt, kk, vv],
        out_specs=vo,
        out_shape=jax.ShapeDtypeStruct((batch, seq, heads * MLA_V), BF16),
        scratch_shapes=[pltpu.VMEM((seq // ATT_KEYS, MLA_V + ATT_ONES, ATT_KEYS), BF16),
                        pltpu.VMEM((3, ATT_KEYS, ATT_TILE), F32),
                        pltpu.VMEM((3, 1, ATT_TILE), F32),
                        pltpu.VMEM((ATT_SPLIT, 1, ATT_TILE // ATT_SPLIT), F32),
                        pltpu.VMEM((ATT_SPLIT, MLA_V + ATT_ONES, ATT_TILE // ATT_SPLIT), F32)],
        compiler_params=_params(("parallel", "parallel")),
        name="attn",
    )(q, k, v)
    return o.reshape(batch * seq, heads * MLA_V)


def _merge_body(yh_ref, ym_ref, gate_ref, h_ref, wh_ref, wm_ref, wo_ref, o_ref):
    d = h_ref.shape[1]
    mix = (gate_ref[:, :d].astype(F32) * _dot(yh_ref[...], wh_ref[...])
           + gate_ref[:, d:].astype(F32) * _dot(ym_ref[...], wm_ref[...]))
    o_ref[...] = h_ref[...] + _dot(mix.astype(BF16), wo_ref[...])


def _merge(yh, ym, gates, h, w_h, w_m, w_o):
    t, d = h.shape
    tm = MERGE_TILE

    def row(n):
        return pl.BlockSpec((tm, n), lambda i: (i, 0))

    return pl.pallas_call(
        _merge_body,
        grid=(t // tm,),
        in_specs=[row(yh.shape[1]), row(ym.shape[1]), row(2 * d), row(d), _resident(w_h.shape),
                  _resident(w_m.shape), _resident(w_o.shape)],
        out_specs=row(d),
        out_shape=jax.ShapeDtypeStruct((t, d), F32),
        compiler_params=_params(("parallel",)),
        name="merge",
    )(yh, ym, gates, h, w_h.astype(BF16), w_m.astype(BF16), w_o.astype(BF16))


def kernel(x, positions, ffn1_norm, ffn1_w_in, ffn1_w_out, mix_norm, w_in, hg_lb_table, hg_out_norm, w_hg_branch, mla_q_lora_norm, w_q_up, mla_kv_lora_norm, w_kv_up, q_head_norm, k_head_norm, w_mla_branch, w_merge, b_merge, w_out, ffn2_norm, ffn2_w_in, ffn2_w_out, final_norm):
    batch, seq, d = x.shape
    depth = ffn1_norm.shape[0]
    hg_width = hg_lb_table.shape[1]
    heads = w_q_up.shape[2] // MLA_QK
    assert hg_out_norm.shape[1] == HG_HEAD and w_hg_branch.shape[1] == hg_width
    assert q_head_norm.shape[1] == MLA_QK and w_kv_up.shape[2] == heads * (MLA_NOPE + MLA_V)
    assert seq % max(HG_BLOCK, ATT_KEYS) == 0 and (batch * seq) % max(MERGE_TILE, FFN_TILE) == 0
    t = batch * seq
    pos = positions.reshape(t, 1)
    h = x.reshape(t, d)
    for l in range(depth):
        h1, u = _ffn(h, ffn1_norm[l], ffn1_w_in[l], ffn1_w_out[l], mix_norm[l], emit_h=True, norm_dtype=BF16)
        hq, lf, kk, hv, hg, gates, q, k, v = _proj(
            u, pos, w_in[l], w_merge[l], b_merge[l], hg_lb_table, mla_q_lora_norm[l], w_q_up[l],
            mla_kv_lora_norm[l], w_kv_up[l], q_head_norm[l], k_head_norm[l], layer=l, hg_width=hg_width,
            heads=heads, seq=seq)
        y_hg = _hgrn(hq, lf, kk, hv, hg, hg_out_norm[l], batch=batch, seq=seq)
        y_mla = _attention(q, k, v, batch=batch, seq=seq, heads=heads)
        h2 = _merge(y_hg, y_mla, gates, h1, w_hg_branch[l], w_mla_branch[l], w_out[l])
        (h,) = _ffn(h2, ffn2_norm[l], ffn2_w_in[l], ffn2_w_out[l], final_norm[l], emit_h=False, norm_dtype=F32)
    return h.reshape(batch, seq, d)
```

```python
import functools

import numpy as np
import jax
import jax.numpy as jnp
from jax import lax
from jax.experimental import pallas as pl
from jax.experimental.pallas import tpu as pltpu

F32 = jnp.float32
BF16 = jnp.bfloat16

EPS = 1e-6
CHUNK = 64
HG_HEAD = 128
MLA_NOPE = 128
MLA_ROPE = 64
MLA_V = 128
MLA_QK = MLA_NOPE + MLA_ROPE
MLA_HEAD_PAD = 256
ROPE_THETA = 10000.0
LOG2E = float(np.log2(np.e))

LANES = 128
MERGE_TILE = 512
FFN_TILE = 1024
PROJ_TILE = 256
FF_CHUNK = 512
HG_BLOCK = 1024
HG_HEADS_PER_STEP = 4
ATT_TILE = 512
ATT_KEYS = 2 * ATT_TILE
ATT_SPLIT = 2
ATT_BLOCK_TILES = 24
ATT_FOLD = 32
ATT_ONES = 16
VMEM_LIMIT = 56 * 1024 * 1024


def _rms(x, gain):
    return x * lax.rsqrt(jnp.mean(x * x, axis=-1, keepdims=True) + EPS) * gain


def _silu(x):
    return x * jax.nn.sigmoid(x)


def _dot(a, b):
    return jnp.dot(a, b, preferred_element_type=F32)


def _dot_nt(a, b):
    return lax.dot_general(a, b, (((1,), (1,)), ((), ())), preferred_element_type=F32)


def _dot_tn(a, b):
    return lax.dot_general(a, b, (((0,), (0,)), ((), ())), preferred_element_type=F32)


def _resident(shape):
    zeros = (0,) * len(shape)
    return pl.BlockSpec(shape, lambda *_: zeros, pipeline_mode=pl.Buffered(1))


def _params(semantics):
    return pltpu.CompilerParams(dimension_semantics=semantics, vmem_limit_bytes=VMEM_LIMIT)


def _ffn_body(x_ref, gin_ref, win_ref, wout_ref, gout_ref, *rest, emit_h):
    if emit_h:
        h_ref, n_ref, a_ref = rest
    else:
        n_ref, a_ref = rest
    d_ff = wout_ref.shape[0]
    x = x_ref[...]
    xn = _rms(x, gin_ref[...]).astype(BF16)
    for lo in range(0, d_ff, FF_CHUNK):
        hi = min(lo + FF_CHUNK, d_ff)
        gate = _dot(xn, win_ref[:, lo:hi])
        up = _dot(xn, win_ref[:, d_ff + lo:d_ff + hi])
        a_ref[:, lo:hi] = (_silu(gate) * up).astype(BF16)
    h = x + 0.5 * _dot(a_ref[...], wout_ref[...])
    if emit_h:
        h_ref[...] = h
    n_ref[...] = _rms(h, gout_ref[...]).astype(n_ref.dtype)


def _ffn(x, gin, w_in, w_out, gout, *, emit_h, norm_dtype):
    t, d = x.shape
    d_ff = w_out.shape[0]
    win = w_in.astype(BF16)
    tm = FFN_TILE
    row = pl.BlockSpec((tm, d), lambda i: (i, 0))
    out_shape = [jax.ShapeDtypeStruct((t, d), norm_dtype)]
    out_specs = [row]
    if emit_h:
        out_shape = [jax.ShapeDtypeStruct((t, d), F32)] + out_shape
        out_specs = [row] + out_specs
    return pl.pallas_call(
        functools.partial(_ffn_body, emit_h=emit_h),
        grid=(t // tm,),
        in_specs=[row, _resident((1, d)), _resident(win.shape), _resident((d_ff, d)), _resident((1, d))],
        out_specs=out_specs,
        out_shape=out_shape,
        scratch_shapes=[pltpu.VMEM((tm, d_ff), BF16)],
        compiler_params=_params(("parallel",)),
        name="ffn1" if emit_h else "ffn2",
    )(x, gin.reshape(1, d), win, w_out.astype(BF16), gout.reshape(1, d))


def _proj_body(u_ref, pos_ref, w_ref, wm_ref, bm_ref, tab_ref, gq_ref, wq_ref, gkv_ref, wkv_ref, qr_gain_ref,
               kn_gain_ref, kr_gain_ref, invf_ref,
               hq_ref, lf_ref, kk_ref, hv_ref, hg_ref, gate_ref, q_ref, k_ref, v_ref, *, layer, heads):
    w = hq_ref.shape[1]
    tm, d = u_ref.shape
    q_lora, kv_lora = wq_ref.shape[0], wkv_ref.shape[0]
    u = u_ref[...]

    c = _dot(u, w_ref[:, 4 * w:])
    hq_ref[...] = _silu(_dot(u, w_ref[:, 0:w])).astype(hq_ref.dtype)
    qraw = _dot(_rms(c[:, :q_lora], gq_ref[...]).astype(BF16), wq_ref[...])
    kvraw = _dot(_rms(c[:, q_lora:q_lora + kv_lora], gkv_ref[...]).astype(BF16), wkv_ref[...])
    kpe = c[:, q_lora + kv_lora:]

    tab = tab_ref[...]
    e = jnp.exp(tab - jnp.max(tab, axis=0, keepdims=True))
    lb = jnp.sum(e[:layer + 1], axis=0, keepdims=True) / jnp.sum(e, axis=0, keepdims=True)
    z = _dot(u, w_ref[:, w:2 * w])
    ez = jnp.exp(-jnp.abs(z))
    r = 1.0 / (1.0 + ez)
    a = ez * r
    pos = z >= 0
    lf_ref[...] = jnp.log(lb + (1.0 - lb) * jnp.where(pos, r, a)) * LOG2E
    kk_ref[...] = ((1.0 - lb) * jnp.where(pos, a, r)).astype(BF16)
    hv_ref[...] = _dot(u, w_ref[:, 2 * w:3 * w]).astype(BF16)
    hg_ref[...] = _silu(_dot(u, w_ref[:, 3 * w:4 * w])).astype(BF16)
    for j in range(2):
        gm = _dot(u, wm_ref[:, j * d:(j + 1) * d]) + bm_ref[:, j * d:(j + 1) * d]
        gate_ref[:, j * d:(j + 1) * d] = jax.nn.sigmoid(gm).astype(BF16)

    half = MLA_ROPE // 2
    rows4 = tm // 4
    lane4 = lax.broadcasted_iota(jnp.int32, (rows4, LANES), 1)
    posf = pos_ref[...].astype(F32)
    pos4 = [posf[g * rows4:(g + 1) * rows4] for g in range(4)]
    packed = jnp.where(lane4 < half, pos4[0],
                       jnp.where(lane4 < 2 * half, pos4[1],
                                 jnp.where(lane4 < 3 * half, pos4[2], pos4[3]))) * invf_ref[...]
    cos4, sin4 = jnp.cos(packed), jnp.sin(packed)
    spread = []
    for g in range(4):
        cg = pltpu.roll(cos4, LANES - half * g, 1) if g else cos4
        sg = pltpu.roll(sin4, LANES - half * g, 1) if g else sin4
        cc = jnp.where(lane4 < half, cg, pltpu.roll(cg, half, 1))
        ss = jnp.where(lane4 < 3 * half, pltpu.roll(sg, 2 * half, 1), pltpu.roll(sg, 3 * half, 1))
        spread.append(jnp.where(lane4 < 2 * half, cc, ss))
    cs = jnp.concatenate(spread, axis=0)
    first = lax.broadcasted_iota(jnp.int32, (tm, LANES), 1) < MLA_ROPE

    def rope(block, gain_cs):
        t = block * gain_cs
        return t + pltpu.roll(t, MLA_ROPE, 1)

    def sumsq(x):
        return jnp.sum(x * x, axis=-1, keepdims=True)

    k_rope = rope(kpe, kr_gain_ref[...] * cs)
    k_pe_ss = 0.5 * sumsq(kpe)
    q_gain_cs = qr_gain_ref[...] * cs
    scale = MLA_QK ** -0.5 * LOG2E
    nope_w = heads * MLA_NOPE
    for h in range(heads):
        lo, hi = h * MLA_NOPE, (h + 1) * MLA_NOPE
        qn = qraw[:, lo:hi]
        qr = qraw[:, nope_w + lo:nope_w + hi]
        sq = lax.rsqrt((sumsq(qn) + 0.5 * sumsq(qr)) / MLA_QK + EPS) * scale
        o = h * MLA_HEAD_PAD
        q_head = jnp.concatenate([qn * sq, rope(qr, q_gain_cs) * sq], axis=1)
        q_ref[0, h, 0] = q_head.T.astype(BF16)
        kn = kvraw[:, lo:hi]
        sk = lax.rsqrt((sumsq(kn) + k_pe_ss) / MLA_QK + EPS)
        k_ref[0, h, :, :MLA_NOPE] = (kn * sk * kn_gain_ref[...]).astype(BF16)
        k_ref[0, h, :, MLA_NOPE:] = jnp.where(first, k_rope * sk, 0.0).astype(BF16)
        v_ref[0, h] = kvraw[:, nope_w + h * MLA_V:nope_w + (h + 1) * MLA_V].astype(BF16)


def _rot_cols(w):
    half = w.shape[-1] // 2
    return jnp.concatenate([-w[..., half:], w[..., :half]], axis=-1)


def _rot_gain(g):
    half = g.shape[-1] // 2
    return jnp.concatenate([g[..., half:], g[..., :half]], axis=-1)


def _proj(u, pos, w_in, w_merge, b_merge, lb_table, gq, w_q_up, gkv, w_kv_up, q_head_gain, k_head_gain,
          *, layer, hg_width, heads, seq):
    t, d = u.shape
    q_lora, kv_lora = w_q_up.shape[0], w_kv_up.shape[0]
    w_in_ext = jnp.concatenate([w_in, _rot_cols(w_in[:, -MLA_ROPE:])], axis=1).astype(BF16)
    cols = w_in_ext.shape[1]
    assert cols == 4 * hg_width + q_lora + kv_lora + 2 * MLA_ROPE
    wq = w_q_up.reshape(q_lora, heads, MLA_QK)
    wq_rope = wq[:, :, MLA_NOPE:]
    wq_ext = jnp.concatenate(
        [wq[:, :, :MLA_NOPE].reshape(q_lora, heads * MLA_NOPE),
         jnp.concatenate([wq_rope, _rot_cols(wq_rope)], axis=-1).reshape(q_lora, heads * 2 * MLA_ROPE)],
        axis=1).astype(BF16)
    wkv = w_kv_up.reshape(kv_lora, heads, MLA_NOPE + MLA_V)
    wkv_ext = jnp.concatenate([wkv[:, :, :MLA_NOPE].reshape(kv_lora, heads * MLA_NOPE),
                               wkv[:, :, MLA_NOPE:].reshape(kv_lora, heads * MLA_V)], axis=1).astype(BF16)

    def rope_gain(g):
        return jnp.concatenate([g[MLA_NOPE:], _rot_gain(g[MLA_NOPE:])]).reshape(1, 2 * MLA_ROPE)

    inv_freq = ROPE_THETA ** (-jnp.arange(0, MLA_ROPE, 2, dtype=F32) / MLA_ROPE)
    invf = jnp.tile(inv_freq, 4).reshape(1, LANES)
    nope_gain = (q_head_gain[:MLA_NOPE] * k_head_gain[:MLA_NOPE]).reshape(1, MLA_NOPE)
    tm = PROJ_TILE

    def row(n):
        return pl.BlockSpec((tm, n), lambda i: (i, 0))

    def head_major(n):
        return pl.BlockSpec((1, heads, tm, n), lambda i: (i // per_seq, 0, i % per_seq, 0))

    per_seq = seq // tm
    per_att = ATT_TILE // tm
    qt_spec = pl.BlockSpec((1, heads, 1, MLA_HEAD_PAD, tm),
                           lambda i: (i // per_seq, 0, (i % per_seq) // per_att, 0, i % per_att))
    return pl.pallas_call(
        functools.partial(_proj_body, layer=layer, heads=heads),
        grid=(t // tm,),
        in_specs=[row(d), row(1), _resident((d, cols)), _resident((d, 2 * d)), _resident((1, 2 * d)),
                  _resident(lb_table.shape), _resident((1, q_lora)), _resident(wq_ext.shape),
                  _resident((1, kv_lora)), _resident(wkv_ext.shape), _resident((1, LANES)),
                  _resident((1, MLA_NOPE)), _resident((1, LANES)), _resident((1, LANES))],
        out_specs=[row(hg_width), row(hg_width), row(hg_width), row(hg_width), row(hg_width), row(2 * d),
                   qt_spec, head_major(MLA_HEAD_PAD), head_major(MLA_V)],
        out_shape=[jax.ShapeDtypeStruct((t, hg_width), BF16),
                   jax.ShapeDtypeStruct((t, hg_width), F32),
                   jax.ShapeDtypeStruct((t, hg_width), BF16),
                   jax.ShapeDtypeStruct((t, hg_width), BF16),
                   jax.ShapeDtypeStruct((t, hg_width), BF16),
                   jax.ShapeDtypeStruct((t, 2 * d), BF16),
                   jax.ShapeDtypeStruct((t // seq, heads, seq // ATT_TILE, MLA_HEAD_PAD, ATT_TILE), BF16),
                   jax.ShapeDtypeStruct((t // seq, heads, seq, MLA_HEAD_PAD), BF16),
                   jax.ShapeDtypeStruct((t // seq, heads, seq, MLA_V), BF16)],
        compiler_params=_params(("parallel",)),
        name="proj",
    )(u, pos, w_in_ext, w_merge.astype(BF16), b_merge.reshape(1, 2 * d), lb_table, gq.reshape(1, q_lora), wq_ext,
      gkv.reshape(1, kv_lora), wkv_ext, rope_gain(q_head_gain), nope_gain, rope_gain(k_head_gain), invf)


def _split3(x):
    hi = x.astype(BF16)
    r = x - hi.astype(F32)
    mid = r.astype(BF16)
    lo = (r - mid.astype(F32)).astype(BF16)
    return hi, mid, lo


def _hgrn_body(q_ref, lf_ref, kk_ref, v_ref, g_ref, gain_ref, tri_ref, y_ref, state_ref):
    @pl.when(pl.program_id(2) == 0)
    def _():
        state_ref[...] = jnp.zeros_like(state_ref)

    n = CHUNK
    tb = q_ref.shape[1]
    chunks = [slice(c * n, (c + 1) * n) for c in range(tb // n)]
    trow = lax.broadcasted_iota(jnp.int32, (n, n), 0)
    tcol = lax.broadcasted_iota(jnp.int32, (n, n), 1)
    tri = tri_ref[...]
    halves = [1 << i for i in range(n.bit_length() - 1)]
    pairs = {half: (trow // (2 * half) == tcol // (2 * half)) & (trow % (2 * half) >= half)
             & (tcol % (2 * half) < half) for half in halves}

    def ref_rows(x, size, row):
        blocks = x.reshape(tb // size, size, HG_HEAD)[:, row:row + 1, :]
        return jnp.broadcast_to(blocks, (tb // size, size, HG_HEAD)).reshape(tb, HG_HEAD)

    r8 = lax.broadcasted_iota(jnp.int32, (1, 8, HG_HEAD), 1)

    def head(q, lf, k, v, g, s_ref):
        lf3 = lf.reshape(tb // 8, 8, HG_HEAD)
        split = jnp.concatenate(_split3(lf), axis=1)
        cums = []
        for sl in chunks:
            r = _dot(tri, split[sl])
            cums.append(r[:, :HG_HEAD] + r[:, HG_HEAD:2 * HG_HEAD] + r[:, 2 * HG_HEAD:])
        cum = jnp.concatenate(cums, axis=0)
        diag = jnp.sum(q * k, axis=-1, keepdims=True)
        scores = [jnp.where(trow == tcol, diag[sl], 0.0) for sl in chunks]
        for half in halves:
            size = 2 * half
            if half >= 4:
                diff = pltpu.bitcast(cum - ref_rows(cum, size, half - 1), jnp.uint32)
                dec = pltpu.bitcast(diff | jnp.uint32(0x80000000), F32)
            elif half == 2:
                dec = jnp.where(r8 % 4 == 3, lf3 + pltpu.roll(lf3, 1, 1),
                                jnp.where(r8 % 4 == 2, lf3,
                                          jnp.where(r8 % 4 == 0, pltpu.roll(lf3, 7, 1), 0.0))).reshape(tb, HG_HEAD)
            else:
                dec = jnp.where(r8 % 2 == 1, lf3, 0.0).reshape(tb, HG_HEAD)
            e = jnp.exp2(dec)
            qe = (q * e).astype(BF16)
            ke = (k * e).astype(BF16)
            for c, sl in enumerate(chunks):
                scores[c] = scores[c] + jnp.where(pairs[half], _dot_nt(qe[sl], ke[sl]), 0.0)
        last = ref_rows(cum, n, n - 1)
        qdec = (q * jnp.exp2(cum)).astype(BF16)
        kdec = (k * jnp.exp2(last - cum)).astype(BF16)
        updates = [_dot_tn(kdec[sl], v[sl]) for sl in chunks]
        lasts = jnp.concatenate([cum[sl][n - 1:n, :] for sl in chunks], axis=0)
        decay_cols = jnp.concatenate([jnp.exp2(lasts)] * (HG_HEAD // len(chunks)), axis=0).T
        state = s_ref[...]
        outs = []
        for c, sl in enumerate(chunks):
            lhs = jnp.concatenate([qdec[sl], scores[c].astype(BF16)], axis=1)
            rhs = jnp.concatenate([state.astype(BF16), v[sl]], axis=0)
            outs.append(_dot(lhs, rhs))
            state = state * decay_cols[:, c:c + 1] + updates[c]
        s_ref[...] = state
        return _rms(jnp.concatenate(outs, axis=0), gain_ref[...]) * g

    for hd in range(q_ref.shape[2] // HG_HEAD):
        lanes = slice(hd * HG_HEAD, (hd + 1) * HG_HEAD)
        y_ref[0, :, lanes] = head(q_ref[0, :, lanes].astype(F32), lf_ref[0, :, lanes],
                                  kk_ref[0, :, lanes].astype(F32), v_ref[0, :, lanes],
                                  g_ref[0, :, lanes].astype(F32), state_ref.at[hd]).astype(y_ref.dtype)


def _hgrn(hq, lf, kk, v, g, out_gain, *, batch, seq):
    width = hq.shape[1]
    heads = width // HG_HEAD
    tb = HG_BLOCK
    tri = jnp.asarray(np.tril(np.ones((CHUNK, CHUNK), np.float32)), BF16)

    def r3(a):
        return a.reshape(batch, seq, width)

    blk = pl.BlockSpec((1, tb, HG_HEADS_PER_STEP * HG_HEAD), lambda b, h, j: (b, j, h))
    y = pl.pallas_call(
        _hgrn_body,
        grid=(batch, heads // HG_HEADS_PER_STEP, seq // tb),
        in_specs=[blk, blk, blk, blk, blk, _resident((1, HG_HEAD)), _resident((CHUNK, CHUNK))],
        out_specs=blk,
        out_shape=jax.ShapeDtypeStruct((batch, seq, width), BF16),
        scratch_shapes=[pltpu.VMEM((HG_HEADS_PER_STEP, HG_HEAD, HG_HEAD), F32)],
        compiler_params=_params(("parallel", "parallel", "arbitrary")),
        name="hgrn",
    )(r3(hq), r3(lf), r3(kk), r3(v), r3(g), out_gain.reshape(1, HG_HEAD), tri)
    return y.reshape(batch * seq, width)


def _attn_body(q_ref, k_ref, v_ref, o_ref, vt_ref, s_ref, mx_ref, m_ref, acc_ref):
    seq = k_ref.shape[2]
    tq = ATT_TILE
    tk = ATT_KEYS
    tqs = tq // ATT_SPLIT
    n_q = seq // tq
    n_k = seq // tk

    def transpose_v(j, _):
        vt_ref[j, :MLA_V] = v_ref[0, 0, pl.ds(pl.multiple_of(j * tk, tk), tk), :].astype(F32).T.astype(BF16)
        vt_ref[j, MLA_V:] = jnp.ones((ATT_ONES, tk), BF16)
        return 0

    lax.fori_loop(0, n_k, transpose_v, 0)

    key_chunk = lax.broadcasted_iota(jnp.int32, (tk, tqs), 0) // CHUNK
    qry_chunk = lax.broadcasted_iota(jnp.int32, (tk, tqs), 1) // CHUNK
    first_query = {"half": 0, "whole": tq}
    allowed = {kind: [key_chunk <= qry_chunk + (off + sub * tqs) // CHUNK for sub in range(ATT_SPLIT)]
               for kind, off in first_query.items()}

    def fold_rows(x, op):
        rows, cols = x.shape
        x = x.reshape(rows // ATT_FOLD, ATT_FOLD, cols)
        out = x[0]
        for i in range(1, rows // ATT_FOLD):
            out = op(out, x[i])
        return out

    def aligned(x, m):
        return x if isinstance(x, int) else pl.multiple_of(x, m)

    def scores(qt, kj, buf):
        k_tile = k_ref[0, 0, pl.ds(aligned(kj * tk, tk), tk), :]
        s = _dot(k_tile, q_ref[0, 0, qt])
        s_ref[buf] = s
        mx_ref[buf] = jnp.max(fold_rows(s, jnp.maximum), axis=0, keepdims=True)

    def start():
        m_ref[...] = jnp.full(m_ref.shape, -jnp.inf, F32)
        acc_ref[...] = jnp.zeros(acc_ref.shape, F32)

    def finish(base):
        for sub in range(ATT_SPLIT):
            acc = acc_ref[sub]
            o_ref[0, pl.ds(aligned(base + sub * tqs, tqs), tqs), :] = (
                (acc[:MLA_V] / acc[MLA_V:MLA_V + 1]).T.astype(o_ref.dtype))

    def absorb(kj, buf, kind):
        for sub in range(ATT_SPLIT):
            cols = slice(sub * tqs, (sub + 1) * tqs)
            keys = tk if kind == "full" else first_query[kind] + (sub + 1) * tqs
            vt_tile = vt_ref[kj, :, :keys]
            ss = s_ref[buf, :keys, cols]
            if kind == "full":
                tile_max = mx_ref[buf, :, cols]
            else:
                ss = jnp.where(allowed[kind][sub][:keys], ss, -jnp.inf)
                tile_max = jnp.max(fold_rows(ss, jnp.maximum), axis=0, keepdims=True)
            m = m_ref[sub]
            m_new = jnp.maximum(m, tile_max)
            alpha = jnp.exp2(m - m_new)
            p = jnp.exp2(ss - m_new)
            m_ref[sub] = m_new
            acc_ref[sub] = alpha * acc_ref[sub] + _dot(vt_tile, p.astype(BF16))

    def query_tile(qi):
        count = qi // 2 + 1
        start()
        for j in range(count):
            if j + 1 < count:
                scores(qi, j + 1, (j + 1) % 2)
            elif qi + 1 < n_q:
                scores(qi + 1, 0, home(qi + 1))
            absorb(j, home(qi) if j == 0 else j % 2, "full" if j + 1 < count else ("whole" if qi % 2 else "half"))
        finish(qi * tq)

    def home(qi):
        return {0: 1, 1: 0}.get(qi, 2)

    blocks, size = [[]], 0
    for qi in range(n_q):
        if blocks[-1] and size + qi // 2 + 1 > ATT_BLOCK_TILES:
            blocks.append([])
            size = 0
        blocks[-1].append(qi)
        size += qi // 2 + 1

    def block_step(i, _):
        for j, tiles in enumerate(blocks):
            @pl.when(i == j)
            def _(tiles=tiles):
                for qi in tiles:
                    query_tile(qi)
        return 0

    scores(0, 0, home(0))
    lax.fori_loop(0, len(blocks), block_step, 0)


def _attention(q, k, v, *, batch, seq, heads):
    qt = pl.BlockSpec((1, 1, seq // ATT_TILE, MLA_HEAD_PAD, ATT_TILE), lambda b, h: (b, h, 0, 0, 0))
    kk = pl.BlockSpec((1, 1, seq, MLA_HEAD_PAD), lambda b, h: (b, h, 0, 0))
    vv = pl.BlockSpec((1, 1, seq, MLA_V), lambda b, h: (b, h, 0, 0))
    vo = pl.BlockSpec((1, seq, MLA_V), lambda b, h: (b, 0, h))
    o = pl.pallas_call(
        _attn_body,
        grid=(batch, heads),
        in_specs=[qt, kk, vv],
        out_specs=vo,
        out_shape=jax.ShapeDtypeStruct((batch, seq, heads * MLA_V), BF16),
        scratch_shapes=[pltpu.VMEM((seq // ATT_KEYS, MLA_V + ATT_ONES, ATT_KEYS), BF16),
                        pltpu.VMEM((3, ATT_KEYS, ATT_TILE), F32),
                        pltpu.VMEM((3, 1, ATT_TILE), F32),
                        pltpu.VMEM((ATT_SPLIT, 1, ATT_TILE // ATT_SPLIT), F32),
                        pltpu.VMEM((ATT_SPLIT, MLA_V + ATT_ONES, ATT_TILE // ATT_SPLIT), F32)],
        compiler_params=_params(("parallel", "parallel")),
        name="attn",
    )(q, k, v)
    return o.reshape(batch * seq, heads * MLA_V)


def _merge_body(yh_ref, ym_ref, gate_ref, h_ref, wh_ref, wm_ref, wo_ref, o_ref):
    d = h_ref.shape[1]
    mix = (gate_ref[:, :d].astype(F32) * _dot(yh_ref[...], wh_ref[...])
           + gate_ref[:, d:].astype(F32) * _dot(ym_ref[...], wm_ref[...]))
    o_ref[...] = h_ref[...] + _dot(mix.astype(BF16), wo_ref[...])


def _merge(yh, ym, gates, h, w_h, w_m, w_o):
    t, d = h.shape
    tm = MERGE_TILE

    def row(n):
        return pl.BlockSpec((tm, n), lambda i: (i, 0))

    return pl.pallas_call(
        _merge_body,
        grid=(t // tm,),
        in_specs=[row(yh.shape[1]), row(ym.shape[1]), row(2 * d), row(d), _resident(w_h.shape),
                  _resident(w_m.shape), _resident(w_o.shape)],
        out_specs=row(d),
        out_shape=jax.ShapeDtypeStruct((t, d), F32),
        compiler_params=_params(("parallel",)),
        name="merge",
    )(yh, ym, gates, h, w_h.astype(BF16), w_m.astype(BF16), w_o.astype(BF16))


def kernel(x, positions, ffn1_norm, ffn1_w_in, ffn1_w_out, mix_norm, w_in, hg_lb_table, hg_out_norm, w_hg_branch, mla_q_lora_norm, w_q_up, mla_kv_lora_norm, w_kv_up, q_head_norm, k_head_norm, w_mla_branch, w_merge, b_merge, w_out, ffn2_norm, ffn2_w_in, ffn2_w_out, final_norm):
    batch, seq, d = x.shape
    depth = ffn1_norm.shape[0]
    hg_width = hg_lb_table.shape[1]
    heads = w_q_up.shape[2] // MLA_QK
    assert hg_out_norm.shape[1] == HG_HEAD and w_hg_branch.shape[1] == hg_width
    assert q_head_norm.shape[1] == MLA_QK and w_kv_up.shape[2] == heads * (MLA_NOPE + MLA_V)
    assert seq % max(HG_BLOCK, ATT_KEYS) == 0 and (batch * seq) % max(MERGE_TILE, FFN_TILE) == 0
    t = batch * seq
    pos = positions.reshape(t, 1)
    h = x.reshape(t, d)
    for l in range(depth):
        h1, u = _ffn(h, ffn1_norm[l], ffn1_w_in[l], ffn1_w_out[l], mix_norm[l], emit_h=True, norm_dtype=BF16)
        hq, lf, kk, hv, hg, gates, q, k, v = _proj(
            u, pos, w_in[l], w_merge[l], b_merge[l], hg_lb_table, mla_q_lora_norm[l], w_q_up[l],
            mla_kv_lora_norm[l], w_kv_up[l], q_head_norm[l], k_head_norm[l], layer=l, hg_width=hg_width,
            heads=heads, seq=seq)
        y_hg = _hgrn(hq, lf, kk, hv, hg, hg_out_norm[l], batch=batch, seq=seq)
        y_mla = _attention(q, k, v, batch=batch, seq=seq, heads=heads)
        h2 = _merge(y_hg, y_mla, gates, h1, w_hg_branch[l], w_mla_branch[l], w_out[l])
        (h,) = _ffn(h2, ffn2_norm[l], ffn2_w_in[l], ffn2_w_out[l], final_norm[l], emit_h=False, norm_dtype=F32)
    return h.reshape(batch, seq, d)
```

```python
import functools

import numpy as np
import jax
import jax.numpy as jnp
from jax import lax
from jax.experimental import pallas as pl
from jax.experimental.pallas import tpu as pltpu

F32 = jnp.float32
BF16 = jnp.bfloat16

EPS = 1e-6
CHUNK = 64
HG_HEAD = 128
MLA_NOPE = 128
MLA_ROPE = 64
MLA_V = 128
MLA_QK = MLA_NOPE + MLA_ROPE
MLA_HEAD_PAD = 256
ROPE_THETA = 10000.0
LOG2E = float(np.log2(np.e))

LANES = 128
MERGE_TILE = 1024
FFN_TILE = 1024
PROJ_TILE = 256
FF_CHUNK = 512
HG_BLOCK = 1024
HG_HEADS_PER_STEP = 4
ATT_TILE = 512
ATT_KEYS = 2 * ATT_TILE
ATT_SPLIT = 2
ATT_BLOCK_TILES = 36
ATT_FOLD = 32
ATT_ONES = 16
VMEM_LIMIT = 56 * 1024 * 1024


def _rms(x, gain):
    return x * lax.rsqrt(jnp.mean(x * x, axis=-1, keepdims=True) + EPS) * gain


def _silu(x):
    return x * jax.nn.sigmoid(x)


def _dot(a, b):
    return jnp.dot(a, b, preferred_element_type=F32)


def _dot_nt(a, b):
    return lax.dot_general(a, b, (((1,), (1,)), ((), ())), preferred_element_type=F32)


def _dot_tn(a, b):
    return lax.dot_general(a, b, (((0,), (0,)), ((), ())), preferred_element_type=F32)


def _resident(shape):
    zeros = (0,) * len(shape)
    return pl.BlockSpec(shape, lambda *_: zeros, pipeline_mode=pl.Buffered(1))


def _params(semantics):
    return pltpu.CompilerParams(dimension_semantics=semantics, vmem_limit_bytes=VMEM_LIMIT)


def _ffn_body(x_ref, gin_ref, win_ref, wout_ref, gout_ref, *rest, emit_h):
    if emit_h:
        h_ref, n_ref, a_ref = rest
    else:
        n_ref, a_ref = rest
    d_ff = wout_ref.shape[0]
    x = x_ref[...]
    xn = _rms(x, gin_ref[...]).astype(BF16)
    for lo in range(0, d_ff, FF_CHUNK):
        hi = min(lo + FF_CHUNK, d_ff)
        gate = _dot(xn, win_ref[:, lo:hi])
        up = _dot(xn, win_ref[:, d_ff + lo:d_ff + hi])
        a_ref[:, lo:hi] = (_silu(gate) * up).astype(BF16)
    h = x + 0.5 * _dot(a_ref[...], wout_ref[...])
    if emit_h:
        h_ref[...] = h
    n_ref[...] = _rms(h, gout_ref[...]).astype(n_ref.dtype)


def _ffn(x, gin, w_in, w_out, gout, *, emit_h, norm_dtype):
    t, d = x.shape
    d_ff = w_out.shape[0]
    win = w_in.astype(BF16)
    tm = FFN_TILE
    row = pl.BlockSpec((tm, d), lambda i: (i, 0))
    out_shape = [jax.ShapeDtypeStruct((t, d), norm_dtype)]
    out_specs = [row]
    if emit_h:
        out_shape = [jax.ShapeDtypeStruct((t, d), F32)] + out_shape
        out_specs = [row] + out_specs
    return pl.pallas_call(
        functools.partial(_ffn_body, emit_h=emit_h),
        grid=(t // tm,),
        in_specs=[row, _resident((1, d)), _resident(win.shape), _resident((d_ff, d)), _resident((1, d))],
        out_specs=out_specs,
        out_shape=out_shape,
        scratch_shapes=[pltpu.VMEM((tm, d_ff), BF16)],
        compiler_params=_params(("parallel",)),
        name="ffn1" if emit_h else "ffn2",
    )(x, gin.reshape(1, d), win, w_out.astype(BF16), gout.reshape(1, d))


def _proj_body(u_ref, pos_ref, w_ref, wm_ref, bm_ref, tab_ref, gq_ref, wq_ref, gkv_ref, wkv_ref, qr_gain_ref,
               kn_gain_ref, kr_gain_ref, invf_ref,
               hq_ref, lf_ref, kk_ref, hv_ref, hg_ref, gate_ref, q_ref, k_ref, v_ref, *, layer, heads):
    w = hq_ref.shape[1]
    tm, d = u_ref.shape
    q_lora, kv_lora = wq_ref.shape[0], wkv_ref.shape[0]
    u = u_ref[...]

    c = _dot(u, w_ref[:, 4 * w:])
    hq_ref[...] = _silu(_dot(u, w_ref[:, 0:w])).astype(hq_ref.dtype)
    qraw = _dot(_rms(c[:, :q_lora], gq_ref[...]).astype(BF16), wq_ref[...])
    kvraw = _dot(_rms(c[:, q_lora:q_lora + kv_lora], gkv_ref[...]).astype(BF16), wkv_ref[...])
    kpe = c[:, q_lora + kv_lora:]

    tab = tab_ref[...]
    e = jnp.exp(tab - jnp.max(tab, axis=0, keepdims=True))
    lb = jnp.sum(e[:layer + 1], axis=0, keepdims=True) / jnp.sum(e, axis=0, keepdims=True)
    z = _dot(u, w_ref[:, w:2 * w])
    ez = jnp.exp(-jnp.abs(z))
    r = 1.0 / (1.0 + ez)
    a = ez * r
    pos = z >= 0
    lf_ref[...] = jnp.log(lb + (1.0 - lb) * jnp.where(pos, r, a)) * LOG2E
    kk_ref[...] = ((1.0 - lb) * jnp.where(pos, a, r)).astype(BF16)
    hv_ref[...] = _dot(u, w_ref[:, 2 * w:3 * w]).astype(BF16)
    hg_ref[...] = _silu(_dot(u, w_ref[:, 3 * w:4 * w])).astype(BF16)
    for j in range(2):
        gm = _dot(u, wm_ref[:, j * d:(j + 1) * d]) + bm_ref[:, j * d:(j + 1) * d]
        gate_ref[:, j * d:(j + 1) * d] = jax.nn.sigmoid(gm).astype(BF16)

    half = MLA_ROPE // 2
    rows4 = tm // 4
    lane4 = lax.broadcasted_iota(jnp.int32, (rows4, LANES), 1)
    posf = pos_ref[...].astype(F32)
    pos4 = [posf[g * rows4:(g + 1) * rows4] for g in range(4)]
    packed = jnp.where(lane4 < half, pos4[0],
                       jnp.where(lane4 < 2 * half, pos4[1],
                                 jnp.where(lane4 < 3 * half, pos4[2], pos4[3]))) * invf_ref[...]
    cos4, sin4 = jnp.cos(packed), jnp.sin(packed)
    spread = []
    for g in range(4):
        cg = pltpu.roll(cos4, LANES - half * g, 1) if g else cos4
        sg = pltpu.roll(sin4, LANES - half * g, 1) if g else sin4
        cc = jnp.where(lane4 < half, cg, pltpu.roll(cg, half, 1))
        ss = jnp.where(lane4 < 3 * half, pltpu.roll(sg, 2 * half, 1), pltpu.roll(sg, 3 * half, 1))
        spread.append(jnp.where(lane4 < 2 * half, cc, ss))
    cs = jnp.concatenate(spread, axis=0)
    first = lax.broadcasted_iota(jnp.int32, (tm, LANES), 1) < MLA_ROPE

    def rope(block, gain_cs):
        t = block * gain_cs
        return t + pltpu.roll(t, MLA_ROPE, 1)

    def sumsq(x):
        return jnp.sum(x * x, axis=-1, keepdims=True)

    k_rope = rope(kpe, kr_gain_ref[...] * cs)
    k_pe_ss = 0.5 * sumsq(kpe)
    q_gain_cs = qr_gain_ref[...] * cs
    scale = MLA_QK ** -0.5 * LOG2E
    nope_w = heads * MLA_NOPE
    for h in range(heads):
        lo, hi = h * MLA_NOPE, (h + 1) * MLA_NOPE
        qn = qraw[:, lo:hi]
        qr = qraw[:, nope_w + lo:nope_w + hi]
        sq = lax.rsqrt((sumsq(qn) + 0.5 * sumsq(qr)) / MLA_QK + EPS) * scale
        o = h * MLA_HEAD_PAD
        q_head = jnp.concatenate([qn * sq, rope(qr, q_gain_cs) * sq], axis=1)
        q_ref[0, h, 0] = q_head.T.astype(BF16)
        kn = kvraw[:, lo:hi]
        sk = lax.rsqrt((sumsq(kn) + k_pe_ss) / MLA_QK + EPS)
        k_ref[0, h, :, :MLA_NOPE] = (kn * sk * kn_gain_ref[...]).astype(BF16)
        k_ref[0, h, :, MLA_NOPE:] = jnp.where(first, k_rope * sk, 0.0).astype(BF16)
        v_ref[0, h] = kvraw[:, nope_w + h * MLA_V:nope_w + (h + 1) * MLA_V].astype(BF16)


def _rot_cols(w):
    half = w.shape[-1] // 2
    return jnp.concatenate([-w[..., half:], w[..., :half]], axis=-1)


def _rot_gain(g):
    half = g.shape[-1] // 2
    return jnp.concatenate([g[..., half:], g[..., :half]], axis=-1)


def _proj(u, pos, w_in, w_merge, b_merge, lb_table, gq, w_q_up, gkv, w_kv_up, q_head_gain, k_head_gain,
          *, layer, hg_width, heads, seq):
    t, d = u.shape
    q_lora, kv_lora = w_q_up.shape[0], w_kv_up.shape[0]
    w_in_ext = jnp.concatenate([w_in, _rot_cols(w_in[:, -MLA_ROPE:])], axis=1).astype(BF16)
    cols = w_in_ext.shape[1]
    assert cols == 4 * hg_width + q_lora + kv_lora + 2 * MLA_ROPE
    wq = w_q_up.reshape(q_lora, heads, MLA_QK)
    wq_rope = wq[:, :, MLA_NOPE:]
    wq_ext = jnp.concatenate(
        [wq[:, :, :MLA_NOPE].reshape(q_lora, heads * MLA_NOPE),
         jnp.concatenate([wq_rope, _rot_cols(wq_rope)], axis=-1).reshape(q_lora, heads * 2 * MLA_ROPE)],
        axis=1).astype(BF16)
    wkv = w_kv_up.reshape(kv_lora, heads, MLA_NOPE + MLA_V)
    wkv_ext = jnp.concatenate([wkv[:, :, :MLA_NOPE].reshape(kv_lora, heads * MLA_NOPE),
                               wkv[:, :, MLA_NOPE:].reshape(kv_lora, heads * MLA_V)], axis=1).astype(BF16)

    def rope_gain(g):
        return jnp.concatenate([g[MLA_NOPE:], _rot_gain(g[MLA_NOPE:])]).reshape(1, 2 * MLA_ROPE)

    inv_freq = ROPE_THETA ** (-jnp.arange(0, MLA_ROPE, 2, dtype=F32) / MLA_ROPE)
    invf = jnp.tile(inv_freq, 4).reshape(1, LANES)
    nope_gain = (q_head_gain[:MLA_NOPE] * k_head_gain[:MLA_NOPE]).reshape(1, MLA_NOPE)
    tm = PROJ_TILE

    def row(n):
        return pl.BlockSpec((tm, n), lambda i: (i, 0))

    def head_major(n):
        return pl.BlockSpec((1, heads, tm, n), lambda i: (i // per_seq, 0, i % per_seq, 0))

    per_seq = seq // tm
    per_att = ATT_TILE // tm
    qt_spec = pl.BlockSpec((1, heads, 1, MLA_HEAD_PAD, tm),
                           lambda i: (i // per_seq, 0, (i % per_seq) // per_att, 0, i % per_att))
    return pl.pallas_call(
        functools.partial(_proj_body, layer=layer, heads=heads),
        grid=(t // tm,),
        in_specs=[row(d), row(1), _resident((d, cols)), _resident((d, 2 * d)), _resident((1, 2 * d)),
                  _resident(lb_table.shape), _resident((1, q_lora)), _resident(wq_ext.shape),
                  _resident((1, kv_lora)), _resident(wkv_ext.shape), _resident((1, LANES)),
                  _resident((1, MLA_NOPE)), _resident((1, LANES)), _resident((1, LANES))],
        out_specs=[row(hg_width), row(hg_width), row(hg_width), row(hg_width), row(hg_width), row(2 * d),
                   qt_spec, head_major(MLA_HEAD_PAD), head_major(MLA_V)],
        out_shape=[jax.ShapeDtypeStruct((t, hg_width), BF16),
                   jax.ShapeDtypeStruct((t, hg_width), F32),
                   jax.ShapeDtypeStruct((t, hg_width), BF16),
                   jax.ShapeDtypeStruct((t, hg_width), BF16),
                   jax.ShapeDtypeStruct((t, hg_width), BF16),
                   jax.ShapeDtypeStruct((t, 2 * d), BF16),
                   jax.ShapeDtypeStruct((t // seq, heads, seq // ATT_TILE, MLA_HEAD_PAD, ATT_TILE), BF16),
                   jax.ShapeDtypeStruct((t // seq, heads, seq, MLA_HEAD_PAD), BF16),
                   jax.ShapeDtypeStruct((t // seq, heads, seq, MLA_V), BF16)],
        compiler_params=_params(("parallel",)),
        name="proj",
    )(u, pos, w_in_ext, w_merge.astype(BF16), b_merge.reshape(1, 2 * d), lb_table, gq.reshape(1, q_lora), wq_ext,
      gkv.reshape(1, kv_lora), wkv_ext, rope_gain(q_head_gain), nope_gain, rope_gain(k_head_gain), invf)


def _split3(x):
    hi = x.astype(BF16)
    r = x - hi.astype(F32)
    mid = r.astype(BF16)
    lo = (r - mid.astype(F32)).astype(BF16)
    return hi, mid, lo


def _hgrn_body(q_ref, lf_ref, kk_ref, v_ref, g_ref, gain_ref, tri_ref, y_ref, state_ref):
    @pl.when(pl.program_id(2) == 0)
    def _():
        state_ref[...] = jnp.zeros_like(state_ref)

    n = CHUNK
    tb = q_ref.shape[1]
    chunks = [slice(c * n, (c + 1) * n) for c in range(tb // n)]
    trow = lax.broadcasted_iota(jnp.int32, (n, n), 0)
    tcol = lax.broadcasted_iota(jnp.int32, (n, n), 1)
    tri = tri_ref[...]
    halves = [1 << i for i in range(n.bit_length() - 1)]
    pairs = {half: (trow // (2 * half) == tcol // (2 * half)) & (trow % (2 * half) >= half)
             & (tcol % (2 * half) < half) for half in halves}

    def ref_rows(x, size, row):
        blocks = x.reshape(tb // size, size, HG_HEAD)[:, row:row + 1, :]
        return jnp.broadcast_to(blocks, (tb // size, size, HG_HEAD)).reshape(tb, HG_HEAD)

    r8 = lax.broadcasted_iota(jnp.int32, (1, 8, HG_HEAD), 1)

    def head(q, lf, k, v, g, s_ref):
        lf3 = lf.reshape(tb // 8, 8, HG_HEAD)
        split = jnp.concatenate(_split3(lf), axis=1)
        cums = []
        for sl in chunks:
            r = _dot(tri, split[sl])
            cums.append(r[:, :HG_HEAD] + r[:, HG_HEAD:2 * HG_HEAD] + r[:, 2 * HG_HEAD:])
        cum = jnp.concatenate(cums, axis=0)
        diag = jnp.sum(q * k, axis=-1, keepdims=True)
        scores = [jnp.where(trow == tcol, diag[sl], 0.0) for sl in chunks]
        for half in halves:
            size = 2 * half
            if half >= 4:
                diff = pltpu.bitcast(cum - ref_rows(cum, size, half - 1), jnp.uint32)
                dec = pltpu.bitcast(diff | jnp.uint32(0x80000000), F32)
            elif half == 2:
                dec = jnp.where(r8 % 4 == 3, lf3 + pltpu.roll(lf3, 1, 1),
                                jnp.where(r8 % 4 == 2, lf3,
                                          jnp.where(r8 % 4 == 0, pltpu.roll(lf3, 7, 1), 0.0))).reshape(tb, HG_HEAD)
            else:
                dec = jnp.where(r8 % 2 == 1, lf3, 0.0).reshape(tb, HG_HEAD)
            e = jnp.exp2(dec)
            qe = (q * e).astype(BF16)
            ke = (k * e).astype(BF16)
            for c, sl in enumerate(chunks):
                scores[c] = scores[c] + jnp.where(pairs[half], _dot_nt(qe[sl], ke[sl]), 0.0)
        last = ref_rows(cum, n, n - 1)
        qdec = (q * jnp.exp2(cum)).astype(BF16)
        kdec = (k * jnp.exp2(last - cum)).astype(BF16)
        updates = [_dot_tn(kdec[sl], v[sl]) for sl in chunks]
        lasts = jnp.concatenate([cum[sl][n - 1:n, :] for sl in chunks], axis=0)
        decay_cols = jnp.concatenate([jnp.exp2(lasts)] * (HG_HEAD // len(chunks)), axis=0).T
        state = s_ref[...]
        outs = []
        for c, sl in enumerate(chunks):
            lhs = jnp.concatenate([qdec[sl], scores[c].astype(BF16)], axis=1)
            rhs = jnp.concatenate([state.astype(BF16), v[sl]], axis=0)
            outs.append(_dot(lhs, rhs))
            state = state * decay_cols[:, c:c + 1] + updates[c]
        s_ref[...] = state
        return _rms(jnp.concatenate(outs, axis=0), gain_ref[...]) * g

    for hd in range(q_ref.shape[2] // HG_HEAD):
        lanes = slice(hd * HG_HEAD, (hd + 1) * HG_HEAD)
        y_ref[0, :, lanes] = head(q_ref[0, :, lanes].astype(F32), lf_ref[0, :, lanes],
                                  kk_ref[0, :, lanes].astype(F32), v_ref[0, :, lanes],
                                  g_ref[0, :, lanes].astype(F32), state_ref.at[hd]).astype(y_ref.dtype)


def _hgrn(hq, lf, kk, v, g, out_gain, *, batch, seq):
    width = hq.shape[1]
    heads = width // HG_HEAD
    tb = HG_BLOCK
    tri = jnp.asarray(np.tril(np.ones((CHUNK, CHUNK), np.float32)), BF16)

    def r3(a):
        return a.reshape(batch, seq, width)

    blk = pl.BlockSpec((1, tb, HG_HEADS_PER_STEP * HG_HEAD), lambda b, h, j: (b, j, h))
    y = pl.pallas_call(
        _hgrn_body,
        grid=(batch, heads // HG_HEADS_PER_STEP, seq // tb),
        in_specs=[blk, blk, blk, blk, blk, _resident((1, HG_HEAD)), _resident((CHUNK, CHUNK))],
        out_specs=blk,
        out_shape=jax.ShapeDtypeStruct((batch, seq, width), BF16),
        scratch_shapes=[pltpu.VMEM((HG_HEADS_PER_STEP, HG_HEAD, HG_HEAD), F32)],
        compiler_params=_params(("parallel", "parallel", "arbitrary")),
        name="hgrn",
    )(r3(hq), r3(lf), r3(kk), r3(v), r3(g), out_gain.reshape(1, HG_HEAD), tri)
    return y.reshape(batch * seq, width)


def _attn_body(q_ref, k_ref, v_ref, o_ref, vt_ref, s_ref, mx_ref, m_ref, acc_ref):
    seq = k_ref.shape[2]
    tq = ATT_TILE
    tk = ATT_KEYS
    tqs = tq // ATT_SPLIT
    n_q = seq // tq
    n_k = seq // tk

    for j in range(n_k):
        vt_ref[j, :MLA_V] = v_ref[0, 0, j * tk:(j + 1) * tk, :].astype(F32).T.astype(BF16)
        vt_ref[j, MLA_V:] = jnp.ones((ATT_ONES, tk), BF16)

    key_chunk = lax.broadcasted_iota(jnp.int32, (tk, tqs), 0) // CHUNK
    qry_chunk = lax.broadcasted_iota(jnp.int32, (tk, tqs), 1) // CHUNK
    first_query = {"half": 0, "whole": tq}
    allowed = {kind: [key_chunk <= qry_chunk + (off + sub * tqs) // CHUNK for sub in range(ATT_SPLIT)]
               for kind, off in first_query.items()}

    def fold_rows(x, op):
        rows, cols = x.shape
        x = x.reshape(rows // ATT_FOLD, ATT_FOLD, cols)
        out = x[0]
        for i in range(1, rows // ATT_FOLD):
            out = op(out, x[i])
        return out

    def aligned(x, m):
        return x if isinstance(x, int) else pl.multiple_of(x, m)

    def scores(qt, kj, buf):
        k_tile = k_ref[0, 0, pl.ds(aligned(kj * tk, tk), tk), :]
        s = _dot(k_tile, q_ref[0, 0, qt])
        s_ref[buf] = s
        mx_ref[buf] = jnp.max(fold_rows(s, jnp.maximum), axis=0, keepdims=True)

    def start():
        m_ref[...] = jnp.full(m_ref.shape, -jnp.inf, F32)
        acc_ref[...] = jnp.zeros(acc_ref.shape, F32)

    def finish(base):
        for sub in range(ATT_SPLIT):
            acc = acc_ref[sub]
            o_ref[0, pl.ds(aligned(base + sub * tqs, tqs), tqs), :] = (
                (acc[:MLA_V] / acc[MLA_V:MLA_V + 1]).T.astype(o_ref.dtype))

    def absorb(kj, buf, kind):
        for sub in range(ATT_SPLIT):
            cols = slice(sub * tqs, (sub + 1) * tqs)
            keys = tk if kind == "full" else first_query[kind] + (sub + 1) * tqs
            vt_tile = vt_ref[kj, :, :keys]
            ss = s_ref[buf, :keys, cols]
            if kind == "full":
                tile_max = mx_ref[buf, :, cols]
            else:
                ss = jnp.where(allowed[kind][sub][:keys], ss, -jnp.inf)
                tile_max = jnp.max(fold_rows(ss, jnp.maximum), axis=0, keepdims=True)
            m = m_ref[sub]
            m_new = jnp.maximum(m, tile_max)
            alpha = jnp.exp2(m - m_new)
            p = jnp.exp2(ss - m_new)
            m_ref[sub] = m_new
            acc_ref[sub] = alpha * acc_ref[sub] + _dot(vt_tile, p.astype(BF16))

    def query_tile(qi):
        count = qi // 2 + 1
        start()
        for j in range(count):
            if j + 1 < count:
                scores(qi, j + 1, (j + 1) % 2)
            elif qi + 1 < n_q:
                scores(qi + 1, 0, home(qi + 1))
            absorb(j, home(qi) if j == 0 else j % 2, "full" if j + 1 < count else ("whole" if qi % 2 else "half"))
        finish(qi * tq)

    def home(qi):
        return {0: 1, 1: 0}.get(qi, 2)

    blocks, size = [[]], 0
    for qi in range(n_q):
        if blocks[-1] and size + qi // 2 + 1 > ATT_BLOCK_TILES:
            blocks.append([])
            size = 0
        blocks[-1].append(qi)
        size += qi // 2 + 1

    def block_step(i, _):
        for j, tiles in enumerate(blocks):
            @pl.when(i == j)
            def _(tiles=tiles):
                for qi in tiles:
                    query_tile(qi)
        return 0

    scores(0, 0, home(0))
    lax.fori_loop(0, len(blocks), block_step, 0)


def _attention(q, k, v, *, batch, seq, heads):
    qt = pl.BlockSpec((1, 1, seq // ATT_TILE, MLA_HEAD_PAD, ATT_TILE), lambda b, h: (b, h, 0, 0, 0))
    kk = pl.BlockSpec((1, 1, seq, MLA_HEAD_PAD), lambda b, h: (b, h, 0, 0))
    vv = pl.BlockSpec((1, 1, seq, MLA_V), lambda b, h: (b, h, 0, 0))
    vo = pl.BlockSpec((1, seq, MLA_V), lambda b, h: (b, 0, h))
    o = pl.pallas_call(
        _attn_body,
        grid=(batch, heads),
        in_specs=[qt, kk, vv],
        out_specs=vo,
        out_shape=jax.ShapeDtypeStruct((batch, seq, heads * MLA_V), BF16),
        scratch_shapes=[pltpu.VMEM((seq // ATT_KEYS, MLA_V + ATT_ONES, ATT_KEYS), BF16),
                        pltpu.VMEM((3, ATT_KEYS, ATT_TILE), F32),
                        pltpu.VMEM((3, 1, ATT_TILE), F32),
                        pltpu.VMEM((ATT_SPLIT, 1, ATT_TILE // ATT_SPLIT), F32),
                        pltpu.VMEM((ATT_SPLIT, MLA_V + ATT_ONES, ATT_TILE // ATT_SPLIT), F32)],
        compiler_params=_params(("parallel", "parallel")),
        name="attn",
    )(q, k, v)
    return o.reshape(batch * seq, heads * MLA_V)


def _merge_body(yh_ref, ym_ref, gate_ref, h_ref, wh_ref, wm_ref, wo_ref, o_ref):
    d = h_ref.shape[1]
    mix = (gate_ref[:, :d].astype(F32) * _dot(yh_ref[...], wh_ref[...])
           + gate_ref[:, d:].astype(F32) * _dot(ym_ref[...], wm_ref[...]))
    o_ref[...] = h_ref[...] + _dot(mix.astype(BF16), wo_ref[...])


def _merge(yh, ym, gates, h, w_h, w_m, w_o):
    t, d = h.shape
    tm = MERGE_TILE

    def row(n):
        return pl.BlockSpec((tm, n), lambda i: (i, 0))

    return pl.pallas_call(
        _merge_body,
        grid=(t // tm,),
        in_specs=[row(yh.shape[1]), row(ym.shape[1]), row(2 * d), row(d), _resident(w_h.shape),
                  _resident(w_m.shape), _resident(w_o.shape)],
        out_specs=row(d),
        out_shape=jax.ShapeDtypeStruct((t, d), F32),
        compiler_params=_params(("parallel",)),
        name="merge",
    )(yh, ym, gates, h, w_h.astype(BF16), w_m.astype(BF16), w_o.astype(BF16))


def kernel(x, positions, ffn1_norm, ffn1_w_in, ffn1_w_out, mix_norm, w_in, hg_lb_table, hg_out_norm, w_hg_branch, mla_q_lora_norm, w_q_up, mla_kv_lora_norm, w_kv_up, q_head_norm, k_head_norm, w_mla_branch, w_merge, b_merge, w_out, ffn2_norm, ffn2_w_in, ffn2_w_out, final_norm):
    batch, seq, d = x.shape
    depth = ffn1_norm.shape[0]
    hg_width = hg_lb_table.shape[1]
    heads = w_q_up.shape[2] // MLA_QK
    assert hg_out_norm.shape[1] == HG_HEAD and w_hg_branch.shape[1] == hg_width
    assert q_head_norm.shape[1] == MLA_QK and w_kv_up.shape[2] == heads * (MLA_NOPE + MLA_V)
    assert seq % max(HG_BLOCK, ATT_KEYS) == 0 and (batch * seq) % max(MERGE_TILE, FFN_TILE) == 0
    t = batch * seq
    pos = positions.reshape(t, 1)
    h = x.reshape(t, d)
    for l in range(depth):
        h1, u = _ffn(h, ffn1_norm[l], ffn1_w_in[l], ffn1_w_out[l], mix_norm[l], emit_h=True, norm_dtype=BF16)
        hq, lf, kk, hv, hg, gates, q, k, v = _proj(
            u, pos, w_in[l], w_merge[l], b_merge[l], hg_lb_table, mla_q_lora_norm[l], w_q_up[l],
            mla_kv_lora_norm[l], w_kv_up[l], q_head_norm[l], k_head_norm[l], layer=l, hg_width=hg_width,
            heads=heads, seq=seq)
        y_hg = _hgrn(hq, lf, kk, hv, hg, hg_out_norm[l], batch=batch, seq=seq)
        y_mla = _attention(q, k, v, batch=batch, seq=seq, heads=heads)
        h2 = _merge(y_hg, y_mla, gates, h1, w_hg_branch[l], w_mla_branch[l], w_out[l])
        (h,) = _ffn(h2, ffn2_norm[l], ffn2_w_in[l], ffn2_w_out[l], final_norm[l], emit_h=False, norm_dtype=F32)
    return h.reshape(batch, seq, d)
```

```python
import functools

import numpy as np
import jax
import jax.numpy as jnp
from jax import lax
from jax.experimental import pallas as pl
from jax.experimental.pallas import tpu as pltpu

F32 = jnp.float32
BF16 = jnp.bfloat16

EPS = 1e-6
CHUNK = 64
HG_HEAD = 128
MLA_NOPE = 128
MLA_ROPE = 64
MLA_V = 128
MLA_QK = MLA_NOPE + MLA_ROPE
MLA_HEAD_PAD = 256
ROPE_THETA = 10000.0
LOG2E = float(np.log2(np.e))

LANES = 128
MERGE_TILE = 1024
FFN_TILE = 512
PROJ_TILE = 256
FF_CHUNK = 512
HG_BLOCK = 1024
HG_HEADS_PER_STEP = 4
ATT_TILE = 512
ATT_KEYS = 2 * ATT_TILE
ATT_SPLIT = 2
ATT_BLOCK_TILES = 24
ATT_FOLD = 32
ATT_ONES = 16
VMEM_LIMIT = 56 * 1024 * 1024


def _rms(x, gain):
    return x * lax.rsqrt(jnp.mean(x * x, axis=-1, keepdims=True) + EPS) * gain


def _silu(x):
    return x * jax.nn.sigmoid(x)


def _dot(a, b):
    return jnp.dot(a, b, preferred_element_type=F32)


def _dot_nt(a, b):
    return lax.dot_general(a, b, (((1,), (1,)), ((), ())), preferred_element_type=F32)


def _dot_tn(a, b):
    return lax.dot_general(a, b, (((0,), (0,)), ((), ())), preferred_element_type=F32)


def _resident(shape):
    zeros = (0,) * len(shape)
    return pl.BlockSpec(shape, lambda *_: zeros, pipeline_mode=pl.Buffered(1))


def _params(semantics):
    return pltpu.CompilerParams(dimension_semantics=semantics, vmem_limit_bytes=VMEM_LIMIT)


def _ffn_body(x_ref, xnext_ref, gin_ref, win_ref, wout_ref, gout_ref, *rest, emit_h):
    if emit_h:
        h_ref, n_ref, a_ref, xn_ref = rest
    else:
        n_ref, a_ref, xn_ref = rest
    d_ff = wout_ref.shape[0]
    x = x_ref[...]

    @pl.when(pl.program_id(0) == 0)
    def _():
        xn_ref[...] = _rms(x_ref[...], gin_ref[...]).astype(BF16)

    for lo in range(0, d_ff, FF_CHUNK):
        hi = min(lo + FF_CHUNK, d_ff)
        gate = _dot(xn_ref[...], win_ref[:, lo:hi])
        up = _dot(xn_ref[...], win_ref[:, d_ff + lo:d_ff + hi])
        a_ref[:, lo:hi] = (_silu(gate) * up).astype(BF16)
    xn_ref[...] = _rms(xnext_ref[...], gin_ref[...]).astype(BF16)
    h = x + 0.5 * _dot(a_ref[...], wout_ref[...])
    if emit_h:
        h_ref[...] = h
    n_ref[...] = _rms(h, gout_ref[...]).astype(n_ref.dtype)


def _ffn(x, gin, w_in, w_out, gout, *, emit_h, norm_dtype):
    t, d = x.shape
    d_ff = w_out.shape[0]
    win = w_in.astype(BF16)
    tm = FFN_TILE
    row = pl.BlockSpec((tm, d), lambda i: (i, 0))
    last = t // tm - 1
    next_row = pl.BlockSpec((tm, d), lambda i: (jnp.minimum(i + 1, last), 0))
    out_shape = [jax.ShapeDtypeStruct((t, d), norm_dtype)]
    out_specs = [row]
    if emit_h:
        out_shape = [jax.ShapeDtypeStruct((t, d), F32)] + out_shape
        out_specs = [row] + out_specs
    return pl.pallas_call(
        functools.partial(_ffn_body, emit_h=emit_h),
        grid=(t // tm,),
        in_specs=[row, next_row, _resident((1, d)), _resident(win.shape), _resident((d_ff, d)), _resident((1, d))],
        out_specs=out_specs,
        out_shape=out_shape,
        scratch_shapes=[pltpu.VMEM((tm, d_ff), BF16), pltpu.VMEM((tm, d), BF16)],
        compiler_params=_params(("arbitrary",)),
        name="ffn1" if emit_h else "ffn2",
    )(x, x, gin.reshape(1, d), win, w_out.astype(BF16), gout.reshape(1, d))


def _proj_body(u_ref, pos_ref, w_ref, wm_ref, bm_ref, tab_ref, gq_ref, wq_ref, gkv_ref, wkv_ref, qr_gain_ref,
               kn_gain_ref, kr_gain_ref, invf_ref,
               hq_ref, lf_ref, kk_ref, hv_ref, hg_ref, gate_ref, q_ref, k_ref, v_ref, *, layer, heads):
    w = hq_ref.shape[1]
    tm, d = u_ref.shape
    q_lora, kv_lora = wq_ref.shape[0], wkv_ref.shape[0]
    u = u_ref[...]

    c = _dot(u, w_ref[:, 4 * w:])
    hq_ref[...] = _silu(_dot(u, w_ref[:, 0:w])).astype(hq_ref.dtype)
    qraw = _dot(_rms(c[:, :q_lora], gq_ref[...]).astype(BF16), wq_ref[...])
    kvraw = _dot(_rms(c[:, q_lora:q_lora + kv_lora], gkv_ref[...]).astype(BF16), wkv_ref[...])
    kpe = c[:, q_lora + kv_lora:]

    tab = tab_ref[...]
    e = jnp.exp(tab - jnp.max(tab, axis=0, keepdims=True))
    lb = jnp.sum(e[:layer + 1], axis=0, keepdims=True) / jnp.sum(e, axis=0, keepdims=True)
    z = _dot(u, w_ref[:, w:2 * w])
    ez = jnp.exp(-jnp.abs(z))
    r = 1.0 / (1.0 + ez)
    a = ez * r
    pos = z >= 0
    lf_ref[...] = jnp.log(lb + (1.0 - lb) * jnp.where(pos, r, a)) * LOG2E
    kk_ref[...] = ((1.0 - lb) * jnp.where(pos, a, r)).astype(BF16)
    hv_ref[...] = _dot(u, w_ref[:, 2 * w:3 * w]).astype(BF16)
    hg_ref[...] = _silu(_dot(u, w_ref[:, 3 * w:4 * w])).astype(BF16)
    for j in range(2):
        gm = _dot(u, wm_ref[:, j * d:(j + 1) * d]) + bm_ref[:, j * d:(j + 1) * d]
        gate_ref[:, j * d:(j + 1) * d] = jax.nn.sigmoid(gm).astype(BF16)

    half = MLA_ROPE // 2
    rows4 = tm // 4
    lane4 = lax.broadcasted_iota(jnp.int32, (rows4, LANES), 1)
    posf = pos_ref[...].astype(F32)
    pos4 = [posf[g * rows4:(g + 1) * rows4] for g in range(4)]
    packed = jnp.where(lane4 < half, pos4[0],
                       jnp.where(lane4 < 2 * half, pos4[1],
                                 jnp.where(lane4 < 3 * half, pos4[2], pos4[3]))) * invf_ref[...]
    cos4, sin4 = jnp.cos(packed), jnp.sin(packed)
    spread = []
    for g in range(4):
        cg = pltpu.roll(cos4, LANES - half * g, 1) if g else cos4
        sg = pltpu.roll(sin4, LANES - half * g, 1) if g else sin4
        cc = jnp.where(lane4 < half, cg, pltpu.roll(cg, half, 1))
        ss = jnp.where(lane4 < 3 * half, pltpu.roll(sg, 2 * half, 1), pltpu.roll(sg, 3 * half, 1))
        spread.append(jnp.where(lane4 < 2 * half, cc, ss))
    cs = jnp.concatenate(spread, axis=0)
    first = lax.broadcasted_iota(jnp.int32, (tm, LANES), 1) < MLA_ROPE

    def rope(block, gain_cs):
        t = block * gain_cs
        return t + pltpu.roll(t, MLA_ROPE, 1)

    def sumsq(x):
        return jnp.sum(x * x, axis=-1, keepdims=True)

    k_rope = rope(kpe, kr_gain_ref[...] * cs)
    k_pe_ss = 0.5 * sumsq(kpe)
    q_gain_cs = qr_gain_ref[...] * cs
    scale = MLA_QK ** -0.5 * LOG2E
    nope_w = heads * MLA_NOPE
    for h in range(heads):
        lo, hi = h * MLA_NOPE, (h + 1) * MLA_NOPE
        qn = qraw[:, lo:hi]
        qr = qraw[:, nope_w + lo:nope_w + hi]
        sq = lax.rsqrt((sumsq(qn) + 0.5 * sumsq(qr)) / MLA_QK + EPS) * scale
        o = h * MLA_HEAD_PAD
        q_head = jnp.concatenate([qn * sq, rope(qr, q_gain_cs) * sq], axis=1)
        q_ref[0, h, 0] = q_head.T.astype(BF16)
        kn = kvraw[:, lo:hi]
        sk = lax.rsqrt((sumsq(kn) + k_pe_ss) / MLA_QK + EPS)
        k_ref[0, h, :, :MLA_NOPE] = (kn * sk * kn_gain_ref[...]).astype(BF16)
        k_ref[0, h, :, MLA_NOPE:] = jnp.where(first, k_rope * sk, 0.0).astype(BF16)
        v_ref[0, h] = kvraw[:, nope_w + h * MLA_V:nope_w + (h + 1) * MLA_V].astype(BF16)


def _rot_cols(w):
    half = w.shape[-1] // 2
    return jnp.concatenate([-w[..., half:], w[..., :half]], axis=-1)


def _rot_gain(g):
    half = g.shape[-1] // 2
    return jnp.concatenate([g[..., half:], g[..., :half]], axis=-1)


def _proj(u, pos, w_in, w_merge, b_merge, lb_table, gq, w_q_up, gkv, w_kv_up, q_head_gain, k_head_gain,
          *, layer, hg_width, heads, seq):
    t, d = u.shape
    q_lora, kv_lora = w_q_up.shape[0], w_kv_up.shape[0]
    w_in_ext = jnp.concatenate([w_in, _rot_cols(w_in[:, -MLA_ROPE:])], axis=1).astype(BF16)
    cols = w_in_ext.shape[1]
    assert cols == 4 * hg_width + q_lora + kv_lora + 2 * MLA_ROPE
    wq = w_q_up.reshape(q_lora, heads, MLA_QK)
    wq_rope = wq[:, :, MLA_NOPE:]
    wq_ext = jnp.concatenate(
        [wq[:, :, :MLA_NOPE].reshape(q_lora, heads * MLA_NOPE),
         jnp.concatenate([wq_rope, _rot_cols(wq_rope)], axis=-1).reshape(q_lora, heads * 2 * MLA_ROPE)],
        axis=1).astype(BF16)
    wkv = w_kv_up.reshape(kv_lora, heads, MLA_NOPE + MLA_V)
    wkv_ext = jnp.concatenate([wkv[:, :, :MLA_NOPE].reshape(kv_lora, heads * MLA_NOPE),
                               wkv[:, :, MLA_NOPE:].reshape(kv_lora, heads * MLA_V)], axis=1).astype(BF16)

    def rope_gain(g):
        return jnp.concatenate([g[MLA_NOPE:], _rot_gain(g[MLA_NOPE:])]).reshape(1, 2 * MLA_ROPE)

    inv_freq = ROPE_THETA ** (-jnp.arange(0, MLA_ROPE, 2, dtype=F32) / MLA_ROPE)
    invf = jnp.tile(inv_freq, 4).reshape(1, LANES)
    nope_gain = (q_head_gain[:MLA_NOPE] * k_head_gain[:MLA_NOPE]).reshape(1, MLA_NOPE)
    tm = PROJ_TILE

    def row(n):
        return pl.BlockSpec((tm, n), lambda i: (i, 0))

    def head_major(n):
        return pl.BlockSpec((1, heads, tm, n), lambda i: (i // per_seq, 0, i % per_seq, 0))

    per_seq = seq // tm
    per_att = ATT_TILE // tm
    qt_spec = pl.BlockSpec((1, heads, 1, MLA_HEAD_PAD, tm),
                           lambda i: (i // per_seq, 0, (i % per_seq) // per_att, 0, i % per_att))
    return pl.pallas_call(
        functools.partial(_proj_body, layer=layer, heads=heads),
        grid=(t // tm,),
        in_specs=[row(d), row(1), _resident((d, cols)), _resident((d, 2 * d)), _resident((1, 2 * d)),
                  _resident(lb_table.shape), _resident((1, q_lora)), _resident(wq_ext.shape),
                  _resident((1, kv_lora)), _resident(wkv_ext.shape), _resident((1, LANES)),
                  _resident((1, MLA_NOPE)), _resident((1, LANES)), _resident((1, LANES))],
        out_specs=[row(hg_width), row(hg_width), row(hg_width), row(hg_width), row(hg_width), row(2 * d),
                   qt_spec, head_major(MLA_HEAD_PAD), head_major(MLA_V)],
        out_shape=[jax.ShapeDtypeStruct((t, hg_width), BF16),
                   jax.ShapeDtypeStruct((t, hg_width), F32),
                   jax.ShapeDtypeStruct((t, hg_width), BF16),
                   jax.ShapeDtypeStruct((t, hg_width), BF16),
                   jax.ShapeDtypeStruct((t, hg_width), BF16),
                   jax.ShapeDtypeStruct((t, 2 * d), BF16),
                   jax.ShapeDtypeStruct((t // seq, heads, seq // ATT_TILE, MLA_HEAD_PAD, ATT_TILE), BF16),
                   jax.ShapeDtypeStruct((t // seq, heads, seq, MLA_HEAD_PAD), BF16),
                   jax.ShapeDtypeStruct((t // seq, heads, seq, MLA_V), BF16)],
        compiler_params=_params(("parallel",)),
        name="proj",
    )(u, pos, w_in_ext, w_merge.astype(BF16), b_merge.reshape(1, 2 * d), lb_table, gq.reshape(1, q_lora), wq_ext,
      gkv.reshape(1, kv_lora), wkv_ext, rope_gain(q_head_gain), nope_gain, rope_gain(k_head_gain), invf)


def _split3(x):
    hi = x.astype(BF16)
    r = x - hi.astype(F32)
    mid = r.astype(BF16)
    lo = (r - mid.astype(F32)).astype(BF16)
    return hi, mid, lo


def _hgrn_body(q_ref, lf_ref, kk_ref, v_ref, g_ref, gain_ref, tri_ref, y_ref, state_ref):
    @pl.when(pl.program_id(2) == 0)
    def _():
        state_ref[...] = jnp.zeros_like(state_ref)

    n = CHUNK
    tb = q_ref.shape[1]
    chunks = [slice(c * n, (c + 1) * n) for c in range(tb // n)]
    trow = lax.broadcasted_iota(jnp.int32, (n, n), 0)
    tcol = lax.broadcasted_iota(jnp.int32, (n, n), 1)
    tri = tri_ref[...]
    halves = [1 << i for i in range(n.bit_length() - 1)]
    pairs = {half: (trow // (2 * half) == tcol // (2 * half)) & (trow % (2 * half) >= half)
             & (tcol % (2 * half) < half) for half in halves}

    def ref_rows(x, size, row):
        blocks = x.reshape(tb // size, size, HG_HEAD)[:, row:row + 1, :]
        return jnp.broadcast_to(blocks, (tb // size, size, HG_HEAD)).reshape(tb, HG_HEAD)

    r8 = lax.broadcasted_iota(jnp.int32, (1, 8, HG_HEAD), 1)

    def head(q, lf, k, v, g, s_ref):
        lf3 = lf.reshape(tb // 8, 8, HG_HEAD)
        split = jnp.concatenate(_split3(lf), axis=1)
        cums = []
        for sl in chunks:
            r = _dot(tri, split[sl])
            cums.append(r[:, :HG_HEAD] + r[:, HG_HEAD:2 * HG_HEAD] + r[:, 2 * HG_HEAD:])
        cum = jnp.concatenate(cums, axis=0)
        diag = jnp.sum(q * k, axis=-1, keepdims=True)
        scores = [jnp.where(trow == tcol, diag[sl], 0.0) for sl in chunks]
        for half in halves:
            size = 2 * half
            if half >= 4:
                diff = pltpu.bitcast(cum - ref_rows(cum, size, half - 1), jnp.uint32)
                dec = pltpu.bitcast(diff | jnp.uint32(0x80000000), F32)
            elif half == 2:
                dec = jnp.where(r8 % 4 == 3, lf3 + pltpu.roll(lf3, 1, 1),
                                jnp.where(r8 % 4 == 2, lf3,
                                          jnp.where(r8 % 4 == 0, pltpu.roll(lf3, 7, 1), 0.0))).reshape(tb, HG_HEAD)
            else:
                dec = jnp.where(r8 % 2 == 1, lf3, 0.0).reshape(tb, HG_HEAD)
            e = jnp.exp2(dec)
            qe = (q * e).astype(BF16)
            ke = (k * e).astype(BF16)
            for c, sl in enumerate(chunks):
                scores[c] = scores[c] + jnp.where(pairs[half], _dot_nt(qe[sl], ke[sl]), 0.0)
        last = ref_rows(cum, n, n - 1)
        qdec = (q * jnp.exp2(cum)).astype(BF16)
        kdec = (k * jnp.exp2(last - cum)).astype(BF16)
        updates = [_dot_tn(kdec[sl], v[sl]) for sl in chunks]
        lasts = jnp.concatenate([cum[sl][n - 1:n, :] for sl in chunks], axis=0)
        decay_cols = jnp.concatenate([jnp.exp2(lasts)] * (HG_HEAD // len(chunks)), axis=0).T
        state = s_ref[...]
        outs = []
        for c, sl in enumerate(chunks):
            lhs = jnp.concatenate([qdec[sl], scores[c].astype(BF16)], axis=1)
            rhs = jnp.concatenate([state.astype(BF16), v[sl]], axis=0)
            outs.append(_dot(lhs, rhs))
            state = state * decay_cols[:, c:c + 1] + updates[c]
        s_ref[...] = state
        return _rms(jnp.concatenate(outs, axis=0), gain_ref[...]) * g

    for hd in range(q_ref.shape[2] // HG_HEAD):
        lanes = slice(hd * HG_HEAD, (hd + 1) * HG_HEAD)
        y_ref[0, :, lanes] = head(q_ref[0, :, lanes].astype(F32), lf_ref[0, :, lanes],
                                  kk_ref[0, :, lanes].astype(F32), v_ref[0, :, lanes],
                                  g_ref[0, :, lanes].astype(F32), state_ref.at[hd]).astype(y_ref.dtype)


def _hgrn(hq, lf, kk, v, g, out_gain, *, batch, seq):
    width = hq.shape[1]
    heads = width // HG_HEAD
    tb = HG_BLOCK
    tri = jnp.asarray(np.tril(np.ones((CHUNK, CHUNK), np.float32)), BF16)

    def r3(a):
        return a.reshape(batch, seq, width)

    blk = pl.BlockSpec((1, tb, HG_HEADS_PER_STEP * HG_HEAD), lambda b, h, j: (b, j, h))
    y = pl.pallas_call(
        _hgrn_body,
        grid=(batch, heads // HG_HEADS_PER_STEP, seq // tb),
        in_specs=[blk, blk, blk, blk, blk, _resident((1, HG_HEAD)), _resident((CHUNK, CHUNK))],
        out_specs=blk,
        out_shape=jax.ShapeDtypeStruct((batch, seq, width), BF16),
        scratch_shapes=[pltpu.VMEM((HG_HEADS_PER_STEP, HG_HEAD, HG_HEAD), F32)],
        compiler_params=_params(("parallel", "parallel", "arbitrary")),
        name="hgrn",
    )(r3(hq), r3(lf), r3(kk), r3(v), r3(g), out_gain.reshape(1, HG_HEAD), tri)
    return y.reshape(batch * seq, width)


def _attn_body(q_ref, k_ref, v_ref, o_ref, vt_ref, s_ref, mx_ref, m_ref, acc_ref):
    seq = k_ref.shape[2]
    tq = ATT_TILE
    tk = ATT_KEYS
    tqs = tq // ATT_SPLIT
    n_q = seq // tq
    n_k = seq // tk

    for j in range(n_k):
        vt_ref[j, :MLA_V] = v_ref[0, 0, j * tk:(j + 1) * tk, :].astype(F32).T.astype(BF16)
        vt_ref[j, MLA_V:] = jnp.ones((ATT_ONES, tk), BF16)

    key_chunk = lax.broadcasted_iota(jnp.int32, (tk, tqs), 0) // CHUNK
    qry_chunk = lax.broadcasted_iota(jnp.int32, (tk, tqs), 1) // CHUNK
    first_query = {"half": 0, "whole": tq}
    allowed = {kind: [key_chunk <= qry_chunk + (off + sub * tqs) // CHUNK for sub in range(ATT_SPLIT)]
               for kind, off in first_query.items()}

    def fold_rows(x, op):
        rows, cols = x.shape
        x = x.reshape(rows // ATT_FOLD, ATT_FOLD, cols)
        out = x[0]
        for i in range(1, rows // ATT_FOLD):
            out = op(out, x[i])
        return out

    def aligned(x, m):
        return x if isinstance(x, int) else pl.multiple_of(x, m)

    def scores(qt, kj, buf):
        k_tile = k_ref[0, 0, pl.ds(aligned(kj * tk, tk), tk), :]
        s = _dot(k_tile, q_ref[0, 0, qt])
        s_ref[buf] = s
        mx_ref[buf] = jnp.max(fold_rows(s, jnp.maximum), axis=0, keepdims=True)

    def start():
        m_ref[...] = jnp.full(m_ref.shape, -jnp.inf, F32)
        acc_ref[...] = jnp.zeros(acc_ref.shape, F32)

    def finish(base):
        for sub in range(ATT_SPLIT):
            acc = acc_ref[sub]
            o_ref[0, pl.ds(aligned(base + sub * tqs, tqs), tqs), :] = (
                (acc[:MLA_V] / acc[MLA_V:MLA_V + 1]).T.astype(o_ref.dtype))

    def absorb(kj, buf, kind):
        for sub in range(ATT_SPLIT):
            cols = slice(sub * tqs, (sub + 1) * tqs)
            keys = tk if kind == "full" else first_query[kind] + (sub + 1) * tqs
            vt_tile = vt_ref[kj, :, :keys]
            ss = s_ref[buf, :keys, cols]
            if kind == "full":
                tile_max = mx_ref[buf, :, cols]
            else:
                ss = jnp.where(allowed[kind][sub][:keys], ss, -jnp.inf)
                tile_max = jnp.max(fold_rows(ss, jnp.maximum), axis=0, keepdims=True)
            m = m_ref[sub]
            m_new = jnp.maximum(m, tile_max)
            alpha = jnp.exp2(m - m_new)
            p = jnp.exp2(ss - m_new)
            m_ref[sub] = m_new
            acc_ref[sub] = alpha * acc_ref[sub] + _dot(vt_tile, p.astype(BF16))

    def query_tile(qi):
        count = qi // 2 + 1
        start()
        for j in range(count):
            if j + 1 < count:
                scores(qi, j + 1, (j + 1) % 2)
            elif qi + 1 < n_q:
                scores(qi + 1, 0, home(qi + 1))
            absorb(j, home(qi) if j == 0 else j % 2, "full" if j + 1 < count else ("whole" if qi % 2 else "half"))
        finish(qi * tq)

    def home(qi):
        return {0: 1, 1: 0}.get(qi, 2)

    blocks, size = [[]], 0
    for qi in range(n_q):
        if blocks[-1] and size + qi // 2 + 1 > ATT_BLOCK_TILES:
            blocks.append([])
            size = 0
        blocks[-1].append(qi)
        size += qi // 2 + 1

    def block_step(i, _):
        for j, tiles in enumerate(blocks):
            @pl.when(i == j)
            def _(tiles=tiles):
                for qi in tiles:
                    query_tile(qi)
        return 0

    scores(0, 0, home(0))
    lax.fori_loop(0, len(blocks), block_step, 0)


def _attention(q, k, v, *, batch, seq, heads):
    qt = pl.BlockSpec((1, 1, seq // ATT_TILE, MLA_HEAD_PAD, ATT_TILE), lambda b, h: (b, h, 0, 0, 0))
    kk = pl.BlockSpec((1, 1, seq, MLA_HEAD_PAD), lambda b, h: (b, h, 0, 0))
    vv = pl.BlockSpec((1, 1, seq, MLA_V), lambda b, h: (b, h, 0, 0))
    vo = pl.BlockSpec((1, seq, MLA_V), lambda b, h: (b, 0, h))
    o = pl.pallas_call(
        _attn_body,
        grid=(batch, heads),
        in_specs=[qt, kk, vv],
        out_specs=vo,
        out_shape=jax.ShapeDtypeStruct((batch, seq, heads * MLA_V), BF16),
        scratch_shapes=[pltpu.VMEM((seq // ATT_KEYS, MLA_V + ATT_ONES, ATT_KEYS), BF16),
                        pltpu.VMEM((3, ATT_KEYS, ATT_TILE), F32),
                        pltpu.VMEM((3, 1, ATT_TILE), F32),
                        pltpu.VMEM((ATT_SPLIT, 1, ATT_TILE // ATT_SPLIT), F32),
                        pltpu.VMEM((ATT_SPLIT, MLA_V + ATT_ONES, ATT_TILE // ATT_SPLIT), F32)],
        compiler_params=_params(("parallel", "parallel")),
        name="attn",
    )(q, k, v)
    return o.reshape(batch * seq, heads * MLA_V)


def _merge_body(yh_ref, ym_ref, gate_ref, h_ref, wh_ref, wm_ref, wo_ref, o_ref):
    d = h_ref.shape[1]
    mix = (gate_ref[:, :d].astype(F32) * _dot(yh_ref[...], wh_ref[...])
           + gate_ref[:, d:].astype(F32) * _dot(ym_ref[...], wm_ref[...]))
    o_ref[...] = h_ref[...] + _dot(mix.astype(BF16), wo_ref[...])


def _merge(yh, ym, gates, h, w_h, w_m, w_o):
    t, d = h.shape
    tm = MERGE_TILE

    def row(n):
        return pl.BlockSpec((tm, n), lambda i: (i, 0))

    return pl.pallas_call(
        _merge_body,
        grid=(t // tm,),
        in_specs=[row(yh.shape[1]), row(ym.shape[1]), row(2 * d), row(d), _resident(w_h.shape),
                  _resident(w_m.shape), _resident(w_o.shape)],
        out_specs=row(d),
        out_shape=jax.ShapeDtypeStruct((t, d), F32),
        compiler_params=_params(("parallel",)),
        name="merge",
    )(yh, ym, gates, h, w_h.astype(BF16), w_m.astype(BF16), w_o.astype(BF16))


def kernel(x, positions, ffn1_norm, ffn1_w_in, ffn1_w_out, mix_norm, w_in, hg_lb_table, hg_out_norm, w_hg_branch, mla_q_lora_norm, w_q_up, mla_kv_lora_norm, w_kv_up, q_head_norm, k_head_norm, w_mla_branch, w_merge, b_merge, w_out, ffn2_norm, ffn2_w_in, ffn2_w_out, final_norm):
    batch, seq, d = x.shape
    depth = ffn1_norm.shape[0]
    hg_width = hg_lb_table.shape[1]
    heads = w_q_up.shape[2] // MLA_QK
    assert hg_out_norm.shape[1] == HG_HEAD and w_hg_branch.shape[1] == hg_width
    assert q_head_norm.shape[1] == MLA_QK and w_kv_up.shape[2] == heads * (MLA_NOPE + MLA_V)
    assert seq % max(HG_BLOCK, ATT_KEYS) == 0 and (batch * seq) % max(MERGE_TILE, FFN_TILE) == 0
    t = batch * seq
    pos = positions.reshape(t, 1)
    h = x.reshape(t, d)
    for l in range(depth):
        h1, u = _ffn(h, ffn1_norm[l], ffn1_w_in[l], ffn1_w_out[l], mix_norm[l], emit_h=True, norm_dtype=BF16)
        hq, lf, kk, hv, hg, gates, q, k, v = _proj(
            u, pos, w_in[l], w_merge[l], b_merge[l], hg_lb_table, mla_q_lora_norm[l], w_q_up[l],
            mla_kv_lora_norm[l], w_kv_up[l], q_head_norm[l], k_head_norm[l], layer=l, hg_width=hg_width,
            heads=heads, seq=seq)
        y_hg = _hgrn(hq, lf, kk, hv, hg, hg_out_norm[l], batch=batch, seq=seq)
        y_mla = _attention(q, k, v, batch=batch, seq=seq, heads=heads)
        h2 = _merge(y_hg, y_mla, gates, h1, w_hg_branch[l], w_mla_branch[l], w_out[l])
        (h,) = _ffn(h2, ffn2_norm[l], ffn2_w_in[l], ffn2_w_out[l], final_norm[l], emit_h=False, norm_dtype=F32)
    return h.reshape(batch, seq, d)
```

```python
import functools

import numpy as np
import jax
import jax.numpy as jnp
from jax import lax
from jax.experimental import pallas as pl
from jax.experimental.pallas import tpu as pltpu

F32 = jnp.float32
BF16 = jnp.bfloat16

EPS = 1e-6
CHUNK = 64
HG_HEAD = 128
MLA_NOPE = 128
MLA_ROPE = 64
MLA_V = 128
MLA_QK = MLA_NOPE + MLA_ROPE
MLA_HEAD_PAD = 256
ROPE_THETA = 10000.0
LOG2E = float(np.log2(np.e))

LANES = 128
MERGE_TILE = 1024
FFN_TILE = 1024
PROJ_TILE = 256
FF_CHUNK = 512
HG_BLOCK = 1024
HG_HEADS_PER_STEP = 4
ATT_TILE = 512
ATT_KEYS = 2 * ATT_TILE
ATT_SPLIT = 1
ATT_BLOCK_TILES = 24
ATT_FOLD = 32
ATT_ONES = 16
VMEM_LIMIT = 56 * 1024 * 1024


def _rms(x, gain):
    return x * lax.rsqrt(jnp.mean(x * x, axis=-1, keepdims=True) + EPS) * gain


def _silu(x):
    return x * jax.nn.sigmoid(x)


def _dot(a, b):
    return jnp.dot(a, b, preferred_element_type=F32)


def _dot_nt(a, b):
    return lax.dot_general(a, b, (((1,), (1,)), ((), ())), preferred_element_type=F32)


def _dot_tn(a, b):
    return lax.dot_general(a, b, (((0,), (0,)), ((), ())), preferred_element_type=F32)


def _resident(shape):
    zeros = (0,) * len(shape)
    return pl.BlockSpec(shape, lambda *_: zeros, pipeline_mode=pl.Buffered(1))


def _params(semantics):
    return pltpu.CompilerParams(dimension_semantics=semantics, vmem_limit_bytes=VMEM_LIMIT)


def _ffn_body(x_ref, gin_ref, win_ref, wout_ref, gout_ref, *rest, emit_h):
    if emit_h:
        h_ref, n_ref, a_ref = rest
    else:
        n_ref, a_ref = rest
    d_ff = wout_ref.shape[0]
    x = x_ref[...]
    xn = _rms(x, gin_ref[...]).astype(BF16)
    for lo in range(0, d_ff, FF_CHUNK):
        hi = min(lo + FF_CHUNK, d_ff)
        gate = _dot(xn, win_ref[:, lo:hi])
        up = _dot(xn, win_ref[:, d_ff + lo:d_ff + hi])
        a_ref[:, lo:hi] = (_silu(gate) * up).astype(BF16)
    h = x + 0.5 * _dot(a_ref[...], wout_ref[...])
    if emit_h:
        h_ref[...] = h
    n_ref[...] = _rms(h, gout_ref[...]).astype(n_ref.dtype)


def _ffn(x, gin, w_in, w_out, gout, *, emit_h, norm_dtype):
    t, d = x.shape
    d_ff = w_out.shape[0]
    win = w_in.astype(BF16)
    tm = FFN_TILE
    row = pl.BlockSpec((tm, d), lambda i: (i, 0))
    out_shape = [jax.ShapeDtypeStruct((t, d), norm_dtype)]
    out_specs = [row]
    if emit_h:
        out_shape = [jax.ShapeDtypeStruct((t, d), F32)] + out_shape
        out_specs = [row] + out_specs
    return pl.pallas_call(
        functools.partial(_ffn_body, emit_h=emit_h),
        grid=(t // tm,),
        in_specs=[row, _resident((1, d)), _resident(win.shape), _resident((d_ff, d)), _resident((1, d))],
        out_specs=out_specs,
        out_shape=out_shape,
        scratch_shapes=[pltpu.VMEM((tm, d_ff), BF16)],
        compiler_params=_params(("parallel",)),
        name="ffn1" if emit_h else "ffn2",
    )(x, gin.reshape(1, d), win, w_out.astype(BF16), gout.reshape(1, d))


def _proj_body(u_ref, pos_ref, w_ref, wm_ref, bm_ref, tab_ref, gq_ref, wq_ref, gkv_ref, wkv_ref, qr_gain_ref,
               kn_gain_ref, kr_gain_ref, invf_ref,
               hq_ref, lf_ref, kk_ref, hv_ref, hg_ref, gate_ref, q_ref, k_ref, v_ref, *, layer, heads):
    w = hq_ref.shape[1]
    tm, d = u_ref.shape
    q_lora, kv_lora = wq_ref.shape[0], wkv_ref.shape[0]
    u = u_ref[...]

    c = _dot(u, w_ref[:, 4 * w:])
    hq_ref[...] = _silu(_dot(u, w_ref[:, 0:w])).astype(hq_ref.dtype)
    qraw = _dot(_rms(c[:, :q_lora], gq_ref[...]).astype(BF16), wq_ref[...])
    kvraw = _dot(_rms(c[:, q_lora:q_lora + kv_lora], gkv_ref[...]).astype(BF16), wkv_ref[...])
    kpe = c[:, q_lora + kv_lora:]

    tab = tab_ref[...]
    e = jnp.exp(tab - jnp.max(tab, axis=0, keepdims=True))
    lb = jnp.sum(e[:layer + 1], axis=0, keepdims=True) / jnp.sum(e, axis=0, keepdims=True)
    z = _dot(u, w_ref[:, w:2 * w])
    ez = jnp.exp(-jnp.abs(z))
    r = 1.0 / (1.0 + ez)
    a = ez * r
    pos = z >= 0
    lf_ref[...] = jnp.log(lb + (1.0 - lb) * jnp.where(pos, r, a)) * LOG2E
    kk_ref[...] = ((1.0 - lb) * jnp.where(pos, a, r)).astype(BF16)
    hv_ref[...] = _dot(u, w_ref[:, 2 * w:3 * w]).astype(BF16)
    hg_ref[...] = _silu(_dot(u, w_ref[:, 3 * w:4 * w])).astype(BF16)
    for j in range(2):
        gm = _dot(u, wm_ref[:, j * d:(j + 1) * d]) + bm_ref[:, j * d:(j + 1) * d]
        gate_ref[:, j * d:(j + 1) * d] = jax.nn.sigmoid(gm).astype(BF16)

    half = MLA_ROPE // 2
    rows4 = tm // 4
    lane4 = lax.broadcasted_iota(jnp.int32, (rows4, LANES), 1)
    posf = pos_ref[...].astype(F32)
    pos4 = [posf[g * rows4:(g + 1) * rows4] for g in range(4)]
    packed = jnp.where(lane4 < half, pos4[0],
                       jnp.where(lane4 < 2 * half, pos4[1],
                                 jnp.where(lane4 < 3 * half, pos4[2], pos4[3]))) * invf_ref[...]
    cos4, sin4 = jnp.cos(packed), jnp.sin(packed)
    spread = []
    for g in range(4):
        cg = pltpu.roll(cos4, LANES - half * g, 1) if g else cos4
        sg = pltpu.roll(sin4, LANES - half * g, 1) if g else sin4
        cc = jnp.where(lane4 < half, cg, pltpu.roll(cg, half, 1))
        ss = jnp.where(lane4 < 3 * half, pltpu.roll(sg, 2 * half, 1), pltpu.roll(sg, 3 * half, 1))
        spread.append(jnp.where(lane4 < 2 * half, cc, ss))
    cs = jnp.concatenate(spread, axis=0)
    first = lax.broadcasted_iota(jnp.int32, (tm, LANES), 1) < MLA_ROPE

    def rope(block, gain_cs):
        t = block * gain_cs
        return t + pltpu.roll(t, MLA_ROPE, 1)

    def sumsq(x):
        return jnp.sum(x * x, axis=-1, keepdims=True)

    k_rope = rope(kpe, kr_gain_ref[...] * cs)
    k_pe_ss = 0.5 * sumsq(kpe)
    q_gain_cs = qr_gain_ref[...] * cs
    scale = MLA_QK ** -0.5 * LOG2E
    nope_w = heads * MLA_NOPE
    for h in range(heads):
        lo, hi = h * MLA_NOPE, (h + 1) * MLA_NOPE
        qn = qraw[:, lo:hi]
        qr = qraw[:, nope_w + lo:nope_w + hi]
        sq = lax.rsqrt((sumsq(qn) + 0.5 * sumsq(qr)) / MLA_QK + EPS) * scale
        o = h * MLA_HEAD_PAD
        q_head = jnp.concatenate([qn * sq, rope(qr, q_gain_cs) * sq], axis=1)
        q_ref[0, h, 0] = q_head.T.astype(BF16)
        kn = kvraw[:, lo:hi]
        sk = lax.rsqrt((sumsq(kn) + k_pe_ss) / MLA_QK + EPS)
        k_ref[0, h, :, :MLA_NOPE] = (kn * sk * kn_gain_ref[...]).astype(BF16)
        k_ref[0, h, :, MLA_NOPE:] = jnp.where(first, k_rope * sk, 0.0).astype(BF16)
        v_ref[0, h] = kvraw[:, nope_w + h * MLA_V:nope_w + (h + 1) * MLA_V].astype(BF16)


def _rot_cols(w):
    half = w.shape[-1] // 2
    return jnp.concatenate([-w[..., half:], w[..., :half]], axis=-1)


def _rot_gain(g):
    half = g.shape[-1] // 2
    return jnp.concatenate([g[..., half:], g[..., :half]], axis=-1)


def _proj(u, pos, w_in, w_merge, b_merge, lb_table, gq, w_q_up, gkv, w_kv_up, q_head_gain, k_head_gain,
          *, layer, hg_width, heads, seq):
    t, d = u.shape
    q_lora, kv_lora = w_q_up.shape[0], w_kv_up.shape[0]
    w_in_ext = jnp.concatenate([w_in, _rot_cols(w_in[:, -MLA_ROPE:])], axis=1).astype(BF16)
    cols = w_in_ext.shape[1]
    assert cols == 4 * hg_width + q_lora + kv_lora + 2 * MLA_ROPE
    wq = w_q_up.reshape(q_lora, heads, MLA_QK)
    wq_rope = wq[:, :, MLA_NOPE:]
    wq_ext = jnp.concatenate(
        [wq[:, :, :MLA_NOPE].reshape(q_lora, heads * MLA_NOPE),
         jnp.concatenate([wq_rope, _rot_cols(wq_rope)], axis=-1).reshape(q_lora, heads * 2 * MLA_ROPE)],
        axis=1).astype(BF16)
    wkv = w_kv_up.reshape(kv_lora, heads, MLA_NOPE + MLA_V)
    wkv_ext = jnp.concatenate([wkv[:, :, :MLA_NOPE].reshape(kv_lora, heads * MLA_NOPE),
                               wkv[:, :, MLA_NOPE:].reshape(kv_lora, heads * MLA_V)], axis=1).astype(BF16)

    def rope_gain(g):
        return jnp.concatenate([g[MLA_NOPE:], _rot_gain(g[MLA_NOPE:])]).reshape(1, 2 * MLA_ROPE)

    inv_freq = ROPE_THETA ** (-jnp.arange(0, MLA_ROPE, 2, dtype=F32) / MLA_ROPE)
    invf = jnp.tile(inv_freq, 4).reshape(1, LANES)
    nope_gain = (q_head_gain[:MLA_NOPE] * k_head_gain[:MLA_NOPE]).reshape(1, MLA_NOPE)
    tm = PROJ_TILE

    def row(n):
        return pl.BlockSpec((tm, n), lambda i: (i, 0))

    def head_major(n):
        return pl.BlockSpec((1, heads, tm, n), lambda i: (i // per_seq, 0, i % per_seq, 0))

    per_seq = seq // tm
    per_att = ATT_TILE // tm
    qt_spec = pl.BlockSpec((1, heads, 1, MLA_HEAD_PAD, tm),
                           lambda i: (i // per_seq, 0, (i % per_seq) // per_att, 0, i % per_att))
    return pl.pallas_call(
        functools.partial(_proj_body, layer=layer, heads=heads),
        grid=(t // tm,),
        in_specs=[row(d), row(1), _resident((d, cols)), _resident((d, 2 * d)), _resident((1, 2 * d)),
                  _resident(lb_table.shape), _resident((1, q_lora)), _resident(wq_ext.shape),
                  _resident((1, kv_lora)), _resident(wkv_ext.shape), _resident((1, LANES)),
                  _resident((1, MLA_NOPE)), _resident((1, LANES)), _resident((1, LANES))],
        out_specs=[row(hg_width), row(hg_width), row(hg_width), row(hg_width), row(hg_width), row(2 * d),
                   qt_spec, head_major(MLA_HEAD_PAD), head_major(MLA_V)],
        out_shape=[jax.ShapeDtypeStruct((t, hg_width), BF16),
                   jax.ShapeDtypeStruct((t, hg_width), F32),
                   jax.ShapeDtypeStruct((t, hg_width), BF16),
                   jax.ShapeDtypeStruct((t, hg_width), BF16),
                   jax.ShapeDtypeStruct((t, hg_width), BF16),
                   jax.ShapeDtypeStruct((t, 2 * d), BF16),
                   jax.ShapeDtypeStruct((t // seq, heads, seq // ATT_TILE, MLA_HEAD_PAD, ATT_TILE), BF16),
                   jax.ShapeDtypeStruct((t // seq, heads, seq, MLA_HEAD_PAD), BF16),
                   jax.ShapeDtypeStruct((t // seq, heads, seq, MLA_V), BF16)],
        compiler_params=_params(("parallel",)),
        name="proj",
    )(u, pos, w_in_ext, w_merge.astype(BF16), b_merge.reshape(1, 2 * d), lb_table, gq.reshape(1, q_lora), wq_ext,
      gkv.reshape(1, kv_lora), wkv_ext, rope_gain(q_head_gain), nope_gain, rope_gain(k_head_gain), invf)


def _split3(x):
    hi = x.astype(BF16)
    r = x - hi.astype(F32)
    mid = r.astype(BF16)
    lo = (r - mid.astype(F32)).astype(BF16)
    return hi, mid, lo


def _hgrn_body(q_ref, lf_ref, kk_ref, v_ref, g_ref, gain_ref, tri_ref, y_ref, state_ref):
    @pl.when(pl.program_id(2) == 0)
    def _():
        state_ref[...] = jnp.zeros_like(state_ref)

    n = CHUNK
    tb = q_ref.shape[1]
    chunks = [slice(c * n, (c + 1) * n) for c in range(tb // n)]
    trow = lax.broadcasted_iota(jnp.int32, (n, n), 0)
    tcol = lax.broadcasted_iota(jnp.int32, (n, n), 1)
    tri = tri_ref[...]
    halves = [1 << i for i in range(n.bit_length() - 1)]
    pairs = {half: (trow // (2 * half) == tcol // (2 * half)) & (trow % (2 * half) >= half)
             & (tcol % (2 * half) < half) for half in halves}

    def ref_rows(x, size, row):
        blocks = x.reshape(tb // size, size, HG_HEAD)[:, row:row + 1, :]
        return jnp.broadcast_to(blocks, (tb // size, size, HG_HEAD)).reshape(tb, HG_HEAD)

    r8 = lax.broadcasted_iota(jnp.int32, (1, 8, HG_HEAD), 1)

    def head(q, lf, k, v, g, s_ref):
        lf3 = lf.reshape(tb // 8, 8, HG_HEAD)
        split = jnp.concatenate(_split3(lf), axis=1)
        cums = []
        for sl in chunks:
            r = _dot(tri, split[sl])
            cums.append(r[:, :HG_HEAD] + r[:, HG_HEAD:2 * HG_HEAD] + r[:, 2 * HG_HEAD:])
        cum = jnp.concatenate(cums, axis=0)
        diag = jnp.sum(q * k, axis=-1, keepdims=True)
        scores = [jnp.where(trow == tcol, diag[sl], 0.0) for sl in chunks]
        for half in halves:
            size = 2 * half
            if half >= 4:
                diff = pltpu.bitcast(cum - ref_rows(cum, size, half - 1), jnp.uint32)
                dec = pltpu.bitcast(diff | jnp.uint32(0x80000000), F32)
            elif half == 2:
                dec = jnp.where(r8 % 4 == 3, lf3 + pltpu.roll(lf3, 1, 1),
                                jnp.where(r8 % 4 == 2, lf3,
                                          jnp.where(r8 % 4 == 0, pltpu.roll(lf3, 7, 1), 0.0))).reshape(tb, HG_HEAD)
            else:
                dec = jnp.where(r8 % 2 == 1, lf3, 0.0).reshape(tb, HG_HEAD)
            e = jnp.exp2(dec)
            qe = (q * e).astype(BF16)
            ke = (k * e).astype(BF16)
            for c, sl in enumerate(chunks):
                scores[c] = scores[c] + jnp.where(pairs[half], _dot_nt(qe[sl], ke[sl]), 0.0)
        last = ref_rows(cum, n, n - 1)
        qdec = (q * jnp.exp2(cum)).astype(BF16)
        kdec = (k * jnp.exp2(last - cum)).astype(BF16)
        updates = [_dot_tn(kdec[sl], v[sl]) for sl in chunks]
        lasts = jnp.concatenate([cum[sl][n - 1:n, :] for sl in chunks], axis=0)
        decay_cols = jnp.concatenate([jnp.exp2(lasts)] * (HG_HEAD // len(chunks)), axis=0).T
        state = s_ref[...]
        outs = []
        for c, sl in enumerate(chunks):
            lhs = jnp.concatenate([qdec[sl], scores[c].astype(BF16)], axis=1)
            rhs = jnp.concatenate([state.astype(BF16), v[sl]], axis=0)
            outs.append(_dot(lhs, rhs))
            state = state * decay_cols[:, c:c + 1] + updates[c]
        s_ref[...] = state
        return _rms(jnp.concatenate(outs, axis=0), gain_ref[...]) * g

    for hd in range(q_ref.shape[2] // HG_HEAD):
        lanes = slice(hd * HG_HEAD, (hd + 1) * HG_HEAD)
        y_ref[0, :, lanes] = head(q_ref[0, :, lanes].astype(F32), lf_ref[0, :, lanes],
                                  kk_ref[0, :, lanes].astype(F32), v_ref[0, :, lanes],
                                  g_ref[0, :, lanes].astype(F32), state_ref.at[hd]).astype(y_ref.dtype)


def _hgrn(hq, lf, kk, v, g, out_gain, *, batch, seq):
    width = hq.shape[1]
    heads = width // HG_HEAD
    tb = HG_BLOCK
    tri = jnp.asarray(np.tril(np.ones((CHUNK, CHUNK), np.float32)), BF16)

    def r3(a):
        return a.reshape(batch, seq, width)

    blk = pl.BlockSpec((1, tb, HG_HEADS_PER_STEP * HG_HEAD), lambda b, h, j: (b, j, h))
    y = pl.pallas_call(
        _hgrn_body,
        grid=(batch, heads // HG_HEADS_PER_STEP, seq // tb),
        in_specs=[blk, blk, blk, blk, blk, _resident((1, HG_HEAD)), _resident((CHUNK, CHUNK))],
        out_specs=blk,
        out_shape=jax.ShapeDtypeStruct((batch, seq, width), BF16),
        scratch_shapes=[pltpu.VMEM((HG_HEADS_PER_STEP, HG_HEAD, HG_HEAD), F32)],
        compiler_params=_params(("parallel", "parallel", "arbitrary")),
        name="hgrn",
    )(r3(hq), r3(lf), r3(kk), r3(v), r3(g), out_gain.reshape(1, HG_HEAD), tri)
    return y.reshape(batch * seq, width)


def _attn_body(q_ref, k_ref, v_ref, o_ref, vt_ref, s_ref, mx_ref, m_ref, acc_ref):
    seq = k_ref.shape[2]
    tq = ATT_TILE
    tk = ATT_KEYS
    tqs = tq // ATT_SPLIT
    n_q = seq // tq
    n_k = seq // tk

    for j in range(n_k):
        vt_ref[j, :MLA_V] = v_ref[0, 0, j * tk:(j + 1) * tk, :].astype(F32).T.astype(BF16)
        vt_ref[j, MLA_V:] = jnp.ones((ATT_ONES, tk), BF16)

    key_chunk = lax.broadcasted_iota(jnp.int32, (tk, tqs), 0) // CHUNK
    qry_chunk = lax.broadcasted_iota(jnp.int32, (tk, tqs), 1) // CHUNK
    first_query = {"half": 0, "whole": tq}
    allowed = {kind: [key_chunk <= qry_chunk + (off + sub * tqs) // CHUNK for sub in range(ATT_SPLIT)]
               for kind, off in first_query.items()}

    def fold_rows(x, op):
        rows, cols = x.shape
        x = x.reshape(rows // ATT_FOLD, ATT_FOLD, cols)
        out = x[0]
        for i in range(1, rows // ATT_FOLD):
            out = op(out, x[i])
        return out

    def aligned(x, m):
        return x if isinstance(x, int) else pl.multiple_of(x, m)

    def scores(qt, kj, buf):
        k_tile = k_ref[0, 0, pl.ds(aligned(kj * tk, tk), tk), :]
        s = _dot(k_tile, q_ref[0, 0, qt])
        s_ref[buf] = s
        mx_ref[buf] = jnp.max(fold_rows(s, jnp.maximum), axis=0, keepdims=True)

    def start():
        m_ref[...] = jnp.full(m_ref.shape, -jnp.inf, F32)
        acc_ref[...] = jnp.zeros(acc_ref.shape, F32)

    def finish(base):
        for sub in range(ATT_SPLIT):
            acc = acc_ref[sub]
            o_ref[0, pl.ds(aligned(base + sub * tqs, tqs), tqs), :] = (
                (acc[:MLA_V] / acc[MLA_V:MLA_V + 1]).T.astype(o_ref.dtype))

    def absorb(kj, buf, kind):
        for sub in range(ATT_SPLIT):
            cols = slice(sub * tqs, (sub + 1) * tqs)
            keys = tk if kind == "full" else first_query[kind] + (sub + 1) * tqs
            vt_tile = vt_ref[kj, :, :keys]
            ss = s_ref[buf, :keys, cols]
            if kind == "full":
                tile_max = mx_ref[buf, :, cols]
            else:
                ss = jnp.where(allowed[kind][sub][:keys], ss, -jnp.inf)
                tile_max = jnp.max(fold_rows(ss, jnp.maximum), axis=0, keepdims=True)
            m = m_ref[sub]
            m_new = jnp.maximum(m, tile_max)
            alpha = jnp.exp2(m - m_new)
            p = jnp.exp2(ss - m_new)
            m_ref[sub] = m_new
            acc_ref[sub] = alpha * acc_ref[sub] + _dot(vt_tile, p.astype(BF16))

    def query_tile(qi):
        count = qi // 2 + 1
        start()
        for j in range(count):
            if j + 1 < count:
                scores(qi, j + 1, (j + 1) % 2)
            elif qi + 1 < n_q:
                scores(qi + 1, 0, home(qi + 1))
            absorb(j, home(qi) if j == 0 else j % 2, "full" if j + 1 < count else ("whole" if qi % 2 else "half"))
        finish(qi * tq)

    def home(qi):
        return {0: 1, 1: 0}.get(qi, 2)

    blocks, size = [[]], 0
    for qi in range(n_q):
        if blocks[-1] and size + qi // 2 + 1 > ATT_BLOCK_TILES:
            blocks.append([])
            size = 0
        blocks[-1].append(qi)
        size += qi // 2 + 1

    def block_step(i, _):
        for j, tiles in enumerate(blocks):
            @pl.when(i == j)
            def _(tiles=tiles):
                for qi in tiles:
                    query_tile(qi)
        return 0

    scores(0, 0, home(0))
    lax.fori_loop(0, len(blocks), block_step, 0)


def _attention(q, k, v, *, batch, seq, heads):
    qt = pl.BlockSpec((1, 1, seq // ATT_TILE, MLA_HEAD_PAD, ATT_TILE), lambda b, h: (b, h, 0, 0, 0))
    kk = pl.BlockSpec((1, 1, seq, MLA_HEAD_PAD), lambda b, h: (b, h, 0, 0))
    vv = pl.BlockSpec((1, 1, seq, MLA_V), lambda b, h: (b, h, 0, 0))
    vo = pl.BlockSpec((1, seq, MLA_V), lambda b, h: (b, 0, h))
    o = pl.pallas_call(
        _attn_body,
        grid=(batch, heads),
        in_specs=[qt, kk, vv],
        out_specs=vo,
        out_shape=jax.ShapeDtypeStruct((batch, seq, heads * MLA_V), BF16),
        scratch_shapes=[pltpu.VMEM((seq // ATT_KEYS, MLA_V + ATT_ONES, ATT_KEYS), BF16),
                        pltpu.VMEM((3, ATT_KEYS, ATT_TILE), F32),
                        pltpu.VMEM((3, 1, ATT_TILE), F32),
                        pltpu.VMEM((ATT_SPLIT, 1, ATT_TILE // ATT_SPLIT), F32),
                        pltpu.VMEM((ATT_SPLIT, MLA_V + ATT_ONES, ATT_TILE // ATT_SPLIT), F32)],
        compiler_params=_params(("parallel", "parallel")),
        name="attn",
    )(q, k, v)
    return o.reshape(batch * seq, heads * MLA_V)


def _merge_body(yh_ref, ym_ref, gate_ref, h_ref, wh_ref, wm_ref, wo_ref, o_ref):
    d = h_ref.shape[1]
    mix = (gate_ref[:, :d].astype(F32) * _dot(yh_ref[...], wh_ref[...])
           + gate_ref[:, d:].astype(F32) * _dot(ym_ref[...], wm_ref[...]))
    o_ref[...] = h_ref[...] + _dot(mix.astype(BF16), wo_ref[...])


def _merge(yh, ym, gates, h, w_h, w_m, w_o):
    t, d = h.shape
    tm = MERGE_TILE

    def row(n):
        return pl.BlockSpec((tm, n), lambda i: (i, 0))

    return pl.pallas_call(
        _merge_body,
        grid=(t // tm,),
        in_specs=[row(yh.shape[1]), row(ym.shape[1]), row(2 * d), row(d), _resident(w_h.shape),
                  _resident(w_m.shape), _resident(w_o.shape)],
        out_specs=row(d),
        out_shape=jax.ShapeDtypeStruct((t, d), F32),
        compiler_params=_params(("parallel",)),
        name="merge",
    )(yh, ym, gates, h, w_h.astype(BF16), w_m.astype(BF16), w_o.astype(BF16))


def kernel(x, positions, ffn1_norm, ffn1_w_in, ffn1_w_out, mix_norm, w_in, hg_lb_table, hg_out_norm, w_hg_branch, mla_q_lora_norm, w_q_up, mla_kv_lora_norm, w_kv_up, q_head_norm, k_head_norm, w_mla_branch, w_merge, b_merge, w_out, ffn2_norm, ffn2_w_in, ffn2_w_out, final_norm):
    batch, seq, d = x.shape
    depth = ffn1_norm.shape[0]
    hg_width = hg_lb_table.shape[1]
    heads = w_q_up.shape[2] // MLA_QK
    assert hg_out_norm.shape[1] == HG_HEAD and w_hg_branch.shape[1] == hg_width
    assert q_head_norm.shape[1] == MLA_QK and w_kv_up.shape[2] == heads * (MLA_NOPE + MLA_V)
    assert seq % max(HG_BLOCK, ATT_KEYS) == 0 and (batch * seq) % max(MERGE_TILE, FFN_TILE) == 0
    t = batch * seq
    pos = positions.reshape(t, 1)
    h = x.reshape(t, d)
    for l in range(depth):
        h1, u = _ffn(h, ffn1_norm[l], ffn1_w_in[l], ffn1_w_out[l], mix_norm[l], emit_h=True, norm_dtype=BF16)
        hq, lf, kk, hv, hg, gates, q, k, v = _proj(
            u, pos, w_in[l], w_merge[l], b_merge[l], hg_lb_table, mla_q_lora_norm[l], w_q_up[l],
            mla_kv_lora_norm[l], w_kv_up[l], q_head_norm[l], k_head_norm[l], layer=l, hg_width=hg_width,
            heads=heads, seq=seq)
        y_hg = _hgrn(hq, lf, kk, hv, hg, hg_out_norm[l], batch=batch, seq=seq)
        y_mla = _attention(q, k, v, batch=batch, seq=seq, heads=heads)
        h2 = _merge(y_hg, y_mla, gates, h1, w_hg_branch[l], w_mla_branch[l], w_out[l])
        (h,) = _ffn(h2, ffn2_norm[l], ffn2_w_in[l], ffn2_w_out[l], final_norm[l], emit_h=False, norm_dtype=F32)
    return h.reshape(batch, seq, d)
```
